```python
import math
import jax
import jax.numpy as jnp
from jax import lax
import numpy as np

D_MODEL = 2048
BATCH = 2
SEQ = 4096
DEPTH = 2

CHUNK = 64
EPS = 1e-6
A_HEADS = 8
A_DK = 128
A_DV = 128
A_WIDTH = A_HEADS * A_DV
CONV_W = 4
GMLP_CHUNK = 128
GMLP_GROUPS = 8
GMLP_GDIM = 128
GMLP_WIDTH = GMLP_GROUPS * GMLP_GDIM
C_HEADS = 8
C_DH = 128
C_WIDTH = C_HEADS * C_DH
LEFT_CHUNKS = 8
BAND = (LEFT_CHUNKS + 1) * CHUNK
MAX_REL = 128
N_BRANCH = 3
IN_COLS = 4 * A_WIDTH + 2 * A_HEADS + 2 * GMLP_WIDTH + 3 * C_WIDTH + N_BRANCH * D_MODEL
PEER_HEADS = 8
PEER_NKEYS = 128
PEER_EXPERTS = PEER_NKEYS * PEER_NKEYS
PEER_TOPK = 16
PEER_QDIM = 256
PEER_QHALF = PEER_QDIM // 2
PEER_BLOCK = 128

kernel_name = "hybrid_deltanet_gmlp_bandattn_peer"


def rmsnorm(x, gain):
    xf = x.astype(jnp.float32)
    y = xf * lax.rsqrt(jnp.mean(xf * xf, axis=-1, keepdims=True) + EPS)
    return (y * gain.astype(jnp.float32)).astype(x.dtype)


def l2norm(x):
    return x * lax.rsqrt(jnp.sum(x * x, axis=-1, keepdims=True) + EPS)


def split_cols(t, sizes):
    parts, start = [], 0
    for size in sizes:
        parts.append(t[..., start:start + size])
        start += size
    return parts


def causal_conv(x, w):
    s = x.shape[1]
    xp = jnp.pad(x, ((0, 0), (CONV_W - 1, 0), (0, 0)))
    y = xp[:, 0:s] * w[0]
    for i in range(1, CONV_W):
        y = y + xp[:, i:i + s] * w[i]
    return y


def gated_delta_rule(q, k, v, beta, g):
    B, S, H, DK = q.shape
    DV = v.shape[-1]
    N = S // CHUNK

    def chunks(t):
        t = t.reshape(B, N, CHUNK, H, *t.shape[3:])
        return jnp.moveaxis(jnp.moveaxis(t, 3, 2), 1, 0)

    qc, kc, vc, bc, gc = (chunks(t) for t in (q, k, v, beta, g))
    gcum = jnp.cumsum(gc, axis=-1)
    causal = jnp.tril(jnp.ones((CHUNK, CHUNK), dtype=bool))
    strict = jnp.tril(jnp.ones((CHUNK, CHUNK), dtype=bool), -1)
    decay = jnp.exp(jnp.where(causal, gcum[..., :, None] - gcum[..., None, :], -jnp.inf))
    kb = kc * bc[..., None]
    a_strict = jnp.where(strict, jnp.einsum('nbhid,nbhjd->nbhij', kb, kc) * decay, 0.0)
    lhs = a_strict + jnp.eye(CHUNK, dtype=a_strict.dtype)
    rhs = jnp.concatenate([vc * bc[..., None], kb * jnp.exp(gcum)[..., None]], axis=-1)
    sol = lax.linalg.triangular_solve(lhs, rhs, left_side=True, lower=True, unit_diagonal=True)
    u, w = sol[..., :DV], sol[..., DV:]
    attn = jnp.einsum('nbhid,nbhjd->nbhij', qc, kc) * decay

    def step(state, inp):
        q_n, k_n, u_n, w_n, g_n, attn_n = inp
        v_new = u_n - jnp.einsum('bhck,bhkv->bhcv', w_n, state)
        o = (jnp.einsum('bhck,bhkv->bhcv', q_n * jnp.exp(g_n)[..., None], state)
             + jnp.einsum('bhij,bhjv->bhiv', attn_n, v_new))
        g_last = g_n[..., -1:]
        k_dec = k_n * jnp.exp(g_last - g_n)[..., None]
        state = state * jnp.exp(g_last)[..., None] + jnp.einsum('bhck,bhcv->bhkv', k_dec, v_new)
        return state, o

    state0 = jnp.zeros((B, H, DK, DV), jnp.float32)
    _, o = lax.scan(step, state0, (qc, kc, u, w, gcum, attn))
    return jnp.moveaxis(o, 0, 1).transpose(0, 1, 3, 2, 4).reshape(B, S, H, DV)


def gated_deltanet(q, k, v, z, beta_logit, a_logit, a_log, dt_bias, out_gain):
    dtype = v.dtype
    f32 = jnp.float32
    q = l2norm(q.astype(f32)) * (A_DK ** -0.5)
    k = l2norm(k.astype(f32))
    beta = jax.nn.sigmoid(beta_logit.astype(f32))
    g = -jnp.exp(a_log.astype(f32)) * jax.nn.softplus(a_logit.astype(f32) + dt_bias.astype(f32))
    o = gated_delta_rule(q, k, v.astype(f32), beta, g)
    o = rmsnorm(o, out_gain) * jax.nn.silu(z.astype(f32))
    B, S = o.shape[:2]
    return o.reshape(B, S, A_WIDTH).astype(dtype)


def chunked_gmlp(u, v, norm_gain, w_spatial, b_spatial):
    B, S, _ = u.shape
    u = jax.nn.gelu(u)
    v = rmsnorm(jax.nn.gelu(v), norm_gain)
    vc = v.reshape(B, S // GMLP_CHUNK, GMLP_CHUNK, GMLP_GROUPS, GMLP_GDIM)
    w = jnp.where(jnp.tril(jnp.ones((GMLP_CHUNK, GMLP_CHUNK), dtype=bool)), w_spatial, 0.0)
    mixed = jnp.einsum('gts,bnsgc->bntgc', w, vc) + b_spatial.T[:, :, None]
    return u * mixed.reshape(B, S, GMLP_WIDTH)


def band_attention(q, k, v, q_gain, k_gain, rel_bias):
    B, S, H, Dh = q.shape
    N = S // CHUNK
    pad = LEFT_CHUNKS * CHUNK
    q = rmsnorm(q, q_gain).reshape(B, N, CHUNK, H, Dh)
    k = jnp.pad(rmsnorm(k, k_gain), ((0, 0), (pad, 0), (0, 0), (0, 0)))
    v = jnp.pad(v, ((0, 0), (pad, 0), (0, 0), (0, 0)))
    idx = jnp.arange(N)[:, None] * CHUNK + jnp.arange(BAND)[None, :]
    k_band = k[:, idx]
    v_band = v[:, idx]
    s = jnp.einsum('bnqhd,bnkhd->bhnqk', q, k_band).astype(jnp.float32) * (Dh ** -0.5)
    rel = jnp.arange(CHUNK)[:, None] - jnp.arange(BAND)[None, :] + pad
    bias = rel_bias.astype(jnp.float32)[:, jnp.clip(rel, -MAX_REL, MAX_REL) + MAX_REL]
    valid = idx >= pad
    s = jnp.where(valid[None, None, :, None, :], s + bias[None, :, None], -jnp.inf)
    p = jax.nn.softmax(s, axis=-1).astype(v.dtype)
    o = jnp.einsum('bhnqk,bnkhd->bnqhd', p, v_band)
    return o.reshape(B, S, H * Dh)


def peer_ffn(x, wq, keys, u_tab, v_tab):
    B, S, D = x.shape
    K = PEER_TOPK
    xt = x.reshape(-1, PEER_BLOCK, D)

    def one_block(xb):
        q = (xb @ wq).reshape(PEER_BLOCK, PEER_HEADS, 2, PEER_QHALF)
        scores = jnp.einsum('thpc,hpkc->thpk', q, keys)
        top_v, top_i = lax.top_k(scores, K)
        cand = top_v[:, :, 0, :, None] + top_v[:, :, 1, None, :]
        best_v, best_f = lax.top_k(cand.reshape(PEER_BLOCK, PEER_HEADS, K * K), K)
        i1 = jnp.take_along_axis(top_i[:, :, 0], best_f // K, axis=-1)
        i2 = jnp.take_along_axis(top_i[:, :, 1], best_f % K, axis=-1)
        expert = i1 * PEER_NKEYS + i2
        gate = jax.nn.softmax(best_v.astype(jnp.float32), axis=-1).astype(xb.dtype)
        hidden = jax.nn.gelu(jnp.einsum('thkd,td->thk', u_tab[expert], xb))
        return jnp.einsum('thk,thkd->td', gate * hidden, v_tab[expert])

    return lax.map(one_block, xt).reshape(B, S, D)


def setup_inputs(seed: int = 0) -> dict:
    key = jax.random.key(seed)
    ks = jax.random.split(key, 24)
    f32 = jnp.float32
    nrm = jax.random.normal

    def gain(k, n):
        return 1.0 + 0.01 * nrm(k, (DEPTH, n), f32)

    dt = jnp.exp(jax.random.uniform(ks[4], (DEPTH, A_HEADS), f32, math.log(1e-3), math.log(1e-1)))
    return {
        'x': nrm(ks[0], (BATCH, SEQ, D_MODEL), f32),
        'w_in': nrm(ks[1], (DEPTH, D_MODEL, IN_COLS), f32) * D_MODEL ** -0.5,
        'conv_w': nrm(ks[2], (DEPTH, CONV_W, 3 * A_WIDTH), f32) * CONV_W ** -0.5,
        'a_log': jnp.log(jax.random.uniform(ks[3], (DEPTH, A_HEADS), f32, 1.0, 16.0)),
        'dt_bias': dt + jnp.log(-jnp.expm1(-dt)),
        'a_out_gain': gain(ks[5], A_DV),
        'gmlp_norm': gain(ks[6], GMLP_WIDTH),
        'w_spatial': nrm(ks[7], (DEPTH, GMLP_GROUPS, GMLP_CHUNK, GMLP_CHUNK), f32) * 0.5 * GMLP_CHUNK ** -0.5,
        'b_spatial': 1.0 + 0.01 * nrm(ks[8], (DEPTH, GMLP_GROUPS, GMLP_CHUNK), f32),
        'c_q_gain': gain(ks[9], C_DH),
        'c_k_gain': gain(ks[10], C_DH),
        'rel_bias': 0.5 * nrm(ks[11], (DEPTH, C_HEADS, 2 * MAX_REL + 1), f32),
        'p_a': nrm(ks[12], (DEPTH, A_WIDTH, D_MODEL), f32) * A_WIDTH ** -0.5,
        'p_b': nrm(ks[13], (DEPTH, GMLP_WIDTH, D_MODEL), f32) * GMLP_WIDTH ** -0.5,
        'p_c': nrm(ks[14], (DEPTH, C_WIDTH, D_MODEL), f32) * C_WIDTH ** -0.5,
        'w_out': nrm(ks[15], (DEPTH, D_MODEL, D_MODEL), f32) * D_MODEL ** -0.5,
        'norm_mix': gain(ks[16], D_MODEL),
        'norm_ffn': gain(ks[17], D_MODEL),
        'peer_wq': nrm(ks[18], (DEPTH, D_MODEL, PEER_HEADS * PEER_QDIM), f32) * D_MODEL ** -0.5,
        'peer_keys': nrm(ks[19], (DEPTH, PEER_HEADS, 2, PEER_NKEYS, PEER_QHALF), f32) * PEER_QHALF ** -0.5,
        'peer_u': nrm(ks[20], (DEPTH, PEER_EXPERTS, D_MODEL), f32) * D_MODEL ** -0.5,
        'peer_v': nrm(ks[21], (DEPTH, PEER_EXPERTS, D_MODEL), f32) * PEER_HEADS ** -0.5,
    }


def reference(x, w_in, conv_w, a_log, dt_bias, a_out_gain, gmlp_norm, w_spatial, b_spatial,
              c_q_gain, c_k_gain, rel_bias, p_a, p_b, p_c, w_out, norm_mix, norm_ffn,
              peer_wq, peer_keys, peer_u, peer_v):
    B, S, _ = x.shape
    for l in range(DEPTH):
        h = rmsnorm(x, norm_mix[l])
        proj = h @ w_in[l]
        qkv_a, z_a, beta_a, alpha_a, u_b, v_b, qkv_c, gate = split_cols(
            proj, (3 * A_WIDTH, A_WIDTH, A_HEADS, A_HEADS, GMLP_WIDTH, GMLP_WIDTH, 3 * C_WIDTH, N_BRANCH * D_MODEL))
        qkv_a = jax.nn.silu(causal_conv(qkv_a, conv_w[l]))
        q_a, k_a, v_a = (t.reshape(B, S, A_HEADS, -1) for t in jnp.split(qkv_a, 3, axis=-1))
        y_a = gated_deltanet(q_a, k_a, v_a, z_a.reshape(B, S, A_HEADS, A_DV), beta_a, alpha_a,
                             a_log[l], dt_bias[l], a_out_gain[l])
        y_b = chunked_gmlp(u_b, v_b, gmlp_norm[l], w_spatial[l], b_spatial[l])
        q_c, k_c, v_c = (t.reshape(B, S, C_HEADS, C_DH) for t in jnp.split(qkv_c, 3, axis=-1))
        y_c = band_attention(q_c, k_c, v_c, c_q_gain[l], c_k_gain[l], rel_bias[l])
        g = jax.nn.sigmoid(gate.astype(jnp.float32)).astype(x.dtype).reshape(B, S, N_BRANCH, D_MODEL)
        merged = (g[:, :, 0] * (y_a @ p_a[l]) + g[:, :, 1] * (y_b @ p_b[l])
                  + g[:, :, 2] * (y_c @ p_c[l]))
        x = x + merged @ w_out[l]
        x = x + peer_ffn(rmsnorm(x, norm_ffn[l]), peer_wq[l], peer_keys[l], peer_u[l], peer_v[l])
    return x
```

```python
import functools
import math

import jax
import jax.numpy as jnp
from jax import lax
from jax.experimental import pallas as pl
from jax.experimental.pallas import tpu as pltpu

F32 = jnp.float32
BF16 = jnp.bfloat16
HIGHEST = lax.Precision.HIGHEST

CHUNK = 64
EPS = 1e-6
A_HEADS = 8
A_DK = 128
A_DV = 128
A_WIDTH = A_HEADS * A_DV
CONV_W = 4
GMLP_CHUNK = 128
GMLP_GROUPS = 8
GMLP_GDIM = 128
GMLP_WIDTH = GMLP_GROUPS * GMLP_GDIM
C_HEADS = 8
C_DH = 128
C_WIDTH = C_HEADS * C_DH
LEFT_CHUNKS = 8
BAND = (LEFT_CHUNKS + 1) * CHUNK
MAX_REL = 128
N_BRANCH = 3
PEER_HEADS = 8
PEER_NKEYS = 128
PEER_TOPK = 16
PEER_QDIM = 256
PEER_QHALF = PEER_QDIM // 2

LANES = 128
NEG_BIG = -1e30
VMEM_LIMIT = 56 * 1024 * 1024


def _cparams(sem):
    return pltpu.CompilerParams(dimension_semantics=sem, vmem_limit_bytes=VMEM_LIMIT)


def _nt_dot(a, b, precision=None):
    return lax.dot_general(a, b, (((1,), (1,)), ((), ())), precision=precision,
                           preferred_element_type=F32)


def _tn_dot(a, b, precision=None):
    return lax.dot_general(a, b, (((0,), (0,)), ((), ())), precision=precision,
                           preferred_element_type=F32)


def _pick(n, pref):
    t = min(pref, n)
    while n % t:
        t -= LANES if t > LANES else 8
    return t


def _rmsnorm_kernel(x_ref, g_ref, o_ref):
    x = x_ref[...]
    ms = jnp.mean(x * x, axis=-1, keepdims=True)
    o_ref[...] = (x * lax.rsqrt(ms + EPS) * g_ref[...]).astype(o_ref.dtype)


def _rmsnorm(x, gain, out_dtype=BF16, tm=512):
    m, d = x.shape
    tm = _pick(m, tm)
    return pl.pallas_call(
        _rmsnorm_kernel,
        grid=(m // tm,),
        in_specs=[pl.BlockSpec((tm, d), lambda i: (i, 0)), pl.BlockSpec((1, d), lambda i: (0, 0))],
        out_specs=pl.BlockSpec((tm, d), lambda i: (i, 0)),
        out_shape=jax.ShapeDtypeStruct((m, d), out_dtype),
        compiler_params=_cparams(("parallel",)),
        name="rmsnorm",
    )(x, gain.reshape(1, d))


def _mm_kernel(a_ref, b_ref, o_ref):
    o_ref[...] = jnp.dot(a_ref[...], b_ref[...], preferred_element_type=F32).astype(o_ref.dtype)


def _mm_res_kernel(a_ref, b_ref, r_ref, o_ref):
    o_ref[...] = r_ref[...] + jnp.dot(a_ref[...], b_ref[...], preferred_element_type=F32)


def _matmul(a, b, out_dtype, residual=None, tm=1024, tn=1024, name="matmul"):
    m, k = a.shape
    n = b.shape[1]
    tm, tn = _pick(m, tm), _pick(n, tn)
    in_specs = [pl.BlockSpec((tm, k), lambda j, i: (i, 0)), pl.BlockSpec((k, tn), lambda j, i: (0, j))]
    args = [a, b]
    kern = _mm_kernel
    if residual is not None:
        in_specs.append(pl.BlockSpec((tm, tn), lambda j, i: (i, j)))
        args.append(residual)
        kern = _mm_res_kernel
    return pl.pallas_call(
        kern,
        grid=(n // tn, m // tm),
        in_specs=in_specs,
        out_specs=pl.BlockSpec((tm, tn), lambda j, i: (i, j)),
        out_shape=jax.ShapeDtypeStruct((m, n), out_dtype),
        compiler_params=_cparams(("parallel", "parallel")),
        name=name,
    )(*args)


def _conv_prep_kernel(cur_ref, prev_ref, w_ref, o_ref, ext_ref, *, ts, tc, prev_rows, n_qk_tiles, n_q_tiles):
    i = pl.program_id(1)
    c = pl.program_id(2)
    prev = prev_ref[...].astype(F32)
    prev = jnp.where(i == 0, 0.0, prev)
    cur = cur_ref[...].astype(F32)
    ext_ref[0:prev_rows, :] = prev
    ext_ref[prev_rows:prev_rows + ts, :] = cur
    w = w_ref[...]
    acc = cur * w[CONV_W - 1:CONV_W, :]
    for d in range(1, CONV_W):
        acc = acc + ext_ref[prev_rows - d:prev_rows - d + ts, :] * w[CONV_W - 1 - d:CONV_W - d, :]
    y = acc * jax.nn.sigmoid(acc)
    parts = []
    for h in range(tc // A_DK):
        yh = y[:, h * A_DK:(h + 1) * A_DK]
        parts.append(yh * lax.rsqrt(jnp.sum(yh * yh, axis=-1, keepdims=True) + EPS))
    yn = jnp.concatenate(parts, axis=-1)
    scale = jnp.where(c < n_q_tiles, A_DK ** -0.5, 1.0).astype(F32)
    o_ref[...] = jnp.where(c < n_qk_tiles, yn * scale, y)


def _conv_prep(qkv, conv_w, batch, seq, ts=512, tc=512):
    t, ch = qkv.shape
    ts = _pick(seq, ts)
    prev_rows = 16
    n_s = seq // ts
    kern = functools.partial(_conv_prep_kernel, ts=ts, tc=tc, prev_rows=prev_rows,
                             n_qk_tiles=2 * A_WIDTH // tc, n_q_tiles=A_WIDTH // tc)
    rpb = ts // prev_rows
    return pl.pallas_call(
        kern,
        grid=(batch, n_s, ch // tc),
        in_specs=[
            pl.BlockSpec((ts, tc), lambda b, i, c: (b * n_s + i, c)),
            pl.BlockSpec((prev_rows, tc), lambda b, i, c: (jnp.maximum((b * n_s + i) * rpb - 1, 0), c)),
            pl.BlockSpec((CONV_W, tc), lambda b, i, c: (0, c)),
        ],
        out_specs=pl.BlockSpec((ts, tc), lambda b, i, c: (b * n_s + i, c)),
        out_shape=jax.ShapeDtypeStruct((t, ch), F32),
        scratch_shapes=[pltpu.VMEM((prev_rows + ts, tc), F32)],
        compiler_params=_cparams(("parallel", "parallel", "parallel")),
        name="conv_prep",
    )(qkv, qkv, conv_w)


def _gates_kernel(ba_ref, alog_ref, dtb_ref, g_ref, b_ref, *, ts):
    ba = ba_ref[...]
    lane = lax.broadcasted_iota(jnp.int32, (ts, LANES), 1)
    head_lane = lane < A_HEADS
    beta = jnp.where(head_lane, jax.nn.sigmoid(ba[:, :LANES]), 0.0)
    g = -jnp.exp(alog_ref[...]) * jax.nn.softplus(ba[:, LANES:] + dtb_ref[...])
    g = jnp.where(head_lane, g, 0.0)
    r = lax.broadcasted_iota(jnp.int32, (ts, ts), 0)
    c = lax.broadcasted_iota(jnp.int32, (ts, ts), 1)
    shift = CHUNK.bit_length() - 1
    tri = jnp.where((c <= r) & ((r >> shift) == (c >> shift)), 1.0, 0.0).astype(F32)
    gcum = jnp.dot(tri, g, precision=HIGHEST, preferred_element_type=F32)
    er = lax.broadcasted_iota(jnp.int32, (LANES, A_WIDTH), 0)
    ec = lax.broadcasted_iota(jnp.int32, (LANES, A_WIDTH), 1)
    spread = jnp.where(er == (ec >> (A_DV.bit_length() - 1)), 1.0, 0.0).astype(F32)
    g_ref[...] = jnp.dot(gcum, spread, precision=HIGHEST, preferred_element_type=F32)
    b_ref[...] = jnp.dot(beta, spread, precision=HIGHEST, preferred_element_type=F32)


def _gates(ba, a_log, dt_bias, ts=512):
    t = ba.shape[0]
    ts = _pick(t, ts)
    pad = lambda v: jnp.pad(v.astype(F32), (0, LANES - A_HEADS)).reshape(1, LANES)
    out = jax.ShapeDtypeStruct((t, A_WIDTH), F32)
    return pl.pallas_call(
        functools.partial(_gates_kernel, ts=ts),
        grid=(t // ts,),
        in_specs=[pl.BlockSpec((ts, 2 * LANES), lambda i: (i, 0)),
                  pl.BlockSpec((1, LANES), lambda i: (0, 0)), pl.BlockSpec((1, LANES), lambda i: (0, 0))],
        out_specs=[pl.BlockSpec((ts, A_WIDTH), lambda i: (i, 0)), pl.BlockSpec((ts, A_WIDTH), lambda i: (i, 0))],
        out_shape=[out, out],
        compiler_params=_cparams(("parallel",)),
        name="gates",
    )(ba, pad(a_log), pad(dt_bias))


def _delta_kernel(q_ref, k_ref, v_ref, g_ref, b_ref, z_ref, gain_ref, o_ref, s_ref, *, ts, hp):
    @pl.when(pl.program_id(2) == 0)
    def _():
        s_ref[...] = jnp.zeros_like(s_ref)

    ri = lax.broadcasted_iota(jnp.int32, (CHUNK, CHUNK), 0)
    ci = lax.broadcasted_iota(jnp.int32, (CHUNK, CHUNK), 1)
    eye = ri == ci
    causal = ri >= ci
    strict = ri > ci
    ones = jnp.ones((CHUNK, CHUNK), F32)
    ident = jnp.where(eye, 1.0, 0.0).astype(F32)
    gain = gain_ref[...]

    def hdot(a, b):
        return jnp.dot(a, b, precision=HIGHEST, preferred_element_type=F32)

    def bdot(a, b):
        return jnp.dot(a.astype(BF16), b.astype(BF16), preferred_element_type=F32)

    def body(n, carry):
        rows = pl.ds(pl.multiple_of(n * CHUNK, CHUNK), CHUNK)
        for h in range(hp):
            cols = slice(h * A_DK, (h + 1) * A_DK)
            q = q_ref[rows, cols]
            k = k_ref[rows, cols]
            v = v_ref[rows, cols]
            gb = g_ref[rows, cols]
            bt = b_ref[rows, cols]
            kb = k * bt
            vb = v * bt
            gi = gb[:, :CHUNK]
            gj = hdot(ones, jnp.where(eye, gi, 0.0))
            decay = jnp.exp(jnp.where(causal, gi - gj, NEG_BIG))
            a = jnp.where(strict, _nt_dot(kb.astype(BF16), k.astype(BF16)) * decay, 0.0)
            attn = _nt_dot(q.astype(BF16), k.astype(BF16)) * decay
            inv = ident - a
            p = hdot(a, a)
            inv = inv + hdot(inv, p)
            for _ in range(4):
                p = hdot(p, p)
                inv = inv + hdot(inv, p)
            eg = jnp.exp(gb)
            u = bdot(inv, vb)
            w = bdot(inv, kb * eg)
            s = s_ref[h]
            v_new = u - bdot(w, s)
            o = bdot(q * eg, s) + bdot(attn, v_new)
            g_last = gb[CHUNK - 1:CHUNK, :]
            k_dec = k * jnp.exp(g_last - gb)
            s_ref[h] = s * jnp.exp(g_last) + _tn_dot(k_dec.astype(BF16), v_new.astype(BF16))
            on = o * lax.rsqrt(jnp.mean(o * o, axis=-1, keepdims=True) + EPS) * gain
            z = z_ref[rows, cols].astype(F32)
            o_ref[rows, cols] = (on * (z * jax.nn.sigmoid(z))).astype(o_ref.dtype)
        return carry

    lax.fori_loop(0, ts // CHUNK, body, 0)


def _delta_rule(qkv, gcum, beta, z, out_gain, batch, seq, ts=512, hp=4):
    t = qkv.shape[0]
    ts = _pick(seq, ts)
    n_s = seq // ts
    hw = hp * A_DK
    n_hb = A_WIDTH // hw
    row = lambda b, hb, i: b * n_s + i
    spec = lambda off: pl.BlockSpec((ts, hw), lambda b, hb, i: (row(b, hb, i), off * n_hb + hb))
    return pl.pallas_call(
        functools.partial(_delta_kernel, ts=ts, hp=hp),
        grid=(batch, n_hb, n_s),
        in_specs=[spec(0), spec(1), spec(2), spec(0), spec(0), spec(0),
                  pl.BlockSpec((1, A_DV), lambda b, hb, i: (0, 0))],
        out_specs=spec(0),
        out_shape=jax.ShapeDtypeStruct((t, A_WIDTH), BF16),
        scratch_shapes=[pltpu.VMEM((hp, A_DK, A_DV), F32)],
        compiler_params=_cparams(("parallel", "parallel", "arbitrary")),
        name="delta_rule",
    )(qkv, qkv, qkv, gcum, beta, z, out_gain.reshape(1, A_DV).astype(F32))


def _gmlp_kernel(u_ref, v_ref, gain_ref, w_ref, bias_ref, o_ref, *, nb):
    ri = lax.broadcasted_iota(jnp.int32, (GMLP_CHUNK, GMLP_CHUNK), 0)
    ci = lax.broadcasted_iota(jnp.int32, (GMLP_CHUNK, GMLP_CHUNK), 1)
    tril = ri >= ci
    gain = gain_ref[...]
    for blk in range(nb):
        rows = slice(blk * GMLP_CHUNK, (blk + 1) * GMLP_CHUNK)
        u = jax.nn.gelu(u_ref[rows, :].astype(F32))
        v = jax.nn.gelu(v_ref[rows, :].astype(F32))
        vn = (v * lax.rsqrt(jnp.mean(v * v, axis=-1, keepdims=True) + EPS) * gain).astype(BF16)
        for g in range(GMLP_GROUPS):
            cols = slice(g * GMLP_GDIM, (g + 1) * GMLP_GDIM)
            w = jnp.where(tril, w_ref[g], 0.0).astype(BF16)
            mixed = jnp.dot(w, vn[:, cols], preferred_element_type=F32) + bias_ref[:, cols]
            o_ref[rows, cols] = (u[:, cols] * mixed).astype(o_ref.dtype)


def _gmlp(uv, norm_gain, w_spatial, b_spatial, nb=4):
    t = uv.shape[0]
    tm = nb * GMLP_CHUNK
    bias = jnp.repeat(b_spatial.T.astype(F32), GMLP_GDIM, axis=1)
    return pl.pallas_call(
        functools.partial(_gmlp_kernel, nb=nb),
        grid=(t // tm,),
        in_specs=[pl.BlockSpec((tm, GMLP_WIDTH), lambda i: (i, 0)),
                  pl.BlockSpec((tm, GMLP_WIDTH), lambda i: (i, 1)),
                  pl.BlockSpec((1, GMLP_WIDTH), lambda i: (0, 0)),
                  pl.BlockSpec((GMLP_GROUPS, GMLP_CHUNK, GMLP_CHUNK), lambda i: (0, 0, 0)),
                  pl.BlockSpec((GMLP_CHUNK, GMLP_WIDTH), lambda i: (0, 0))],
        out_specs=pl.BlockSpec((tm, GMLP_WIDTH), lambda i: (i, 0)),
        out_shape=jax.ShapeDtypeStruct((t, GMLP_WIDTH), BF16),
        compiler_params=_cparams(("parallel",)),
        name="gmlp",
    )(uv, uv, norm_gain.reshape(1, GMLP_WIDTH).astype(F32), w_spatial.astype(F32), bias)


def _band_kernel(q_ref, kp_ref, kc_ref, vp_ref, vc_ref, qg_ref, kg_ref, bias_ref, o_ref, *, tq):
    i = pl.program_id(2)

    def norm(x, gain):
        x = x.astype(F32)
        return x * lax.rsqrt(jnp.mean(x * x, axis=-1, keepdims=True) + EPS) * gain

    qn = norm(q_ref[...], qg_ref[...]).astype(BF16)
    kcat = jnp.concatenate([norm(kp_ref[...], kg_ref[...]), norm(kc_ref[...], kg_ref[...])], axis=0).astype(BF16)
    vcat = jnp.concatenate([vp_ref[...], vc_ref[...]], axis=0).astype(BF16)
    bias = bias_ref[0]
    pad = LEFT_CHUNKS * CHUNK
    kpos = lax.broadcasted_iota(jnp.int32, (CHUNK, BAND), 1)
    for c in range(tq // CHUNK):
        lo = c * CHUNK + (tq - pad)
        kw = kcat[lo:lo + BAND]
        vw = vcat[lo:lo + BAND]
        s = _nt_dot(qn[c * CHUNK:(c + 1) * CHUNK], kw) * (C_DH ** -0.5) + bias
        valid = (i > 0) | (kpos + c * CHUNK >= pad)
        s = jnp.where(valid, s, NEG_BIG)
        m = jnp.max(s, axis=-1, keepdims=True)
        p = jnp.exp(s - m)
        denom = jnp.sum(p, axis=-1, keepdims=True)
        o = jnp.dot((p / denom).astype(BF16), vw, preferred_element_type=F32)
        o_ref[c * CHUNK:(c + 1) * CHUNK, :] = o.astype(o_ref.dtype)


def _band_attention(qkv, q_gain, k_gain, rel_bias, batch, seq):
    t = qkv.shape[0]
    tq = LEFT_CHUNKS * CHUNK
    n_s = seq // tq
    rel = jnp.arange(CHUNK)[:, None] - jnp.arange(BAND)[None, :] + LEFT_CHUNKS * CHUNK
    bias = rel_bias.astype(F32)[:, jnp.clip(rel, -MAX_REL, MAX_REL) + MAX_REL]
    cur = lambda off: pl.BlockSpec((tq, C_DH), lambda b, h, i: (b * n_s + i, off * C_HEADS + h))
    prv = lambda off: pl.BlockSpec((tq, C_DH), lambda b, h, i: (b * n_s + jnp.maximum(i - 1, 0), off * C_HEADS + h))
    vec = pl.BlockSpec((1, C_DH), lambda b, h, i: (0, 0))
    return pl.pallas_call(
        functools.partial(_band_kernel, tq=tq),
        grid=(batch, C_HEADS, n_s),
        in_specs=[cur(0), prv(1), cur(1), prv(2), cur(2), vec, vec,
                  pl.BlockSpec((1, CHUNK, BAND), lambda b, h, i: (h, 0, 0))],
        out_specs=pl.BlockSpec((tq, C_DH), lambda b, h, i: (b * n_s + i, h)),
        out_shape=jax.ShapeDtypeStruct((t, C_WIDTH), BF16),
        compiler_params=_cparams(("parallel", "parallel", "parallel")),
        name="band_attention",
    )(qkv, qkv, qkv, qkv, qkv, q_gain.reshape(1, C_DH).astype(F32), k_gain.reshape(1, C_DH).astype(F32), bias)


def _merge_kernel(ya_ref, yb_ref, yc_ref, pa_ref, pb_ref, pc_ref, ga_ref, gb_ref, gc_ref, o_ref):
    def branch(y_ref, p_ref, g_ref):
        gate = jax.nn.sigmoid(g_ref[...].astype(F32))
        return gate * jnp.dot(y_ref[...], p_ref[...], preferred_element_type=F32)

    merged = branch(ya_ref, pa_ref, ga_ref) + branch(yb_ref, pb_ref, gb_ref) + branch(yc_ref, pc_ref, gc_ref)
    o_ref[...] = merged.astype(o_ref.dtype)


def _merge(ya, yb, yc, pa, pb, pc, gate, tm=1024, tn=512):
    t, k = ya.shape
    d = pa.shape[1]
    tm, tn = _pick(t, tm), _pick(d, tn)
    nd = d // tn
    ysp = pl.BlockSpec((tm, k), lambda j, i: (i, 0))
    psp = pl.BlockSpec((k, tn), lambda j, i: (0, j))
    gsp = lambda br: pl.BlockSpec((tm, tn), lambda j, i: (i, br * nd + j))
    return pl.pallas_call(
        _merge_kernel,
        grid=(nd, t // tm),
        in_specs=[ysp, ysp, ysp, psp, psp, psp, gsp(0), gsp(1), gsp(2)],
        out_specs=pl.BlockSpec((tm, tn), lambda j, i: (i, j)),
        out_shape=jax.ShapeDtypeStruct((t, d), BF16),
        compiler_params=_cparams(("parallel", "parallel")),
        name="merge",
    )(ya, yb, yc, pa, pb, pc, gate, gate, gate)


def _top_values(x, k, out_ref):
    cur = x
    for r in range(k):
        m = jnp.max(cur, axis=0, keepdims=True)
        out_ref[r:r + 1, :] = m
        if r + 1 < k:
            cur = jnp.where(cur == m, NEG_BIG, cur)


def _peer_select_kernel(q_ref, keys_ref, s1_ref, f1_ref, s2_ref, e2_ref, tau_ref, a_ref, b_ref, c_ref, *, tt):
    k = PEER_TOPK

    def head(h, carry):
        qh = q_ref[h]
        s1 = _nt_dot(keys_ref[h, 0], qh[:, :PEER_QHALF], precision=HIGHEST)
        s2 = _nt_dot(keys_ref[h, 1], qh[:, PEER_QHALF:], precision=HIGHEST)
        _top_values(s1, k, a_ref)
        _top_values(s2, k, b_ref)
        bv = b_ref[...]
        cand = jnp.concatenate([a_ref[r:r + 1, :] + bv for r in range(k)], axis=0)
        _top_values(cand, k, c_ref)
        cv = c_ref[...]
        z = jnp.sum(jnp.exp(cv - cv[0:1, :]), axis=0, keepdims=True)
        s1_ref[h] = s1
        s2_ref[h] = s2
        f1_ref[h] = jnp.exp(s1 - a_ref[0:1, :]) / z
        e2_ref[h] = jnp.exp(s2 - b_ref[0:1, :])
        tau_ref[pl.ds(h, 1), :] = cv[k - 1:k, :]
        return carry

    lax.fori_loop(0, PEER_HEADS, head, 0)


def _peer_select(q, keys, tt=256):
    t = q.shape[0]
    tt = _pick(t, tt)
    qh = q.reshape(t, PEER_HEADS, PEER_QDIM).transpose(1, 0, 2)
    big = jax.ShapeDtypeStruct((PEER_HEADS, PEER_NKEYS, t), F32)
    bspec = pl.BlockSpec((PEER_HEADS, PEER_NKEYS, tt), lambda i: (0, 0, i))
    return pl.pallas_call(
        functools.partial(_peer_select_kernel, tt=tt),
        grid=(t // tt,),
        in_specs=[pl.BlockSpec((PEER_HEADS, tt, PEER_QDIM), lambda i: (0, i, 0)),
                  pl.BlockSpec((PEER_HEADS, 2, PEER_NKEYS, PEER_QHALF), lambda i: (0, 0, 0, 0))],
        out_specs=[bspec, bspec, bspec, bspec, pl.BlockSpec((PEER_HEADS, tt), lambda i: (0, i))],
        out_shape=[big, big, big, big, jax.ShapeDtypeStruct((PEER_HEADS, t), F32)],
        scratch_shapes=[pltpu.VMEM((PEER_TOPK, tt), F32), pltpu.VMEM((PEER_TOPK, tt), F32),
                        pltpu.VMEM((PEER_TOPK, tt), F32)],
        compiler_params=_cparams(("parallel",)),
        name="peer_select",
    )(qh, keys.astype(F32))


def _peer_dense_kernel(hn_ref, u_ref, vt_ref, s1_ref, f1_ref, s2_ref, e2_ref, tau_ref, x_ref, o_ref,
                       acc_ref, g_ref, *, tt, ec):
    c = pl.program_id(1)

    @pl.when(c == 0)
    def _():
        acc_ref[...] = jnp.zeros_like(acc_ref)

    ht = _nt_dot(u_ref[...], hn_ref[...])
    nk = PEER_NKEYS
    for ii in range(ec // nk):
        for tg in range(tt // LANES):
            lanes = slice(tg * LANES, (tg + 1) * LANES)
            wsel = jnp.zeros((nk, LANES), F32)
            for h in range(PEER_HEADS):
                s1 = s1_ref[ii, h:h + 1, lanes]
                f1 = f1_ref[ii, h:h + 1, lanes]
                total = s1 + s2_ref[h, :, lanes]
                wsel = wsel + jnp.where(total >= tau_ref[h:h + 1, lanes], f1 * e2_ref[h, :, lanes], 0.0)
            hid = jax.nn.gelu(ht[ii * nk:(ii + 1) * nk, lanes])
            g_ref[ii * nk:(ii + 1) * nk, lanes] = (wsel * hid).astype(g_ref.dtype)
    acc_ref[...] += jnp.dot(vt_ref[...], g_ref[...], preferred_element_type=F32)

    @pl.when(c == pl.num_programs(1) - 1)
    def _():
        o_ref[...] = x_ref[...] + acc_ref[...].T


def _peer_dense(hn, u_tab, vt_tab, s1r, f1r, s2, e2, tau, x, tt=512, ec=512):
    t, d = hn.shape
    e = u_tab.shape[0]
    tt, ec = _pick(t, tt), _pick(e, ec)
    ni = ec // PEER_NKEYS
    return pl.pallas_call(
        functools.partial(_peer_dense_kernel, tt=tt, ec=ec),
        grid=(t // tt, e // ec),
        in_specs=[pl.BlockSpec((tt, d), lambda i, c: (i, 0)),
                  pl.BlockSpec((ec, d), lambda i, c: (c, 0)),
                  pl.BlockSpec((d, ec), lambda i, c: (0, c)),
                  pl.BlockSpec((ni, PEER_HEADS, tt), lambda i, c: (c, 0, i)),
                  pl.BlockSpec((ni, PEER_HEADS, tt), lambda i, c: (c, 0, i)),
                  pl.BlockSpec((PEER_HEADS, PEER_NKEYS, tt), lambda i, c: (0, 0, i)),
                  pl.BlockSpec((PEER_HEADS, PEER_NKEYS, tt), lambda i, c: (0, 0, i)),
                  pl.BlockSpec((PEER_HEADS, tt), lambda i, c: (0, i)),
                  pl.BlockSpec((tt, d), lambda i, c: (i, 0))],
        out_specs=pl.BlockSpec((tt, d), lambda i, c: (i, 0)),
        out_shape=jax.ShapeDtypeStruct((t, d), F32),
        scratch_shapes=[pltpu.VMEM((d, tt), F32), pltpu.VMEM((ec, tt), BF16)],
        compiler_params=_cparams(("parallel", "arbitrary")),
        name="peer_dense",
    )(hn, u_tab, vt_tab, s1r, f1r, s2, e2, tau, x)


def _layer(x, batch, seq, w_in, conv_w, a_log, dt_bias, a_out_gain, gmlp_norm, w_spatial, b_spatial,
           c_q_gain, c_k_gain, rel_bias, p_a, p_b, p_c, w_out, norm_mix, norm_ffn,
           peer_wq, peer_keys, peer_u, peer_v):
    bf = lambda w: w.astype(BF16)
    o_qkv_a = 0
    o_z = o_qkv_a + 3 * A_WIDTH
    o_beta = o_z + A_WIDTH
    o_alpha = o_beta + A_HEADS
    o_uv = o_alpha + A_HEADS
    o_qkv_c = o_uv + 2 * GMLP_WIDTH
    o_gate = o_qkv_c + 3 * C_WIDTH
    d_model = x.shape[1]

    h = _rmsnorm(x, norm_mix)
    qkv_a = _matmul(h, bf(w_in[:, o_qkv_a:o_z]), BF16, name="proj_qkv_a")
    z_a = _matmul(h, bf(w_in[:, o_z:o_beta]), BF16, name="proj_z_a")
    pad_cols = lambda w: jnp.pad(w, ((0, 0), (0, LANES - w.shape[1])))
    w_ba = jnp.concatenate([pad_cols(w_in[:, o_beta:o_alpha]), pad_cols(w_in[:, o_alpha:o_uv])], axis=1)
    ba = _matmul(h, bf(w_ba), F32, name="proj_beta_alpha")
    uv_b = _matmul(h, bf(w_in[:, o_uv:o_qkv_c]), BF16, name="proj_uv_b")
    qkv_c = _matmul(h, bf(w_in[:, o_qkv_c:o_gate]), BF16, name="proj_qkv_c")
    gate = _matmul(h, bf(w_in[:, o_gate:]), BF16, name="proj_gate")

    qkv_prep = _conv_prep(qkv_a, conv_w.astype(F32), batch, seq)
    gcum, beta = _gates(ba, a_log, dt_bias)
    y_a = _delta_rule(qkv_prep, gcum, beta, z_a, a_out_gain, batch, seq)
    y_b = _gmlp(uv_b, gmlp_norm, w_spatial, b_spatial)
    y_c = _band_attention(qkv_c, c_q_gain, c_k_gain, rel_bias, batch, seq)

    merged = _merge(y_a, y_b, y_c, bf(p_a), bf(p_b), bf(p_c), gate)
    x = _matmul(merged, bf(w_out), F32, residual=x, name="out_proj")

    hn = _rmsnorm(x, norm_ffn)
    q = _matmul(hn, bf(peer_wq), F32, name="peer_query")
    s1, f1, s2, e2, tau = _peer_select(q, peer_keys)
    to_rows = lambda a: a.transpose(1, 0, 2)
    return _peer_dense(hn, bf(peer_u), bf(peer_v).T, to_rows(s1), to_rows(f1), s2, e2, tau, x)


def kernel(x, w_in, conv_w, a_log, dt_bias, a_out_gain, gmlp_norm, w_spatial, b_spatial, c_q_gain, c_k_gain,
           rel_bias, p_a, p_b, p_c, w_out, norm_mix, norm_ffn, peer_wq, peer_keys, peer_u, peer_v):
    batch, seq, d_model = x.shape
    xt = x.reshape(batch * seq, d_model)
    for l in range(w_in.shape[0]):
        xt = _layer(xt, batch, seq, w_in[l], conv_w[l], a_log[l], dt_bias[l], a_out_gain[l], gmlp_norm[l],
                    w_spatial[l], b_spatial[l], c_q_gain[l], c_k_gain[l], rel_bias[l], p_a[l], p_b[l], p_c[l],
                    w_out[l], norm_mix[l], norm_ffn[l], peer_wq[l], peer_keys[l], peer_u[l], peer_v[l])
    return xt.reshape(batch, seq, d_model)
```

```python
import functools

import jax
import jax.numpy as jnp
import numpy as np
from jax import lax
from jax.experimental import pallas as pl
from jax.experimental.pallas import tpu as pltpu

F32 = jnp.float32
BF16 = jnp.bfloat16
HIGHEST = lax.Precision.HIGHEST

CHUNK = 64
EPS = 1e-6
A_HEADS = 8
A_DK = 128
A_DV = 128
A_WIDTH = A_HEADS * A_DV
CONV_W = 4
GMLP_CHUNK = 128
GMLP_GROUPS = 8
GMLP_GDIM = 128
GMLP_WIDTH = GMLP_GROUPS * GMLP_GDIM
C_HEADS = 8
C_DH = 128
C_WIDTH = C_HEADS * C_DH
LEFT_CHUNKS = 8
BAND = (LEFT_CHUNKS + 1) * CHUNK
MAX_REL = 128
N_BRANCH = 3
PEER_HEADS = 8
PEER_NKEYS = 128
PEER_TOPK = 16
PEER_QDIM = 256
PEER_QHALF = PEER_QDIM // 2

LANES = 128
SUBLANES = 8
NEG_BIG = -1e30
VMEM_LIMIT = 56 * 1024 * 1024
CHUNK_SHIFT = CHUNK.bit_length() - 1


def _cparams(sem, vmem_limit=VMEM_LIMIT, flags=None):
    return pltpu.CompilerParams(dimension_semantics=sem, vmem_limit_bytes=vmem_limit, flags=flags)


def _nt_dot(a, b, precision=None):
    return lax.dot_general(a, b, (((1,), (1,)), ((), ())), precision=precision,
                           preferred_element_type=F32)


def _tn_dot(a, b, precision=None):
    return lax.dot_general(a, b, (((0,), (0,)), ((), ())), precision=precision,
                           preferred_element_type=F32)


def _bdot(a, b):
    return jnp.dot(a.astype(BF16), b.astype(BF16), preferred_element_type=F32)


def _pick(n, pref):
    t = min(pref, n)
    while n % t:
        t -= LANES if t > LANES else 8
    return t


def _rmsnorm_kernel(x_ref, g_ref, o_ref):
    x = x_ref[...]
    ms = jnp.mean(x * x, axis=-1, keepdims=True)
    o_ref[...] = (x * lax.rsqrt(ms + EPS) * g_ref[...]).astype(o_ref.dtype)


def _rmsnorm(x, gain, out_dtype=BF16, tm=512):
    m, d = x.shape
    tm = _pick(m, tm)
    return pl.pallas_call(
        _rmsnorm_kernel,
        grid=(m // tm,),
        in_specs=[pl.BlockSpec((tm, d), lambda i: (i, 0)), pl.BlockSpec((1, d), lambda i: (0, 0))],
        out_specs=pl.BlockSpec((tm, d), lambda i: (i, 0)),
        out_shape=jax.ShapeDtypeStruct((m, d), out_dtype),
        compiler_params=_cparams(("parallel",)),
        name="rmsnorm",
    )(x, gain.reshape(1, d))


def _mm_kernel(a_ref, b_ref, o_ref):
    o_ref[...] = jnp.dot(a_ref[...], b_ref[...], preferred_element_type=F32).astype(o_ref.dtype)


def _mm_res_kernel(a_ref, b_ref, r_ref, o_ref):
    o_ref[...] = r_ref[...] + jnp.dot(a_ref[...], b_ref[...], preferred_element_type=F32)


def _matmul(a, b, out_dtype, residual=None, tm=1024, tn=1024, name="matmul"):
    m, k = a.shape
    n = b.shape[1]
    tm, tn = _pick(m, tm), _pick(n, tn)
    in_specs = [pl.BlockSpec((tm, k), lambda j, i: (i, 0)), pl.BlockSpec((k, tn), lambda j, i: (0, j))]
    args = [a, b]
    kern = _mm_kernel
    if residual is not None:
        in_specs.append(pl.BlockSpec((tm, tn), lambda j, i: (i, j)))
        args.append(residual)
        kern = _mm_res_kernel
    return pl.pallas_call(
        kern,
        grid=(n // tn, m // tm),
        in_specs=in_specs,
        out_specs=pl.BlockSpec((tm, tn), lambda j, i: (i, j)),
        out_shape=jax.ShapeDtypeStruct((m, n), out_dtype),
        compiler_params=_cparams(("parallel", "parallel")),
        name=name,
    )(*args)


def _conv_prep_kernel(cur_ref, prev_ref, w_ref, o_ref, ext_ref, *, ts, tc, prev_rows, n_qk_tiles, n_q_tiles):
    i = pl.program_id(1)
    c = pl.program_id(2)
    prev = prev_ref[...].astype(F32)
    prev = jnp.where(i == 0, 0.0, prev)
    cur = cur_ref[...].astype(F32)
    ext_ref[0:prev_rows, :] = prev
    ext_ref[prev_rows:prev_rows + ts, :] = cur
    w = w_ref[...]
    acc = cur * w[CONV_W - 1:CONV_W, :]
    for d in range(1, CONV_W):
        acc = acc + ext_ref[prev_rows - d:prev_rows - d + ts, :] * w[CONV_W - 1 - d:CONV_W - d, :]
    y = acc * jax.nn.sigmoid(acc)
    parts = []
    for h in range(tc // A_DK):
        yh = y[:, h * A_DK:(h + 1) * A_DK]
        parts.append(yh * lax.rsqrt(jnp.sum(yh * yh, axis=-1, keepdims=True) + EPS))
    yn = jnp.concatenate(parts, axis=-1)
    scale = jnp.where(c < n_q_tiles, A_DK ** -0.5, 1.0).astype(F32)
    o_ref[...] = jnp.where(c < n_qk_tiles, yn * scale, y).astype(o_ref.dtype)


def _conv_prep(qkv, conv_w, batch, seq, ts=512, tc=512):
    t, ch = qkv.shape
    ts = _pick(seq, ts)
    prev_rows = 16
    n_s = seq // ts
    kern = functools.partial(_conv_prep_kernel, ts=ts, tc=tc, prev_rows=prev_rows,
                             n_qk_tiles=2 * A_WIDTH // tc, n_q_tiles=A_WIDTH // tc)
    rpb = ts // prev_rows
    return pl.pallas_call(
        kern,
        grid=(batch, n_s, ch // tc),
        in_specs=[
            pl.BlockSpec((ts, tc), lambda b, i, c: (b * n_s + i, c)),
            pl.BlockSpec((prev_rows, tc), lambda b, i, c: (jnp.maximum((b * n_s + i) * rpb - 1, 0), c)),
            pl.BlockSpec((CONV_W, tc), lambda b, i, c: (0, c)),
        ],
        out_specs=pl.BlockSpec((ts, tc), lambda b, i, c: (b * n_s + i, c)),
        out_shape=jax.ShapeDtypeStruct((t, ch), BF16),
        scratch_shapes=[pltpu.VMEM((prev_rows + ts, tc), F32)],
        compiler_params=_cparams(("parallel", "parallel", "parallel")),
        name="conv_prep",
    )(qkv, qkv, conv_w)


def _gates_kernel(ba_ref, alog_ref, dtb_ref, g_ref, b_ref, dec_ref, *, ts):
    hdot = functools.partial(jnp.dot, precision=HIGHEST, preferred_element_type=F32)
    ba = ba_ref[...]
    lane = lax.broadcasted_iota(jnp.int32, (ts, LANES), 1)
    head_lane = lane < A_HEADS
    beta = jnp.where(head_lane, jax.nn.sigmoid(ba[:, :LANES]), 0.0)
    g = -jnp.exp(alog_ref[...]) * jax.nn.softplus(ba[:, LANES:] + dtb_ref[...])
    g = jnp.where(head_lane, g, 0.0)
    r = lax.broadcasted_iota(jnp.int32, (ts, ts), 0)
    c = lax.broadcasted_iota(jnp.int32, (ts, ts), 1)
    same_chunk = (r >> CHUNK_SHIFT) == (c >> CHUNK_SHIFT)
    tri = jnp.where((c <= r) & same_chunk, 1.0, 0.0).astype(F32)
    gcum = hdot(tri, g)
    er = lax.broadcasted_iota(jnp.int32, (LANES, A_WIDTH), 0)
    ec = lax.broadcasted_iota(jnp.int32, (LANES, A_WIDTH), 1)
    spread = jnp.where(er == (ec >> (A_DV.bit_length() - 1)), 1.0, 0.0).astype(F32)
    g_ref[...] = hdot(gcum, spread)
    b_ref[...] = hdot(beta, spread)
    wd = A_HEADS * CHUNK
    er = lax.broadcasted_iota(jnp.int32, (LANES, wd), 0)
    ec = lax.broadcasted_iota(jnp.int32, (LANES, wd), 1)
    gi = hdot(gcum, jnp.where(er == (ec >> CHUNK_SHIFT), 1.0, 0.0).astype(F32))
    ipos = lax.broadcasted_iota(jnp.int32, (ts, wd), 0) & (CHUNK - 1)
    jpos = lax.broadcasted_iota(jnp.int32, (ts, wd), 1) & (CHUNK - 1)
    blk = jnp.where(same_chunk, 1.0, 0.0).astype(F32)
    gj = hdot(blk, jnp.where(ipos == jpos, gi, 0.0))
    dec_ref[...] = jnp.exp(jnp.where(ipos >= jpos, gi - gj, NEG_BIG))


def _gates(ba, a_log, dt_bias, ts=512):
    t = ba.shape[0]
    ts = _pick(t, ts)
    pad = lambda v: jnp.pad(v.astype(F32), (0, LANES - A_HEADS)).reshape(1, LANES)
    out = jax.ShapeDtypeStruct((t, A_WIDTH), F32)
    wd = A_HEADS * CHUNK
    return pl.pallas_call(
        functools.partial(_gates_kernel, ts=ts),
        grid=(t // ts,),
        in_specs=[pl.BlockSpec((ts, 2 * LANES), lambda i: (i, 0)),
                  pl.BlockSpec((1, LANES), lambda i: (0, 0)), pl.BlockSpec((1, LANES), lambda i: (0, 0))],
        out_specs=[pl.BlockSpec((ts, A_WIDTH), lambda i: (i, 0)), pl.BlockSpec((ts, A_WIDTH), lambda i: (i, 0)),
                   pl.BlockSpec((ts, wd), lambda i: (i, 0))],
        out_shape=[out, out, jax.ShapeDtypeStruct((t, wd), F32)],
        compiler_params=_cparams(("parallel",)),
        name="gates",
    )(ba, pad(a_log), pad(dt_bias))


def _delta_solve_kernel(q_ref, k_ref, v_ref, g_ref, b_ref, dec_ref, u_ref, wq_ref, kd_ref, at_ref, egl_ref,
                        *, ts, hp):
    ri = lax.broadcasted_iota(jnp.int32, (CHUNK, CHUNK), 0)
    ci = lax.broadcasted_iota(jnp.int32, (CHUNK, CHUNK), 1)
    strict = ri > ci
    ident = jnp.where(ri == ci, 1.0, 0.0).astype(F32)

    def body(n, carry):
        rows = pl.ds(pl.multiple_of(n * CHUNK, CHUNK), CHUNK)
        for h in range(hp):
            cols = slice(h * A_DK, (h + 1) * A_DK)
            dcols = slice(h * CHUNK, (h + 1) * CHUNK)
            q = q_ref[rows, cols].astype(F32)
            k = k_ref[rows, cols].astype(F32)
            v = v_ref[rows, cols].astype(F32)
            gb = g_ref[rows, cols]
            bt = b_ref[rows, cols]
            dec = dec_ref[rows, dcols]
            kb = k * bt
            eg = jnp.exp(gb)
            qk_kk = _nt_dot(jnp.concatenate([q, kb], axis=0).astype(BF16), k.astype(BF16))
            attn = qk_kk[:CHUNK] * dec
            a = jnp.where(strict, qk_kk[CHUNK:] * dec, 0.0)
            inv = ident - a
            p = _bdot(a, a)
            for _ in range(CHUNK_SHIFT - 2):
                y = _bdot(jnp.concatenate([inv, p], axis=0), p)
                inv = inv + y[:CHUNK]
                p = y[CHUNK:]
            inv = inv + _bdot(inv, p)
            uw = _bdot(inv, jnp.concatenate([v * bt, kb * eg], axis=1))
            g_last = gb[CHUNK - 1:CHUNK, :]
            u_ref[rows, cols] = uw[:, :A_DV].astype(u_ref.dtype)
            wq_ref[pl.ds(pl.multiple_of(2 * n * CHUNK, CHUNK), CHUNK), cols] = uw[:, A_DV:].astype(wq_ref.dtype)
            wq_ref[pl.ds(pl.multiple_of((2 * n + 1) * CHUNK, CHUNK), CHUNK), cols] = (q * eg).astype(wq_ref.dtype)
            kd_ref[rows, cols] = (k * jnp.exp(g_last - gb)).astype(kd_ref.dtype)
            at_ref[rows, dcols] = attn.astype(at_ref.dtype)
            egl_ref[pl.ds(pl.multiple_of(n * SUBLANES, SUBLANES), SUBLANES), cols] = jnp.broadcast_to(
                jnp.exp(g_last), (SUBLANES, A_DV))
        return carry

    lax.fori_loop(0, ts // CHUNK, body, 0, unroll=2)


def _delta_solve(qkv, gcum, beta, dec, ts=512, hp=4):
    t = qkv.shape[0]
    ts = _pick(t, ts)
    hw = hp * A_DK
    n_hb = A_WIDTH // hw
    spec = lambda off: pl.BlockSpec((ts, hw), lambda i, hb: (i, off * n_hb + hb))
    wide = jax.ShapeDtypeStruct((t, A_WIDTH), BF16)
    return pl.pallas_call(
        functools.partial(_delta_solve_kernel, ts=ts, hp=hp),
        grid=(t // ts, n_hb),
        in_specs=[spec(0), spec(1), spec(2), spec(0), spec(0),
                  pl.BlockSpec((ts, hp * CHUNK), lambda i, hb: (i, hb))],
        out_specs=[spec(0),
                   pl.BlockSpec((2 * ts, hw), lambda i, hb: (i, hb)),
                   spec(0),
                   pl.BlockSpec((ts, hp * CHUNK), lambda i, hb: (i, hb)),
                   pl.BlockSpec((ts // CHUNK * SUBLANES, hw), lambda i, hb: (i, hb))],
        out_shape=[wide, jax.ShapeDtypeStruct((2 * t, A_WIDTH), BF16), wide,
                   jax.ShapeDtypeStruct((t, A_HEADS * CHUNK), BF16),
                   jax.ShapeDtypeStruct((t // CHUNK * SUBLANES, A_WIDTH), F32)],
        compiler_params=_cparams(("parallel", "parallel")),
        name="delta_solve",
    )(qkv, qkv, qkv, gcum, beta, dec)


def _delta_rec_kernel(u_ref, wq_ref, kd_ref, at_ref, egl_ref, z_ref, gain_ref, o_ref, s_ref, *, ts):
    @pl.when(pl.program_id(1) == 0)
    def _():
        s_ref[...] = jnp.zeros_like(s_ref)

    gain = gain_ref[...]

    def body(n, carry):
        rows = pl.ds(pl.multiple_of(n * CHUNK, CHUNK), CHUNK)
        rows2 = pl.ds(pl.multiple_of(2 * n * CHUNK, 2 * CHUNK), 2 * CHUNK)
        for h in range(A_HEADS):
            cols = slice(h * A_DK, (h + 1) * A_DK)
            s = s_ref[h]
            ws = jnp.dot(wq_ref[rows2, cols], s.astype(BF16), preferred_element_type=F32)
            v_new = (u_ref[rows, cols].astype(F32) - ws[:CHUNK]).astype(BF16)
            o = ws[CHUNK:] + jnp.dot(at_ref[rows, h * CHUNK:(h + 1) * CHUNK], v_new, preferred_element_type=F32)
            eg_last = egl_ref[pl.ds(pl.multiple_of(n * SUBLANES, SUBLANES), SUBLANES), cols][0:1, :]
            s_ref[h] = s * eg_last + _tn_dot(kd_ref[rows, cols], v_new)
            on = o * lax.rsqrt(jnp.mean(o * o, axis=-1, keepdims=True) + EPS) * gain
            z = z_ref[rows, cols].astype(F32)
            o_ref[rows, cols] = (on * (z * jax.nn.sigmoid(z))).astype(o_ref.dtype)
        return carry

    lax.fori_loop(0, ts // CHUNK, body, 0)


def _delta_rec(u, wq, kd, at, egl, z, out_gain, batch, seq, ts=512):
    t = u.shape[0]
    ts = _pick(seq, ts)
    n_s = seq // ts
    row = lambda b, i: (b * n_s + i, 0)
    return pl.pallas_call(
        functools.partial(_delta_rec_kernel, ts=ts),
        grid=(batch, n_s),
        in_specs=[pl.BlockSpec((ts, A_WIDTH), row), pl.BlockSpec((2 * ts, A_WIDTH), row),
                  pl.BlockSpec((ts, A_WIDTH), row), pl.BlockSpec((ts, A_HEADS * CHUNK), row),
                  pl.BlockSpec((ts // CHUNK * SUBLANES, A_WIDTH), row), pl.BlockSpec((ts, A_WIDTH), row),
                  pl.BlockSpec((1, A_DV), lambda b, i: (0, 0))],
        out_specs=pl.BlockSpec((ts, A_WIDTH), row),
        out_shape=jax.ShapeDtypeStruct((t, A_WIDTH), BF16),
        scratch_shapes=[pltpu.VMEM((A_HEADS, A_DK, A_DV), F32)],
        compiler_params=_cparams(("parallel", "arbitrary")),
        name="delta_rec",
    )(u, wq, kd, at, egl, z, out_gain.reshape(1, A_DV).astype(F32))


def _gmlp_kernel(u_ref, v_ref, gain_ref, w_ref, bias_ref, o_ref, *, nb):
    ri = lax.broadcasted_iota(jnp.int32, (GMLP_CHUNK, GMLP_CHUNK), 0)
    ci = lax.broadcasted_iota(jnp.int32, (GMLP_CHUNK, GMLP_CHUNK), 1)
    tril = ri >= ci
    gain = gain_ref[...]
    for blk in range(nb):
        rows = slice(blk * GMLP_CHUNK, (blk + 1) * GMLP_CHUNK)
        u = jax.nn.gelu(u_ref[rows, :].astype(F32))
        v = jax.nn.gelu(v_ref[rows, :].astype(F32))
        vn = (v * lax.rsqrt(jnp.mean(v * v, axis=-1, keepdims=True) + EPS) * gain).astype(BF16)
        for g in range(GMLP_GROUPS):
            cols = slice(g * GMLP_GDIM, (g + 1) * GMLP_GDIM)
            w = jnp.where(tril, w_ref[g], 0.0).astype(BF16)
            mixed = jnp.dot(w, vn[:, cols], preferred_element_type=F32) + bias_ref[:, cols]
            o_ref[rows, cols] = (u[:, cols] * mixed).astype(o_ref.dtype)


def _gmlp(uv, norm_gain, w_spatial, b_spatial, nb=4):
    t = uv.shape[0]
    tm = nb * GMLP_CHUNK
    bias = jnp.repeat(b_spatial.T.astype(F32), GMLP_GDIM, axis=1)
    return pl.pallas_call(
        functools.partial(_gmlp_kernel, nb=nb),
        grid=(t // tm,),
        in_specs=[pl.BlockSpec((tm, GMLP_WIDTH), lambda i: (i, 0)),
                  pl.BlockSpec((tm, GMLP_WIDTH), lambda i: (i, 1)),
                  pl.BlockSpec((1, GMLP_WIDTH), lambda i: (0, 0)),
                  pl.BlockSpec((GMLP_GROUPS, GMLP_CHUNK, GMLP_CHUNK), lambda i: (0, 0, 0)),
                  pl.BlockSpec((GMLP_CHUNK, GMLP_WIDTH), lambda i: (0, 0))],
        out_specs=pl.BlockSpec((tm, GMLP_WIDTH), lambda i: (i, 0)),
        out_shape=jax.ShapeDtypeStruct((t, GMLP_WIDTH), BF16),
        compiler_params=_cparams(("parallel",)),
        name="gmlp",
    )(uv, uv, norm_gain.reshape(1, GMLP_WIDTH).astype(F32), w_spatial.astype(F32), bias)


def _band_kernel(q_ref, kp_ref, kc_ref, vp_ref, vc_ref, qg_ref, kg_ref, bias_ref, o_ref, *, tq):
    i = pl.program_id(2)

    def norm(x, gain):
        x = x.astype(F32)
        return x * lax.rsqrt(jnp.mean(x * x, axis=-1, keepdims=True) + EPS) * gain

    qn = norm(q_ref[...], qg_ref[...]).astype(BF16)
    kcat = jnp.concatenate([norm(kp_ref[...], kg_ref[...]), norm(kc_ref[...], kg_ref[...])], axis=0).astype(BF16)
    vcat = jnp.concatenate([vp_ref[...], vc_ref[...]], axis=0).astype(BF16)
    bias = bias_ref[0]
    pad = LEFT_CHUNKS * CHUNK
    kpos = lax.broadcasted_iota(jnp.int32, (CHUNK, BAND), 1)
    for c in range(tq // CHUNK):
        lo = c * CHUNK + (tq - pad)
        kw = kcat[lo:lo + BAND]
        vw = vcat[lo:lo + BAND]
        s = _nt_dot(qn[c * CHUNK:(c + 1) * CHUNK], kw) * (C_DH ** -0.5) + bias
        valid = (i > 0) | (kpos + c * CHUNK >= pad)
        s = jnp.where(valid, s, NEG_BIG)
        m = jnp.max(s, axis=-1, keepdims=True)
        p = jnp.exp(s - m)
        denom = jnp.sum(p, axis=-1, keepdims=True)
        o = jnp.dot((p / denom).astype(BF16), vw, preferred_element_type=F32)
        o_ref[c * CHUNK:(c + 1) * CHUNK, :] = o.astype(o_ref.dtype)


def _band_bias(rel_bias):
    diag = np.arange(-(CHUNK - 1), BAND)
    idx = np.clip(LEFT_CHUNKS * CHUNK - diag, -MAX_REL, MAX_REL) + MAX_REL
    vec = rel_bias.astype(F32)[:, idx]
    return jnp.stack([vec[:, CHUNK - 1 - i:CHUNK - 1 - i + BAND] for i in range(CHUNK)], axis=1)


def _band_attention(qkv, q_gain, k_gain, rel_bias, batch, seq):
    t = qkv.shape[0]
    tq = LEFT_CHUNKS * CHUNK
    n_s = seq // tq
    bias = _band_bias(rel_bias)
    cur = lambda off: pl.BlockSpec((tq, C_DH), lambda b, h, i: (b * n_s + i, off * C_HEADS + h))
    prv = lambda off: pl.BlockSpec((tq, C_DH), lambda b, h, i: (b * n_s + jnp.maximum(i - 1, 0), off * C_HEADS + h))
    vec = pl.BlockSpec((1, C_DH), lambda b, h, i: (0, 0))
    return pl.pallas_call(
        functools.partial(_band_kernel, tq=tq),
        grid=(batch, C_HEADS, n_s),
        in_specs=[cur(0), prv(1), cur(1), prv(2), cur(2), vec, vec,
                  pl.BlockSpec((1, CHUNK, BAND), lambda b, h, i: (h, 0, 0))],
        out_specs=pl.BlockSpec((tq, C_DH), lambda b, h, i: (b * n_s + i, h)),
        out_shape=jax.ShapeDtypeStruct((t, C_WIDTH), BF16),
        compiler_params=_cparams(("parallel", "parallel", "parallel")),
        name="band_attention",
    )(qkv, qkv, qkv, qkv, qkv, q_gain.reshape(1, C_DH).astype(F32), k_gain.reshape(1, C_DH).astype(F32), bias)


def _merge_kernel(ya_ref, yb_ref, yc_ref, pa_ref, pb_ref, pc_ref, ga_ref, gb_ref, gc_ref, o_ref):
    def branch(y_ref, p_ref, g_ref):
        gate = jax.nn.sigmoid(g_ref[...].astype(F32))
        return gate * jnp.dot(y_ref[...], p_ref[...], preferred_element_type=F32)

    merged = branch(ya_ref, pa_ref, ga_ref) + branch(yb_ref, pb_ref, gb_ref) + branch(yc_ref, pc_ref, gc_ref)
    o_ref[...] = merged.astype(o_ref.dtype)


def _merge(ya, yb, yc, pa, pb, pc, gate, tm=1024, tn=512):
    t, k = ya.shape
    d = pa.shape[1]
    tm, tn = _pick(t, tm), _pick(d, tn)
    nd = d // tn
    ysp = pl.BlockSpec((tm, k), lambda j, i: (i, 0))
    psp = pl.BlockSpec((k, tn), lambda j, i: (0, j))
    gsp = lambda br: pl.BlockSpec((tm, tn), lambda j, i: (i, br * nd + j))
    return pl.pallas_call(
        _merge_kernel,
        grid=(nd, t // tm),
        in_specs=[ysp, ysp, ysp, psp, psp, psp, gsp(0), gsp(1), gsp(2)],
        out_specs=pl.BlockSpec((tm, tn), lambda j, i: (i, j)),
        out_shape=jax.ShapeDtypeStruct((t, d), BF16),
        compiler_params=_cparams(("parallel", "parallel")),
        name="merge",
    )(ya, yb, yc, pa, pb, pc, gate, gate, gate)


def _top_values(x, k, out_ref, want_rank=False):
    cur = x
    rank = jnp.full(x.shape, float(k), F32) if want_rank else None
    for r in range(k):
        m = jnp.max(cur, axis=0, keepdims=True)
        out_ref[r:r + 1, :] = m
        hit = cur == m
        if want_rank:
            rank = jnp.where(hit, float(r), rank)
        if r + 1 < k:
            cur = jnp.where(hit, NEG_BIG, cur)
    return rank


def _peer_select_kernel(q_ref, keys_ref, cnt_ref, f1_ref, r2_ref, e2_ref, a_ref, b_ref, c_ref, *, tt):
    k = PEER_TOPK

    def head(h, carry):
        qh = q_ref[:, pl.ds(pl.multiple_of(h * PEER_QDIM, PEER_QDIM), PEER_QDIM)]
        s1 = _nt_dot(keys_ref[h, 0], qh[:, :PEER_QHALF], precision=HIGHEST)
        s2 = _nt_dot(keys_ref[h, 1], qh[:, PEER_QHALF:], precision=HIGHEST)
        _top_values(s1, k, a_ref)
        rank2 = _top_values(s2, k, b_ref, want_rank=True)
        bv = b_ref[...]
        cand = jnp.concatenate([a_ref[r:r + 1, :] + bv for r in range(k)], axis=0)
        _top_values(cand, k, c_ref)
        cv = c_ref[...]
        z = jnp.sum(jnp.exp(cv - cv[0:1, :]), axis=0, keepdims=True)
        tau = cv[k - 1:k, :]
        cnt = jnp.zeros(s1.shape, F32)
        for r in range(k):
            cnt = cnt + jnp.where(s1 + b_ref[r:r + 1, :] >= tau, 1.0, 0.0)
        cnt_ref[h] = cnt
        f1_ref[h] = jnp.exp(s1 - a_ref[0:1, :]) / z
        rows = pl.ds(pl.multiple_of(h * PEER_NKEYS, PEER_NKEYS), PEER_NKEYS)
        r2_ref[rows, :] = rank2.astype(r2_ref.dtype)
        e2_ref[rows, :] = jnp.exp(s2 - b_ref[0:1, :]).astype(e2_ref.dtype)
        return carry

    lax.fori_loop(0, PEER_HEADS, head, 0)


def _peer_select(q, keys, tt=256):
    t = q.shape[0]
    tt = _pick(t, tt)
    shape = (PEER_HEADS, PEER_NKEYS, t)
    bspec = pl.BlockSpec((PEER_HEADS, PEER_NKEYS, tt), lambda i: (0, 0, i))
    flat = (PEER_HEADS * PEER_NKEYS, t)
    flat_spec = pl.BlockSpec((PEER_HEADS * PEER_NKEYS, tt), lambda i: (0, i))
    return pl.pallas_call(
        functools.partial(_peer_select_kernel, tt=tt),
        grid=(t // tt,),
        in_specs=[pl.BlockSpec((tt, PEER_HEADS * PEER_QDIM), lambda i: (i, 0)),
                  pl.BlockSpec((PEER_HEADS, 2, PEER_NKEYS, PEER_QHALF), lambda i: (0, 0, 0, 0))],
        out_specs=[bspec, bspec, flat_spec, flat_spec],
        out_shape=[jax.ShapeDtypeStruct(shape, F32), jax.ShapeDtypeStruct(shape, F32),
                   jax.ShapeDtypeStruct(flat, BF16), jax.ShapeDtypeStruct(flat, BF16)],
        scratch_shapes=[pltpu.VMEM((PEER_TOPK, tt), F32), pltpu.VMEM((PEER_TOPK, tt), F32),
                        pltpu.VMEM((PEER_TOPK, tt), F32)],
        compiler_params=_cparams(("parallel",)),
        name="peer_select",
    )(q, keys.astype(F32))


def _peer_dense_kernel(hn_ref, u_ref, vt_ref, cnt_ref, f1_ref, r2_ref, e2_ref, x_ref, o_ref,
                       acc_ref, ht0_ref, ht1_ref, g0_ref, g1_ref, *, tt, ec, nc, d):
    s = pl.program_id(0)
    c_out = lax.rem(jnp.maximum(s - 2, 0), nc)

    @pl.when(s == 0)
    def _():
        for ref in (ht0_ref, ht1_ref, g0_ref, g1_ref):
            ref[...] = jnp.zeros_like(ref)

    @pl.when(c_out == 0)
    def _():
        acc_ref[...] = jnp.zeros_like(acc_ref)

    nk = PEER_NKEYS

    def stages(ht_w, ht_r, g_w, g_r):
        ht_w[...] = _nt_dot(u_ref[...], hn_ref[...]).astype(ht_w.dtype)
        rb = 2 * SUBLANES
        for ii in range(ec // nk):
            for tg in range(tt // LANES):
                lanes = slice(tg * LANES, (tg + 1) * LANES)
                bcast = lambda ref, h: jnp.broadcast_to(ref[ii, h:h + 1, lanes], (rb, LANES)).astype(BF16)
                cnt = [bcast(cnt_ref, h) for h in range(PEER_HEADS)]
                f1 = [bcast(f1_ref, h) for h in range(PEER_HEADS)]
                for j0 in range(0, nk, rb):
                    wsel = jnp.zeros((rb, LANES), BF16)
                    for h in range(PEER_HEADS):
                        hr = slice(h * nk + j0, h * nk + j0 + rb)
                        wsel = wsel + jnp.where(r2_ref[hr, lanes] < cnt[h], f1[h] * e2_ref[hr, lanes],
                                                jnp.zeros((), BF16))
                    rows = slice(ii * nk + j0, ii * nk + j0 + rb)
                    g_w[rows, lanes] = wsel * jax.nn.gelu(ht_r[rows, lanes])
        acc_ref[...] += jnp.dot(vt_ref[...], g_r[...], preferred_element_type=F32)

    parity = lax.rem(s, 2)
    pl.when(parity == 0)(functools.partial(stages, ht0_ref, ht1_ref, g1_ref, g0_ref))
    pl.when(parity == 1)(functools.partial(stages, ht1_ref, ht0_ref, g0_ref, g1_ref))

    @pl.when((c_out == nc - 1) & (s >= 2))
    def _():
        step = 512
        for d0 in range(0, d, step):
            o_ref[:, d0:d0 + step] = x_ref[:, d0:d0 + step] + acc_ref[d0:d0 + step, :].T


def _peer_dense(hn, u_tab, vt_tab, cntr, f1r, r2, e2, x, tt=512, ec=1024):
    t, d = hn.shape
    e = u_tab.shape[0]
    tt, ec = _pick(t, tt), _pick(e, ec)
    ni = ec // PEER_NKEYS
    nc = e // ec
    n_steps = (t // tt) * nc
    tile = lambda s, lag: jnp.clip(s - lag, 0, n_steps - 1) // nc
    chunk = lambda s, lag: jnp.clip(s - lag, 0, n_steps - 1) % nc
    once = dict(pipeline_mode=pl.Buffered(1))
    return pl.pallas_call(
        functools.partial(_peer_dense_kernel, tt=tt, ec=ec, nc=nc, d=d),
        grid=(n_steps + 2,),
        in_specs=[pl.BlockSpec((tt, d), lambda s: (tile(s, 0), 0), **once),
                  pl.BlockSpec((ec, d), lambda s: (chunk(s, 0), 0)),
                  pl.BlockSpec((d, ec), lambda s: (0, chunk(s, 2))),
                  pl.BlockSpec((ni, PEER_HEADS, tt), lambda s: (chunk(s, 1), 0, tile(s, 1))),
                  pl.BlockSpec((ni, PEER_HEADS, tt), lambda s: (chunk(s, 1), 0, tile(s, 1))),
                  pl.BlockSpec((PEER_HEADS * PEER_NKEYS, tt), lambda s: (0, tile(s, 1))),
                  pl.BlockSpec((PEER_HEADS * PEER_NKEYS, tt), lambda s: (0, tile(s, 1))),
                  pl.BlockSpec((tt, d), lambda s: (tile(s, 2), 0), **once)],
        out_specs=pl.BlockSpec((tt, d), lambda s: (tile(s, 2), 0)),
        out_shape=jax.ShapeDtypeStruct((t, d), F32),
        scratch_shapes=[pltpu.VMEM((d, tt), F32)] + [pltpu.VMEM((ec, tt), BF16)] * 4,
        compiler_params=_cparams(("arbitrary",)),
        name="peer_dense",
    )(hn, u_tab, vt_tab, cntr, f1r, r2, e2, x)


def _layer(x, batch, seq, w_in, conv_w, a_log, dt_bias, a_out_gain, gmlp_norm, w_spatial, b_spatial,
           c_q_gain, c_k_gain, rel_bias, p_a, p_b, p_c, w_out, norm_mix, norm_ffn,
           peer_wq, peer_keys, peer_u, peer_v):
    bf = lambda w: w.astype(BF16)
    o_qkv_a = 0
    o_z = o_qkv_a + 3 * A_WIDTH
    o_beta = o_z + A_WIDTH
    o_alpha = o_beta + A_HEADS
    o_uv = o_alpha + A_HEADS
    o_qkv_c = o_uv + 2 * GMLP_WIDTH
    o_gate = o_qkv_c + 3 * C_WIDTH

    h = _rmsnorm(x, norm_mix)
    qkv_a = _matmul(h, bf(w_in[:, o_qkv_a:o_z]), BF16, name="proj_qkv_a")
    z_a = _matmul(h, bf(w_in[:, o_z:o_beta]), BF16, name="proj_z_a")
    pad_cols = lambda w: jnp.pad(w, ((0, 0), (0, LANES - w.shape[1])))
    w_ba = jnp.concatenate([pad_cols(w_in[:, o_beta:o_alpha]), pad_cols(w_in[:, o_alpha:o_uv])], axis=1)
    ba = _matmul(h, bf(w_ba), F32, name="proj_beta_alpha")
    uv_b = _matmul(h, bf(w_in[:, o_uv:o_qkv_c]), BF16, name="proj_uv_b")
    qkv_c = _matmul(h, bf(w_in[:, o_qkv_c:o_gate]), BF16, name="proj_qkv_c")
    gate = _matmul(h, bf(w_in[:, o_gate:]), BF16, name="proj_gate")

    qkv_prep = _conv_prep(qkv_a, conv_w.astype(F32), batch, seq)
    gcum, beta, dec = _gates(ba, a_log, dt_bias)
    u, wq, kd, at, egl = _delta_solve(qkv_prep, gcum, beta, dec)
    y_a = _delta_rec(u, wq, kd, at, egl, z_a, a_out_gain, batch, seq)
    y_b = _gmlp(uv_b, gmlp_norm, w_spatial, b_spatial)
    y_c = _band_attention(qkv_c, c_q_gain, c_k_gain, rel_bias, batch, seq)

    merged = _merge(y_a, y_b, y_c, bf(p_a), bf(p_b), bf(p_c), gate)
    x = _matmul(merged, bf(w_out), F32, residual=x, name="out_proj")

    hn = _rmsnorm(x, norm_ffn)
    q = _matmul(hn, bf(peer_wq), F32, name="peer_query")
    cnt, f1, r2, e2 = _peer_select(q, peer_keys)
    to_rows = lambda a: a.transpose(1, 0, 2)
    return _peer_dense(hn, bf(peer_u), bf(peer_v).T, to_rows(cnt), to_rows(f1), r2, e2, x)


def kernel(x, w_in, conv_w, a_log, dt_bias, a_out_gain, gmlp_norm, w_spatial, b_spatial, c_q_gain, c_k_gain,
           rel_bias, p_a, p_b, p_c, w_out, norm_mix, norm_ffn, peer_wq, peer_keys, peer_u, peer_v):
    batch, seq, d_model = x.shape
    xt = x.reshape(batch * seq, d_model)
    for l in range(w_in.shape[0]):
        xt = _layer(xt, batch, seq, w_in[l], conv_w[l], a_log[l], dt_bias[l], a_out_gain[l], gmlp_norm[l],
                    w_spatial[l], b_spatial[l], c_q_gain[l], c_k_gain[l], rel_bias[l], p_a[l], p_b[l], p_c[l],
                    w_out[l], norm_mix[l], norm_ffn[l], peer_wq[l], peer_keys[l], peer_u[l], peer_v[l])
    return xt.reshape(batch, seq, d_model)
```

```python
import functools

import jax
import jax.numpy as jnp
import numpy as np
from jax import lax
from jax.experimental import pallas as pl
from jax.experimental.pallas import tpu as pltpu

F32 = jnp.float32
BF16 = jnp.bfloat16
HIGHEST = lax.Precision.HIGHEST

CHUNK = 64
EPS = 1e-6
A_HEADS = 8
A_DK = 128
A_DV = 128
A_WIDTH = A_HEADS * A_DV
CONV_W = 4
GMLP_CHUNK = 128
GMLP_GROUPS = 8
GMLP_GDIM = 128
GMLP_WIDTH = GMLP_GROUPS * GMLP_GDIM
C_HEADS = 8
C_DH = 128
C_WIDTH = C_HEADS * C_DH
LEFT_CHUNKS = 8
BAND = (LEFT_CHUNKS + 1) * CHUNK
MAX_REL = 128
N_BRANCH = 3
PEER_HEADS = 8
PEER_NKEYS = 128
PEER_TOPK = 16
PEER_QDIM = 256
PEER_QHALF = PEER_QDIM // 2

LANES = 128
SUBLANES = 8
NEG_BIG = -1e30
VMEM_LIMIT = 56 * 1024 * 1024
CHUNK_SHIFT = CHUNK.bit_length() - 1


def _cparams(sem, vmem_limit=VMEM_LIMIT, flags=None):
    return pltpu.CompilerParams(dimension_semantics=sem, vmem_limit_bytes=vmem_limit, flags=flags)


def _nt_dot(a, b, precision=None):
    return lax.dot_general(a, b, (((1,), (1,)), ((), ())), precision=precision,
                           preferred_element_type=F32)


def _tn_dot(a, b, precision=None):
    return lax.dot_general(a, b, (((0,), (0,)), ((), ())), precision=precision,
                           preferred_element_type=F32)


def _bdot(a, b):
    return jnp.dot(a.astype(BF16), b.astype(BF16), preferred_element_type=F32)


def _pick(n, pref):
    t = min(pref, n)
    while n % t:
        t -= LANES if t > LANES else 8
    return t


def _rmsnorm_kernel(x_ref, g_ref, o_ref):
    x = x_ref[...]
    ms = jnp.mean(x * x, axis=-1, keepdims=True)
    o_ref[...] = (x * lax.rsqrt(ms + EPS) * g_ref[...]).astype(o_ref.dtype)


def _rmsnorm(x, gain, out_dtype=BF16, tm=512):
    m, d = x.shape
    tm = _pick(m, tm)
    return pl.pallas_call(
        _rmsnorm_kernel,
        grid=(m // tm,),
        in_specs=[pl.BlockSpec((tm, d), lambda i: (i, 0)), pl.BlockSpec((1, d), lambda i: (0, 0))],
        out_specs=pl.BlockSpec((tm, d), lambda i: (i, 0)),
        out_shape=jax.ShapeDtypeStruct((m, d), out_dtype),
        compiler_params=_cparams(("parallel",)),
        name="rmsnorm",
    )(x, gain.reshape(1, d))


def _mm_kernel(a_ref, b_ref, o_ref):
    o_ref[...] = jnp.dot(a_ref[...], b_ref[...], preferred_element_type=F32).astype(o_ref.dtype)


def _mm_res_kernel(a_ref, b_ref, r_ref, o_ref):
    o_ref[...] = r_ref[...] + jnp.dot(a_ref[...], b_ref[...], preferred_element_type=F32)


def _matmul(a, b, out_dtype, residual=None, tm=1024, tn=1024, name="matmul"):
    m, k = a.shape
    n = b.shape[1]
    tm, tn = _pick(m, tm), _pick(n, tn)
    in_specs = [pl.BlockSpec((tm, k), lambda j, i: (i, 0)), pl.BlockSpec((k, tn), lambda j, i: (0, j))]
    args = [a, b]
    kern = _mm_kernel
    if residual is not None:
        in_specs.append(pl.BlockSpec((tm, tn), lambda j, i: (i, j)))
        args.append(residual)
        kern = _mm_res_kernel
    return pl.pallas_call(
        kern,
        grid=(n // tn, m // tm),
        in_specs=in_specs,
        out_specs=pl.BlockSpec((tm, tn), lambda j, i: (i, j)),
        out_shape=jax.ShapeDtypeStruct((m, n), out_dtype),
        compiler_params=_cparams(("parallel", "parallel")),
        name=name,
    )(*args)


def _conv_prep_kernel(cur_ref, prev_ref, w_ref, o_ref, ext_ref, *, ts, tc, prev_rows, n_qk_tiles, n_q_tiles):
    i = pl.program_id(1)
    c = pl.program_id(2)
    prev = prev_ref[...].astype(F32)
    prev = jnp.where(i == 0, 0.0, prev)
    cur = cur_ref[...].astype(F32)
    ext_ref[0:prev_rows, :] = prev
    ext_ref[prev_rows:prev_rows + ts, :] = cur
    w = w_ref[...]
    acc = cur * w[CONV_W - 1:CONV_W, :]
    for d in range(1, CONV_W):
        acc = acc + ext_ref[prev_rows - d:prev_rows - d + ts, :] * w[CONV_W - 1 - d:CONV_W - d, :]
    y = acc * jax.nn.sigmoid(acc)
    parts = []
    for h in range(tc // A_DK):
        yh = y[:, h * A_DK:(h + 1) * A_DK]
        parts.append(yh * lax.rsqrt(jnp.sum(yh * yh, axis=-1, keepdims=True) + EPS))
    yn = jnp.concatenate(parts, axis=-1)
    scale = jnp.where(c < n_q_tiles, A_DK ** -0.5, 1.0).astype(F32)
    o_ref[...] = jnp.where(c < n_qk_tiles, yn * scale, y).astype(o_ref.dtype)


def _conv_prep(qkv, conv_w, batch, seq, ts=512, tc=512):
    t, ch = qkv.shape
    ts = _pick(seq, ts)
    prev_rows = 16
    n_s = seq // ts
    kern = functools.partial(_conv_prep_kernel, ts=ts, tc=tc, prev_rows=prev_rows,
                             n_qk_tiles=2 * A_WIDTH // tc, n_q_tiles=A_WIDTH // tc)
    rpb = ts // prev_rows
    return pl.pallas_call(
        kern,
        grid=(batch, n_s, ch // tc),
        in_specs=[
            pl.BlockSpec((ts, tc), lambda b, i, c: (b * n_s + i, c)),
            pl.BlockSpec((prev_rows, tc), lambda b, i, c: (jnp.maximum((b * n_s + i) * rpb - 1, 0), c)),
            pl.BlockSpec((CONV_W, tc), lambda b, i, c: (0, c)),
        ],
        out_specs=pl.BlockSpec((ts, tc), lambda b, i, c: (b * n_s + i, c)),
        out_shape=jax.ShapeDtypeStruct((t, ch), BF16),
        scratch_shapes=[pltpu.VMEM((prev_rows + ts, tc), F32)],
        compiler_params=_cparams(("parallel", "parallel", "parallel")),
        name="conv_prep",
    )(qkv, qkv, conv_w)


def _gates_kernel(ba_ref, alog_ref, dtb_ref, g_ref, b_ref, dec_ref, *, ts):
    hdot = functools.partial(jnp.dot, precision=HIGHEST, preferred_element_type=F32)
    ba = ba_ref[...]
    lane = lax.broadcasted_iota(jnp.int32, (ts, LANES), 1)
    head_lane = lane < A_HEADS
    beta = jnp.where(head_lane, jax.nn.sigmoid(ba[:, :LANES]), 0.0)
    g = -jnp.exp(alog_ref[...]) * jax.nn.softplus(ba[:, LANES:] + dtb_ref[...])
    g = jnp.where(head_lane, g, 0.0)
    r = lax.broadcasted_iota(jnp.int32, (ts, ts), 0)
    c = lax.broadcasted_iota(jnp.int32, (ts, ts), 1)
    same_chunk = (r >> CHUNK_SHIFT) == (c >> CHUNK_SHIFT)
    tri = jnp.where((c <= r) & same_chunk, 1.0, 0.0).astype(F32)
    gcum = hdot(tri, g)
    er = lax.broadcasted_iota(jnp.int32, (LANES, A_WIDTH), 0)
    ec = lax.broadcasted_iota(jnp.int32, (LANES, A_WIDTH), 1)
    spread = jnp.where(er == (ec >> (A_DV.bit_length() - 1)), 1.0, 0.0).astype(F32)
    g_ref[...] = hdot(gcum, spread)
    b_ref[...] = hdot(beta, spread)
    wd = A_HEADS * CHUNK
    er = lax.broadcasted_iota(jnp.int32, (LANES, wd), 0)
    ec = lax.broadcasted_iota(jnp.int32, (LANES, wd), 1)
    gi = hdot(gcum, jnp.where(er == (ec >> CHUNK_SHIFT), 1.0, 0.0).astype(F32))
    ipos = lax.broadcasted_iota(jnp.int32, (ts, wd), 0) & (CHUNK - 1)
    jpos = lax.broadcasted_iota(jnp.int32, (ts, wd), 1) & (CHUNK - 1)
    blk = jnp.where(same_chunk, 1.0, 0.0).astype(F32)
    gj = hdot(blk, jnp.where(ipos == jpos, gi, 0.0))
    dec_ref[...] = jnp.exp(jnp.where(ipos >= jpos, gi - gj, NEG_BIG))


def _gates(ba, a_log, dt_bias, ts=512):
    t = ba.shape[0]
    ts = _pick(t, ts)
    pad = lambda v: jnp.pad(v.astype(F32), (0, LANES - A_HEADS)).reshape(1, LANES)
    out = jax.ShapeDtypeStruct((t, A_WIDTH), F32)
    wd = A_HEADS * CHUNK
    return pl.pallas_call(
        functools.partial(_gates_kernel, ts=ts),
        grid=(t // ts,),
        in_specs=[pl.BlockSpec((ts, 2 * LANES), lambda i: (i, 0)),
                  pl.BlockSpec((1, LANES), lambda i: (0, 0)), pl.BlockSpec((1, LANES), lambda i: (0, 0))],
        out_specs=[pl.BlockSpec((ts, A_WIDTH), lambda i: (i, 0)), pl.BlockSpec((ts, A_WIDTH), lambda i: (i, 0)),
                   pl.BlockSpec((ts, wd), lambda i: (i, 0))],
        out_shape=[out, out, jax.ShapeDtypeStruct((t, wd), F32)],
        compiler_params=_cparams(("parallel",)),
        name="gates",
    )(ba, pad(a_log), pad(dt_bias))


def _delta_solve_kernel(q_ref, k_ref, v_ref, g_ref, b_ref, dec_ref, u_ref, wq_ref, kd_ref, at_ref, egl_ref,
                        *, ts, hp, group=8):
    ri = lax.broadcasted_iota(jnp.int32, (CHUNK, CHUNK), 0)
    ci = lax.broadcasted_iota(jnp.int32, (CHUNK, CHUNK), 1)
    strict = ri > ci
    ident = jnp.where(ri == ci, 1.0, 0.0).astype(F32)

    def load(n, h):
        rows = slice(n * CHUNK, (n + 1) * CHUNK)
        cols = slice(h * A_DK, (h + 1) * A_DK)
        c = dict(n=n, rows=rows, cols=cols, dcols=slice(h * CHUNK, (h + 1) * CHUNK))
        c["q"] = q_ref[rows, cols].astype(F32)
        c["k"] = k_ref[rows, cols].astype(F32)
        c["gb"] = g_ref[rows, cols]
        c["bt"] = b_ref[rows, cols]
        c["kb"] = c["k"] * c["bt"]
        c["eg"] = jnp.exp(c["gb"])
        return c

    chains = [(n, h) for n in range(ts // CHUNK) for h in range(hp)]
    for g0 in range(0, len(chains), group):
        cs = [load(n, h) for n, h in chains[g0:g0 + group]]
        for c in cs:
            qk_kk = _nt_dot(jnp.concatenate([c["q"], c["kb"]], axis=0).astype(BF16), c["k"].astype(BF16))
            dec = dec_ref[c["rows"], c["dcols"]]
            c["attn"] = qk_kk[:CHUNK] * dec
            c["a"] = jnp.where(strict, qk_kk[CHUNK:] * dec, 0.0)
        for c in cs:
            c["inv"] = ident - c["a"]
            c["p"] = _bdot(c["a"], c["a"])
        for _ in range(CHUNK_SHIFT - 2):
            for c in cs:
                y = _bdot(jnp.concatenate([c["inv"], c["p"]], axis=0), c["p"])
                c["inv"] = c["inv"] + y[:CHUNK]
                c["p"] = y[CHUNK:]
        for c in cs:
            c["inv"] = c["inv"] + _bdot(c["inv"], c["p"])
        for c in cs:
            v = v_ref[c["rows"], c["cols"]].astype(F32)
            c["uw"] = _bdot(c["inv"], jnp.concatenate([v * c["bt"], c["kb"] * c["eg"]], axis=1))
        for c in cs:
            n, rows, cols = c["n"], c["rows"], c["cols"]
            g_last = c["gb"][CHUNK - 1:CHUNK, :]
            u_ref[rows, cols] = c["uw"][:, :A_DV].astype(u_ref.dtype)
            wq_ref[2 * n * CHUNK:(2 * n + 1) * CHUNK, cols] = c["uw"][:, A_DV:].astype(wq_ref.dtype)
            wq_ref[(2 * n + 1) * CHUNK:(2 * n + 2) * CHUNK, cols] = (c["q"] * c["eg"]).astype(wq_ref.dtype)
            kd_ref[rows, cols] = (c["k"] * jnp.exp(g_last - c["gb"])).astype(kd_ref.dtype)
            at_ref[rows, c["dcols"]] = c["attn"].astype(at_ref.dtype)
            egl_ref[n * SUBLANES:(n + 1) * SUBLANES, cols] = jnp.broadcast_to(jnp.exp(g_last), (SUBLANES, A_DV))


def _delta_solve(qkv, gcum, beta, dec, ts=512, hp=4):
    t = qkv.shape[0]
    ts = _pick(t, ts)
    hw = hp * A_DK
    n_hb = A_WIDTH // hw
    spec = lambda off: pl.BlockSpec((ts, hw), lambda i, hb: (i, off * n_hb + hb))
    wide = jax.ShapeDtypeStruct((t, A_WIDTH), BF16)
    return pl.pallas_call(
        functools.partial(_delta_solve_kernel, ts=ts, hp=hp),
        grid=(t // ts, n_hb),
        in_specs=[spec(0), spec(1), spec(2), spec(0), spec(0),
                  pl.BlockSpec((ts, hp * CHUNK), lambda i, hb: (i, hb))],
        out_specs=[spec(0),
                   pl.BlockSpec((2 * ts, hw), lambda i, hb: (i, hb)),
                   spec(0),
                   pl.BlockSpec((ts, hp * CHUNK), lambda i, hb: (i, hb)),
                   pl.BlockSpec((ts // CHUNK * SUBLANES, hw), lambda i, hb: (i, hb))],
        out_shape=[wide, jax.ShapeDtypeStruct((2 * t, A_WIDTH), BF16), wide,
                   jax.ShapeDtypeStruct((t, A_HEADS * CHUNK), BF16),
                   jax.ShapeDtypeStruct((t // CHUNK * SUBLANES, A_WIDTH), F32)],
        compiler_params=_cparams(("parallel", "parallel")),
        name="delta_solve",
    )(qkv, qkv, qkv, gcum, beta, dec)


def _delta_rec_kernel(u_ref, wq_ref, kd_ref, at_ref, egl_ref, z_ref, gain_ref, o_ref, s_ref, *, ts):
    @pl.when(pl.program_id(1) == 0)
    def _():
        s_ref[...] = jnp.zeros_like(s_ref)

    gain = gain_ref[...]

    heads = range(A_HEADS)
    col = lambda h: slice(h * A_DK, (h + 1) * A_DK)
    for n in range(ts // CHUNK):
        rows = slice(n * CHUNK, (n + 1) * CHUNK)
        rows2 = slice(2 * n * CHUNK, (2 * n + 2) * CHUNK)
        s = [s_ref[h] for h in heads]
        ws = [jnp.dot(wq_ref[rows2, col(h)], s[h].astype(BF16), preferred_element_type=F32)
              for h in heads]
        v_new = [(u_ref[rows, col(h)].astype(F32) - ws[h][:CHUNK]).astype(BF16) for h in heads]
        o = [ws[h][CHUNK:] + jnp.dot(at_ref[rows, h * CHUNK:(h + 1) * CHUNK], v_new[h], preferred_element_type=F32)
             for h in heads]
        for h in heads:
            eg_last = egl_ref[n * SUBLANES:n * SUBLANES + 1, col(h)]
            s_ref[h] = s[h] * eg_last + _tn_dot(kd_ref[rows, col(h)], v_new[h])
        for h in heads:
            on = o[h] * lax.rsqrt(jnp.mean(o[h] * o[h], axis=-1, keepdims=True) + EPS) * gain
            z = z_ref[rows, col(h)].astype(F32)
            o_ref[rows, col(h)] = (on * (z * jax.nn.sigmoid(z))).astype(o_ref.dtype)


def _delta_rec(u, wq, kd, at, egl, z, out_gain, batch, seq, ts=512):
    t = u.shape[0]
    ts = _pick(seq, ts)
    n_s = seq // ts
    row = lambda b, i: (b * n_s + i, 0)
    return pl.pallas_call(
        functools.partial(_delta_rec_kernel, ts=ts),
        grid=(batch, n_s),
        in_specs=[pl.BlockSpec((ts, A_WIDTH), row), pl.BlockSpec((2 * ts, A_WIDTH), row),
                  pl.BlockSpec((ts, A_WIDTH), row), pl.BlockSpec((ts, A_HEADS * CHUNK), row),
                  pl.BlockSpec((ts // CHUNK * SUBLANES, A_WIDTH), row), pl.BlockSpec((ts, A_WIDTH), row),
                  pl.BlockSpec((1, A_DV), lambda b, i: (0, 0))],
        out_specs=pl.BlockSpec((ts, A_WIDTH), row),
        out_shape=jax.ShapeDtypeStruct((t, A_WIDTH), BF16),
        scratch_shapes=[pltpu.VMEM((A_HEADS, A_DK, A_DV), F32)],
        compiler_params=_cparams(("parallel", "arbitrary")),
        name="delta_rec",
    )(u, wq, kd, at, egl, z, out_gain.reshape(1, A_DV).astype(F32))


def _gmlp_kernel(u_ref, v_ref, gain_ref, w_ref, bias_ref, o_ref, *, nb):
    ri = lax.broadcasted_iota(jnp.int32, (GMLP_CHUNK, GMLP_CHUNK), 0)
    ci = lax.broadcasted_iota(jnp.int32, (GMLP_CHUNK, GMLP_CHUNK), 1)
    tril = ri >= ci
    gain = gain_ref[...]
    for blk in range(nb):
        rows = slice(blk * GMLP_CHUNK, (blk + 1) * GMLP_CHUNK)
        u = jax.nn.gelu(u_ref[rows, :].astype(F32))
        v = jax.nn.gelu(v_ref[rows, :].astype(F32))
        vn = (v * lax.rsqrt(jnp.mean(v * v, axis=-1, keepdims=True) + EPS) * gain).astype(BF16)
        for g in range(GMLP_GROUPS):
            cols = slice(g * GMLP_GDIM, (g + 1) * GMLP_GDIM)
            w = jnp.where(tril, w_ref[g], 0.0).astype(BF16)
            mixed = jnp.dot(w, vn[:, cols], preferred_element_type=F32) + bias_ref[:, cols]
            o_ref[rows, cols] = (u[:, cols] * mixed).astype(o_ref.dtype)


def _gmlp(uv, norm_gain, w_spatial, b_spatial, nb=4):
    t = uv.shape[0]
    tm = nb * GMLP_CHUNK
    bias = jnp.repeat(b_spatial.T.astype(F32), GMLP_GDIM, axis=1)
    return pl.pallas_call(
        functools.partial(_gmlp_kernel, nb=nb),
        grid=(t // tm,),
        in_specs=[pl.BlockSpec((tm, GMLP_WIDTH), lambda i: (i, 0)),
                  pl.BlockSpec((tm, GMLP_WIDTH), lambda i: (i, 1)),
                  pl.BlockSpec((1, GMLP_WIDTH), lambda i: (0, 0)),
                  pl.BlockSpec((GMLP_GROUPS, GMLP_CHUNK, GMLP_CHUNK), lambda i: (0, 0, 0)),
                  pl.BlockSpec((GMLP_CHUNK, GMLP_WIDTH), lambda i: (0, 0))],
        out_specs=pl.BlockSpec((tm, GMLP_WIDTH), lambda i: (i, 0)),
        out_shape=jax.ShapeDtypeStruct((t, GMLP_WIDTH), BF16),
        compiler_params=_cparams(("parallel",)),
        name="gmlp",
    )(uv, uv, norm_gain.reshape(1, GMLP_WIDTH).astype(F32), w_spatial.astype(F32), bias)


def _band_kernel(q_ref, kp_ref, kc_ref, vp_ref, vc_ref, qg_ref, kg_ref, bias_ref, o_ref, *, tq):
    i = pl.program_id(2)

    def norm(x, gain):
        x = x.astype(F32)
        return x * lax.rsqrt(jnp.mean(x * x, axis=-1, keepdims=True) + EPS) * gain

    qn = norm(q_ref[...], qg_ref[...]).astype(BF16)
    kcat = jnp.concatenate([norm(kp_ref[...], kg_ref[...]), norm(kc_ref[...], kg_ref[...])], axis=0).astype(BF16)
    vcat = jnp.concatenate([vp_ref[...], vc_ref[...]], axis=0).astype(BF16)
    bias = bias_ref[0]
    pad = LEFT_CHUNKS * CHUNK
    kpos = lax.broadcasted_iota(jnp.int32, (CHUNK, BAND), 1)
    chunks = range(tq // CHUNK)
    lo = [c * CHUNK + (tq - pad) for c in chunks]
    s = [_nt_dot(qn[c * CHUNK:(c + 1) * CHUNK], kcat[lo[c]:lo[c] + BAND]) for c in chunks]
    p = []
    for c in chunks:
        sc = s[c] * (C_DH ** -0.5) + bias
        valid = (i > 0) | (kpos + c * CHUNK >= pad)
        sc = jnp.where(valid, sc, NEG_BIG)
        e = jnp.exp(sc - jnp.max(sc, axis=-1, keepdims=True))
        p.append((e / jnp.sum(e, axis=-1, keepdims=True)).astype(BF16))
    o = [jnp.dot(p[c], vcat[lo[c]:lo[c] + BAND], preferred_element_type=F32) for c in chunks]
    for c in chunks:
        o_ref[c * CHUNK:(c + 1) * CHUNK, :] = o[c].astype(o_ref.dtype)


def _band_bias(rel_bias):
    diag = np.arange(-(CHUNK - 1), BAND)
    idx = np.clip(LEFT_CHUNKS * CHUNK - diag, -MAX_REL, MAX_REL) + MAX_REL
    vec = rel_bias.astype(F32)[:, idx]
    return jnp.stack([vec[:, CHUNK - 1 - i:CHUNK - 1 - i + BAND] for i in range(CHUNK)], axis=1)


def _band_attention(qkv, q_gain, k_gain, rel_bias, batch, seq):
    t = qkv.shape[0]
    tq = LEFT_CHUNKS * CHUNK
    n_s = seq // tq
    bias = _band_bias(rel_bias)
    cur = lambda off: pl.BlockSpec((tq, C_DH), lambda b, h, i: (b * n_s + i, off * C_HEADS + h))
    prv = lambda off: pl.BlockSpec((tq, C_DH), lambda b, h, i: (b * n_s + jnp.maximum(i - 1, 0), off * C_HEADS + h))
    vec = pl.BlockSpec((1, C_DH), lambda b, h, i: (0, 0))
    return pl.pallas_call(
        functools.partial(_band_kernel, tq=tq),
        grid=(batch, C_HEADS, n_s),
        in_specs=[cur(0), prv(1), cur(1), prv(2), cur(2), vec, vec,
                  pl.BlockSpec((1, CHUNK, BAND), lambda b, h, i: (h, 0, 0))],
        out_specs=pl.BlockSpec((tq, C_DH), lambda b, h, i: (b * n_s + i, h)),
        out_shape=jax.ShapeDtypeStruct((t, C_WIDTH), BF16),
        compiler_params=_cparams(("parallel", "parallel", "parallel")),
        name="band_attention",
    )(qkv, qkv, qkv, qkv, qkv, q_gain.reshape(1, C_DH).astype(F32), k_gain.reshape(1, C_DH).astype(F32), bias)


def _merge_kernel(ya_ref, yb_ref, yc_ref, pa_ref, pb_ref, pc_ref, ga_ref, gb_ref, gc_ref, o_ref):
    def branch(y_ref, p_ref, g_ref):
        gate = jax.nn.sigmoid(g_ref[...].astype(F32))
        return gate * jnp.dot(y_ref[...], p_ref[...], preferred_element_type=F32)

    merged = branch(ya_ref, pa_ref, ga_ref) + branch(yb_ref, pb_ref, gb_ref) + branch(yc_ref, pc_ref, gc_ref)
    o_ref[...] = merged.astype(o_ref.dtype)


def _merge(ya, yb, yc, pa, pb, pc, gate, tm=1024, tn=512):
    t, k = ya.shape
    d = pa.shape[1]
    tm, tn = _pick(t, tm), _pick(d, tn)
    nd = d // tn
    ysp = pl.BlockSpec((tm, k), lambda j, i: (i, 0))
    psp = pl.BlockSpec((k, tn), lambda j, i: (0, j))
    gsp = lambda br: pl.BlockSpec((tm, tn), lambda j, i: (i, br * nd + j))
    return pl.pallas_call(
        _merge_kernel,
        grid=(nd, t // tm),
        in_specs=[ysp, ysp, ysp, psp, psp, psp, gsp(0), gsp(1), gsp(2)],
        out_specs=pl.BlockSpec((tm, tn), lambda j, i: (i, j)),
        out_shape=jax.ShapeDtypeStruct((t, d), BF16),
        compiler_params=_cparams(("parallel", "parallel")),
        name="merge",
    )(ya, yb, yc, pa, pb, pc, gate, gate, gate)


def _top_values(x, k, out_ref, want_rank=False):
    cur = x
    rank = jnp.full(x.shape, float(k), F32) if want_rank else None
    for r in range(k):
        m = jnp.max(cur, axis=0, keepdims=True)
        out_ref[r:r + 1, :] = m
        hit = cur == m
        if want_rank:
            rank = jnp.where(hit, float(r), rank)
        if r + 1 < k:
            cur = jnp.where(hit, NEG_BIG, cur)
    return rank


def _peer_select_kernel(q_ref, keys_ref, cnt_ref, f1_ref, r2_ref, e2_ref, a_ref, b_ref, c_ref, *, tt):
    k = PEER_TOPK

    def head(h, carry):
        qh = q_ref[:, pl.ds(pl.multiple_of(h * PEER_QDIM, PEER_QDIM), PEER_QDIM)]
        s1 = _nt_dot(keys_ref[h, 0], qh[:, :PEER_QHALF], precision=HIGHEST)
        s2 = _nt_dot(keys_ref[h, 1], qh[:, PEER_QHALF:], precision=HIGHEST)
        _top_values(s1, k, a_ref)
        rank2 = _top_values(s2, k, b_ref, want_rank=True)
        bv = b_ref[...]
        cand = jnp.concatenate([a_ref[r:r + 1, :] + bv for r in range(k)], axis=0)
        _top_values(cand, k, c_ref)
        cv = c_ref[...]
        z = jnp.sum(jnp.exp(cv - cv[0:1, :]), axis=0, keepdims=True)
        tau = cv[k - 1:k, :]
        cnt = jnp.zeros(s1.shape, F32)
        for r in range(k):
            cnt = cnt + jnp.where(s1 + b_ref[r:r + 1, :] >= tau, 1.0, 0.0)
        cnt_ref[h] = cnt
        f1_ref[h] = jnp.exp(s1 - a_ref[0:1, :]) / z
        rows = pl.ds(pl.multiple_of(h * PEER_NKEYS, PEER_NKEYS), PEER_NKEYS)
        r2_ref[rows, :] = rank2.astype(r2_ref.dtype)
        e2_ref[rows, :] = jnp.exp(s2 - b_ref[0:1, :]).astype(e2_ref.dtype)
        return carry

    lax.fori_loop(0, PEER_HEADS, head, 0)


def _peer_select(q, keys, tt=256):
    t = q.shape[0]
    tt = _pick(t, tt)
    shape = (PEER_HEADS, PEER_NKEYS, t)
    bspec = pl.BlockSpec((PEER_HEADS, PEER_NKEYS, tt), lambda i: (0, 0, i))
    flat = (PEER_HEADS * PEER_NKEYS, t)
    flat_spec = pl.BlockSpec((PEER_HEADS * PEER_NKEYS, tt), lambda i: (0, i))
    return pl.pallas_call(
        functools.partial(_peer_select_kernel, tt=tt),
        grid=(t // tt,),
        in_specs=[pl.BlockSpec((tt, PEER_HEADS * PEER_QDIM), lambda i: (i, 0)),
                  pl.BlockSpec((PEER_HEADS, 2, PEER_NKEYS, PEER_QHALF), lambda i: (0, 0, 0, 0))],
        out_specs=[bspec, bspec, flat_spec, flat_spec],
        out_shape=[jax.ShapeDtypeStruct(shape, F32), jax.ShapeDtypeStruct(shape, F32),
                   jax.ShapeDtypeStruct(flat, BF16), jax.ShapeDtypeStruct(flat, BF16)],
        scratch_shapes=[pltpu.VMEM((PEER_TOPK, tt), F32), pltpu.VMEM((PEER_TOPK, tt), F32),
                        pltpu.VMEM((PEER_TOPK, tt), F32)],
        compiler_params=_cparams(("parallel",)),
        name="peer_select",
    )(q, keys.astype(F32))


def _peer_dense_kernel(hn_ref, u_ref, vt_ref, cnt_ref, f1_ref, r2_ref, e2_ref, x_ref, o_ref,
                       acc_ref, ht0_ref, ht1_ref, g0_ref, g1_ref, *, tt, ec, nc, d):
    s = pl.program_id(0)
    c_out = lax.rem(jnp.maximum(s - 2, 0), nc)

    @pl.when(s == 0)
    def _():
        for ref in (ht0_ref, ht1_ref, g0_ref, g1_ref):
            ref[...] = jnp.zeros_like(ref)

    @pl.when(c_out == 0)
    def _():
        acc_ref[...] = jnp.zeros_like(acc_ref)

    nk = PEER_NKEYS

    def stages(ht_w, ht_r, g_w, g_r):
        ht_w[...] = _nt_dot(u_ref[...], hn_ref[...]).astype(ht_w.dtype)
        acc_ref[...] += jnp.dot(vt_ref[...], g_r[...], preferred_element_type=F32)
        rb = 2 * SUBLANES
        for ii in range(ec // nk):
            for tg in range(tt // LANES):
                lanes = slice(tg * LANES, (tg + 1) * LANES)
                bcast = lambda ref, h: jnp.broadcast_to(ref[ii, h:h + 1, lanes], (rb, LANES)).astype(BF16)
                cnt = [bcast(cnt_ref, h) for h in range(PEER_HEADS)]
                f1 = [bcast(f1_ref, h) for h in range(PEER_HEADS)]
                for j0 in range(0, nk, rb):
                    wsel = jnp.zeros((rb, LANES), BF16)
                    for h in range(PEER_HEADS):
                        hr = slice(h * nk + j0, h * nk + j0 + rb)
                        wsel = wsel + jnp.where(r2_ref[hr, lanes] < cnt[h], f1[h] * e2_ref[hr, lanes],
                                                jnp.zeros((), BF16))
                    rows = slice(ii * nk + j0, ii * nk + j0 + rb)
                    g_w[rows, lanes] = wsel * jax.nn.gelu(ht_r[rows, lanes])

    parity = lax.rem(s, 2)
    pl.when(parity == 0)(functools.partial(stages, ht0_ref, ht1_ref, g1_ref, g0_ref))
    pl.when(parity == 1)(functools.partial(stages, ht1_ref, ht0_ref, g0_ref, g1_ref))

    @pl.when((c_out == nc - 1) & (s >= 2))
    def _():
        step = 512
        for d0 in range(0, d, step):
            o_ref[:, d0:d0 + step] = x_ref[:, d0:d0 + step] + acc_ref[d0:d0 + step, :].T


def _peer_dense(hn, u_tab, vt_tab, cntr, f1r, r2, e2, x, tt=512, ec=1024):
    t, d = hn.shape
    e = u_tab.shape[0]
    tt, ec = _pick(t, tt), _pick(e, ec)
    ni = ec // PEER_NKEYS
    nc = e // ec
    n_steps = (t // tt) * nc
    tile = lambda s, lag: jnp.clip(s - lag, 0, n_steps - 1) // nc
    chunk = lambda s, lag: jnp.clip(s - lag, 0, n_steps - 1) % nc
    once = dict(pipeline_mode=pl.Buffered(1))
    return pl.pallas_call(
        functools.partial(_peer_dense_kernel, tt=tt, ec=ec, nc=nc, d=d),
        grid=(n_steps + 2,),
        in_specs=[pl.BlockSpec((tt, d), lambda s: (tile(s, 0), 0), **once),
                  pl.BlockSpec((ec, d), lambda s: (chunk(s, 0), 0)),
                  pl.BlockSpec((d, ec), lambda s: (0, chunk(s, 2))),
                  pl.BlockSpec((ni, PEER_HEADS, tt), lambda s: (chunk(s, 1), 0, tile(s, 1))),
                  pl.BlockSpec((ni, PEER_HEADS, tt), lambda s: (chunk(s, 1), 0, tile(s, 1))),
                  pl.BlockSpec((PEER_HEADS * PEER_NKEYS, tt), lambda s: (0, tile(s, 1))),
                  pl.BlockSpec((PEER_HEADS * PEER_NKEYS, tt), lambda s: (0, tile(s, 1))),
                  pl.BlockSpec((tt, d), lambda s: (tile(s, 2), 0), **once)],
        out_specs=pl.BlockSpec((tt, d), lambda s: (tile(s, 2), 0)),
        out_shape=jax.ShapeDtypeStruct((t, d), F32),
        scratch_shapes=[pltpu.VMEM((d, tt), F32)] + [pltpu.VMEM((ec, tt), BF16)] * 4,
        compiler_params=_cparams(("arbitrary",)),
        name="peer_dense",
    )(hn, u_tab, vt_tab, cntr, f1r, r2, e2, x)


def _layer(x, batch, seq, w_in, conv_w, a_log, dt_bias, a_out_gain, gmlp_norm, w_spatial, b_spatial,
           c_q_gain, c_k_gain, rel_bias, p_a, p_b, p_c, w_out, norm_mix, norm_ffn,
           peer_wq, peer_keys, peer_u, peer_v):
    bf = lambda w: w.astype(BF16)
    o_qkv_a = 0
    o_z = o_qkv_a + 3 * A_WIDTH
    o_beta = o_z + A_WIDTH
    o_alpha = o_beta + A_HEADS
    o_uv = o_alpha + A_HEADS
    o_qkv_c = o_uv + 2 * GMLP_WIDTH
    o_gate = o_qkv_c + 3 * C_WIDTH

    h = _rmsnorm(x, norm_mix)
    qkv_a = _matmul(h, bf(w_in[:, o_qkv_a:o_z]), BF16, name="proj_qkv_a")
    z_a = _matmul(h, bf(w_in[:, o_z:o_beta]), BF16, name="proj_z_a")
    pad_cols = lambda w: jnp.pad(w, ((0, 0), (0, LANES - w.shape[1])))
    w_ba = jnp.concatenate([pad_cols(w_in[:, o_beta:o_alpha]), pad_cols(w_in[:, o_alpha:o_uv])], axis=1)
    ba = _matmul(h, bf(w_ba), F32, name="proj_beta_alpha")
    uv_b = _matmul(h, bf(w_in[:, o_uv:o_qkv_c]), BF16, name="proj_uv_b")
    qkv_c = _matmul(h, bf(w_in[:, o_qkv_c:o_gate]), BF16, name="proj_qkv_c")
    gate = _matmul(h, bf(w_in[:, o_gate:]), BF16, name="proj_gate")

    qkv_prep = _conv_prep(qkv_a, conv_w.astype(F32), batch, seq)
    gcum, beta, dec = _gates(ba, a_log, dt_bias)
    u, wq, kd, at, egl = _delta_solve(qkv_prep, gcum, beta, dec)
    y_a = _delta_rec(u, wq, kd, at, egl, z_a, a_out_gain, batch, seq)
    y_b = _gmlp(uv_b, gmlp_norm, w_spatial, b_spatial)
    y_c = _band_attention(qkv_c, c_q_gain, c_k_gain, rel_bias, batch, seq)

    merged = _merge(y_a, y_b, y_c, bf(p_a), bf(p_b), bf(p_c), gate)
    x = _matmul(merged, bf(w_out), F32, residual=x, name="out_proj")

    hn = _rmsnorm(x, norm_ffn)
    q = _matmul(hn, bf(peer_wq), F32, name="peer_query")
    cnt, f1, r2, e2 = _peer_select(q, peer_keys)
    to_rows = lambda a: a.transpose(1, 0, 2)
    return _peer_dense(hn, bf(peer_u), bf(peer_v).T, to_rows(cnt), to_rows(f1), r2, e2, x)


def kernel(x, w_in, conv_w, a_log, dt_bias, a_out_gain, gmlp_norm, w_spatial, b_spatial, c_q_gain, c_k_gain,
           rel_bias, p_a, p_b, p_c, w_out, norm_mix, norm_ffn, peer_wq, peer_keys, peer_u, peer_v):
    batch, seq, d_model = x.shape
    xt = x.reshape(batch * seq, d_model)
    for l in range(w_in.shape[0]):
        xt = _layer(xt, batch, seq, w_in[l], conv_w[l], a_log[l], dt_bias[l], a_out_gain[l], gmlp_norm[l],
                    w_spatial[l], b_spatial[l], c_q_gain[l], c_k_gain[l], rel_bias[l], p_a[l], p_b[l], p_c[l],
                    w_out[l], norm_mix[l], norm_ffn[l], peer_wq[l], peer_keys[l], peer_u[l], peer_v[l])
    return xt.reshape(batch, seq, d_model)
```

```python
import functools

import jax
import jax.numpy as jnp
import numpy as np
from jax import lax
from jax.experimental import pallas as pl
from jax.experimental.pallas import tpu as pltpu

F32 = jnp.float32
BF16 = jnp.bfloat16
HIGHEST = lax.Precision.HIGHEST

CHUNK = 64
EPS = 1e-6
A_HEADS = 8
A_DK = 128
A_DV = 128
A_WIDTH = A_HEADS * A_DV
CONV_W = 4
GMLP_CHUNK = 128
GMLP_GROUPS = 8
GMLP_GDIM = 128
GMLP_WIDTH = GMLP_GROUPS * GMLP_GDIM
C_HEADS = 8
C_DH = 128
C_WIDTH = C_HEADS * C_DH
LEFT_CHUNKS = 8
BAND = (LEFT_CHUNKS + 1) * CHUNK
MAX_REL = 128
N_BRANCH = 3
PEER_HEADS = 8
PEER_NKEYS = 128
PEER_TOPK = 16
PEER_QDIM = 256
PEER_QHALF = PEER_QDIM // 2

LANES = 128
SUBLANES = 8
NEG_BIG = -1e30
VMEM_LIMIT = 56 * 1024 * 1024
CHUNK_SHIFT = CHUNK.bit_length() - 1


def _cparams(sem, vmem_limit=VMEM_LIMIT, flags=None):
    return pltpu.CompilerParams(dimension_semantics=sem, vmem_limit_bytes=vmem_limit, flags=flags)


def _nt_dot(a, b, precision=None):
    return lax.dot_general(a, b, (((1,), (1,)), ((), ())), precision=precision,
                           preferred_element_type=F32)


def _tn_dot(a, b, precision=None):
    return lax.dot_general(a, b, (((0,), (0,)), ((), ())), precision=precision,
                           preferred_element_type=F32)


def _bdot(a, b):
    return jnp.dot(a.astype(BF16), b.astype(BF16), preferred_element_type=F32)


def _pick(n, pref):
    t = min(pref, n)
    while n % t:
        t -= LANES if t > LANES else 8
    return t


def _rmsnorm_kernel(x_ref, g_ref, o_ref):
    x = x_ref[...]
    ms = jnp.mean(x * x, axis=-1, keepdims=True)
    o_ref[...] = (x * lax.rsqrt(ms + EPS) * g_ref[...]).astype(o_ref.dtype)


def _rmsnorm(x, gain, out_dtype=BF16, tm=512):
    m, d = x.shape
    tm = _pick(m, tm)
    return pl.pallas_call(
        _rmsnorm_kernel,
        grid=(m // tm,),
        in_specs=[pl.BlockSpec((tm, d), lambda i: (i, 0)), pl.BlockSpec((1, d), lambda i: (0, 0))],
        out_specs=pl.BlockSpec((tm, d), lambda i: (i, 0)),
        out_shape=jax.ShapeDtypeStruct((m, d), out_dtype),
        compiler_params=_cparams(("parallel",)),
        name="rmsnorm",
    )(x, gain.reshape(1, d))


def _mm_kernel(a_ref, b_ref, o_ref):
    o_ref[...] = jnp.dot(a_ref[...], b_ref[...], preferred_element_type=F32).astype(o_ref.dtype)


def _mm_res_kernel(a_ref, b_ref, r_ref, o_ref):
    o_ref[...] = r_ref[...] + jnp.dot(a_ref[...], b_ref[...], preferred_element_type=F32)


def _matmul(a, b, out_dtype, residual=None, tm=1024, tn=1024, name="matmul"):
    m, k = a.shape
    n = b.shape[1]
    tm, tn = _pick(m, tm), _pick(n, tn)
    in_specs = [pl.BlockSpec((tm, k), lambda j, i: (i, 0)), pl.BlockSpec((k, tn), lambda j, i: (0, j))]
    args = [a, b]
    kern = _mm_kernel
    if residual is not None:
        in_specs.append(pl.BlockSpec((tm, tn), lambda j, i: (i, j)))
        args.append(residual)
        kern = _mm_res_kernel
    return pl.pallas_call(
        kern,
        grid=(n // tn, m // tm),
        in_specs=in_specs,
        out_specs=pl.BlockSpec((tm, tn), lambda j, i: (i, j)),
        out_shape=jax.ShapeDtypeStruct((m, n), out_dtype),
        compiler_params=_cparams(("parallel", "parallel")),
        name=name,
    )(*args)


def _conv_prep_kernel(cur_ref, prev_ref, w_ref, o_ref, ext_ref, *, ts, tc, prev_rows, n_qk_tiles, n_q_tiles):
    i = pl.program_id(1)
    c = pl.program_id(2)
    prev = prev_ref[...].astype(F32)
    prev = jnp.where(i == 0, 0.0, prev)
    cur = cur_ref[...].astype(F32)
    ext_ref[0:prev_rows, :] = prev
    ext_ref[prev_rows:prev_rows + ts, :] = cur
    w = w_ref[...]
    acc = cur * w[CONV_W - 1:CONV_W, :]
    for d in range(1, CONV_W):
        acc = acc + ext_ref[prev_rows - d:prev_rows - d + ts, :] * w[CONV_W - 1 - d:CONV_W - d, :]
    y = acc * jax.nn.sigmoid(acc)
    parts = []
    for h in range(tc // A_DK):
        yh = y[:, h * A_DK:(h + 1) * A_DK]
        parts.append(yh * lax.rsqrt(jnp.sum(yh * yh, axis=-1, keepdims=True) + EPS))
    yn = jnp.concatenate(parts, axis=-1)
    scale = jnp.where(c < n_q_tiles, A_DK ** -0.5, 1.0).astype(F32)
    o_ref[...] = jnp.where(c < n_qk_tiles, yn * scale, y).astype(o_ref.dtype)


def _conv_prep(qkv, conv_w, batch, seq, ts=512, tc=512):
    t, ch = qkv.shape
    ts = _pick(seq, ts)
    prev_rows = 16
    n_s = seq // ts
    kern = functools.partial(_conv_prep_kernel, ts=ts, tc=tc, prev_rows=prev_rows,
                             n_qk_tiles=2 * A_WIDTH // tc, n_q_tiles=A_WIDTH // tc)
    rpb = ts // prev_rows
    return pl.pallas_call(
        kern,
        grid=(batch, n_s, ch // tc),
        in_specs=[
            pl.BlockSpec((ts, tc), lambda b, i, c: (b * n_s + i, c)),
            pl.BlockSpec((prev_rows, tc), lambda b, i, c: (jnp.maximum((b * n_s + i) * rpb - 1, 0), c)),
            pl.BlockSpec((CONV_W, tc), lambda b, i, c: (0, c)),
        ],
        out_specs=pl.BlockSpec((ts, tc), lambda b, i, c: (b * n_s + i, c)),
        out_shape=jax.ShapeDtypeStruct((t, ch), BF16),
        scratch_shapes=[pltpu.VMEM((prev_rows + ts, tc), F32)],
        compiler_params=_cparams(("parallel", "parallel", "parallel")),
        name="conv_prep",
    )(qkv, qkv, conv_w)


def _gates_kernel(ba_ref, alog_ref, dtb_ref, g_ref, b_ref, dec_ref, *, ts):
    hdot = functools.partial(jnp.dot, precision=HIGHEST, preferred_element_type=F32)
    ba = ba_ref[...]
    lane = lax.broadcasted_iota(jnp.int32, (ts, LANES), 1)
    head_lane = lane < A_HEADS
    beta = jnp.where(head_lane, jax.nn.sigmoid(ba[:, :LANES]), 0.0)
    g = -jnp.exp(alog_ref[...]) * jax.nn.softplus(ba[:, LANES:] + dtb_ref[...])
    g = jnp.where(head_lane, g, 0.0)
    r = lax.broadcasted_iota(jnp.int32, (ts, ts), 0)
    c = lax.broadcasted_iota(jnp.int32, (ts, ts), 1)
    same_chunk = (r >> CHUNK_SHIFT) == (c >> CHUNK_SHIFT)
    tri = jnp.where((c <= r) & same_chunk, 1.0, 0.0).astype(F32)
    gcum = hdot(tri, g)
    er = lax.broadcasted_iota(jnp.int32, (LANES, A_WIDTH), 0)
    ec = lax.broadcasted_iota(jnp.int32, (LANES, A_WIDTH), 1)
    spread = jnp.where(er == (ec >> (A_DV.bit_length() - 1)), 1.0, 0.0).astype(F32)
    g_ref[...] = hdot(gcum, spread)
    b_ref[...] = hdot(beta, spread)
    wd = A_HEADS * CHUNK
    er = lax.broadcasted_iota(jnp.int32, (LANES, wd), 0)
    ec = lax.broadcasted_iota(jnp.int32, (LANES, wd), 1)
    gi = hdot(gcum, jnp.where(er == (ec >> CHUNK_SHIFT), 1.0, 0.0).astype(F32))
    ipos = lax.broadcasted_iota(jnp.int32, (ts, wd), 0) & (CHUNK - 1)
    jpos = lax.broadcasted_iota(jnp.int32, (ts, wd), 1) & (CHUNK - 1)
    blk = jnp.where(same_chunk, 1.0, 0.0).astype(F32)
    gj = hdot(blk, jnp.where(ipos == jpos, gi, 0.0))
    dec_ref[...] = jnp.exp(jnp.where(ipos >= jpos, gi - gj, NEG_BIG))


def _gates(ba, a_log, dt_bias, ts=256):
    t = ba.shape[0]
    ts = _pick(t, ts)
    pad = lambda v: jnp.pad(v.astype(F32), (0, LANES - A_HEADS)).reshape(1, LANES)
    out = jax.ShapeDtypeStruct((t, A_WIDTH), F32)
    wd = A_HEADS * CHUNK
    return pl.pallas_call(
        functools.partial(_gates_kernel, ts=ts),
        grid=(t // ts,),
        in_specs=[pl.BlockSpec((ts, 2 * LANES), lambda i: (i, 0)),
                  pl.BlockSpec((1, LANES), lambda i: (0, 0)), pl.BlockSpec((1, LANES), lambda i: (0, 0))],
        out_specs=[pl.BlockSpec((ts, A_WIDTH), lambda i: (i, 0)), pl.BlockSpec((ts, A_WIDTH), lambda i: (i, 0)),
                   pl.BlockSpec((ts, wd), lambda i: (i, 0))],
        out_shape=[out, out, jax.ShapeDtypeStruct((t, wd), F32)],
        compiler_params=_cparams(("parallel",)),
        name="gates",
    )(ba, pad(a_log), pad(dt_bias))


def _delta_solve_kernel(q_ref, k_ref, v_ref, g_ref, b_ref, dec_ref, u_ref, wq_ref, kd_ref, at_ref, egl_ref,
                        *, ts, hp, group=8):
    ri = lax.broadcasted_iota(jnp.int32, (CHUNK, CHUNK), 0)
    ci = lax.broadcasted_iota(jnp.int32, (CHUNK, CHUNK), 1)
    strict = ri > ci
    ident = jnp.where(ri == ci, 1.0, 0.0).astype(F32)

    def load(n, h):
        rows = slice(n * CHUNK, (n + 1) * CHUNK)
        cols = slice(h * A_DK, (h + 1) * A_DK)
        c = dict(n=n, rows=rows, cols=cols, dcols=slice(h * CHUNK, (h + 1) * CHUNK))
        c["q"] = q_ref[rows, cols].astype(F32)
        c["k"] = k_ref[rows, cols].astype(F32)
        c["gb"] = g_ref[rows, cols]
        c["bt"] = b_ref[rows, cols]
        c["kb"] = c["k"] * c["bt"]
        c["eg"] = jnp.exp(c["gb"])
        return c

    chains = [(n, h) for n in range(ts // CHUNK) for h in range(hp)]
    for g0 in range(0, len(chains), group):
        cs = [load(n, h) for n, h in chains[g0:g0 + group]]
        for c in cs:
            qk_kk = _nt_dot(jnp.concatenate([c["q"], c["kb"]], axis=0).astype(BF16), c["k"].astype(BF16))
            dec = dec_ref[c["rows"], c["dcols"]]
            c["attn"] = qk_kk[:CHUNK] * dec
            c["a"] = jnp.where(strict, qk_kk[CHUNK:] * dec, 0.0)
        for c in cs:
            c["inv"] = ident - c["a"]
            c["p"] = _bdot(c["a"], c["a"])
        for _ in range(CHUNK_SHIFT - 2):
            for c in cs:
                y = _bdot(jnp.concatenate([c["inv"], c["p"]], axis=0), c["p"])
                c["inv"] = c["inv"] + y[:CHUNK]
                c["p"] = y[CHUNK:]
        for c in cs:
            c["inv"] = c["inv"] + _bdot(c["inv"], c["p"])
        for c in cs:
            v = v_ref[c["rows"], c["cols"]].astype(F32)
            c["uw"] = _bdot(c["inv"], jnp.concatenate([v * c["bt"], c["kb"] * c["eg"]], axis=1))
        for c in cs:
            n, rows, cols = c["n"], c["rows"], c["cols"]
            g_last = c["gb"][CHUNK - 1:CHUNK, :]
            u_ref[rows, cols] = c["uw"][:, :A_DV].astype(u_ref.dtype)
            wq_ref[2 * n * CHUNK:(2 * n + 1) * CHUNK, cols] = c["uw"][:, A_DV:].astype(wq_ref.dtype)
            wq_ref[(2 * n + 1) * CHUNK:(2 * n + 2) * CHUNK, cols] = (c["q"] * c["eg"]).astype(wq_ref.dtype)
            kd_ref[rows, cols] = (c["k"] * jnp.exp(g_last - c["gb"])).astype(kd_ref.dtype)
            at_ref[rows, c["dcols"]] = c["attn"].astype(at_ref.dtype)
            egl_ref[n * SUBLANES:(n + 1) * SUBLANES, cols] = jnp.broadcast_to(jnp.exp(g_last), (SUBLANES, A_DV))


def _delta_solve(qkv, gcum, beta, dec, ts=512, hp=4):
    t = qkv.shape[0]
    ts = _pick(t, ts)
    hw = hp * A_DK
    n_hb = A_WIDTH // hw
    spec = lambda off: pl.BlockSpec((ts, hw), lambda i, hb: (i, off * n_hb + hb))
    wide = jax.ShapeDtypeStruct((t, A_WIDTH), BF16)
    return pl.pallas_call(
        functools.partial(_delta_solve_kernel, ts=ts, hp=hp),
        grid=(t // ts, n_hb),
        in_specs=[spec(0), spec(1), spec(2), spec(0), spec(0),
                  pl.BlockSpec((ts, hp * CHUNK), lambda i, hb: (i, hb))],
        out_specs=[spec(0),
                   pl.BlockSpec((2 * ts, hw), lambda i, hb: (i, hb)),
                   spec(0),
                   pl.BlockSpec((ts, hp * CHUNK), lambda i, hb: (i, hb)),
                   pl.BlockSpec((ts // CHUNK * SUBLANES, hw), lambda i, hb: (i, hb))],
        out_shape=[wide, jax.ShapeDtypeStruct((2 * t, A_WIDTH), BF16), wide,
                   jax.ShapeDtypeStruct((t, A_HEADS * CHUNK), BF16),
                   jax.ShapeDtypeStruct((t // CHUNK * SUBLANES, A_WIDTH), F32)],
        compiler_params=_cparams(("parallel", "parallel")),
        name="delta_solve",
    )(qkv, qkv, qkv, gcum, beta, dec)


def _delta_rec_kernel(u_ref, wq_ref, kd_ref, at_ref, egl_ref, z_ref, gain_ref, o_ref, s_ref, *, ts):
    @pl.when(pl.program_id(1) == 0)
    def _():
        s_ref[...] = jnp.zeros_like(s_ref)

    gain = gain_ref[...]

    heads = range(A_HEADS)
    col = lambda h: slice(h * A_DK, (h + 1) * A_DK)
    for n in range(ts // CHUNK):
        rows = slice(n * CHUNK, (n + 1) * CHUNK)
        rows2 = slice(2 * n * CHUNK, (2 * n + 2) * CHUNK)
        s = [s_ref[h] for h in heads]
        ws = [jnp.dot(wq_ref[rows2, col(h)], s[h].astype(BF16), preferred_element_type=F32)
              for h in heads]
        v_new = [(u_ref[rows, col(h)].astype(F32) - ws[h][:CHUNK]).astype(BF16) for h in heads]
        o = [ws[h][CHUNK:] + jnp.dot(at_ref[rows, h * CHUNK:(h + 1) * CHUNK], v_new[h], preferred_element_type=F32)
             for h in heads]
        for h in heads:
            eg_last = egl_ref[n * SUBLANES:n * SUBLANES + 1, col(h)]
            s_ref[h] = s[h] * eg_last + _tn_dot(kd_ref[rows, col(h)], v_new[h])
        for h in heads:
            on = o[h] * lax.rsqrt(jnp.mean(o[h] * o[h], axis=-1, keepdims=True) + EPS) * gain
            z = z_ref[rows, col(h)].astype(F32)
            o_ref[rows, col(h)] = (on * (z * jax.nn.sigmoid(z))).astype(o_ref.dtype)


def _delta_rec(u, wq, kd, at, egl, z, out_gain, batch, seq, ts=512):
    t = u.shape[0]
    ts = _pick(seq, ts)
    n_s = seq // ts
    row = lambda b, i: (b * n_s + i, 0)
    return pl.pallas_call(
        functools.partial(_delta_rec_kernel, ts=ts),
        grid=(batch, n_s),
        in_specs=[pl.BlockSpec((ts, A_WIDTH), row), pl.BlockSpec((2 * ts, A_WIDTH), row),
                  pl.BlockSpec((ts, A_WIDTH), row), pl.BlockSpec((ts, A_HEADS * CHUNK), row),
                  pl.BlockSpec((ts // CHUNK * SUBLANES, A_WIDTH), row), pl.BlockSpec((ts, A_WIDTH), row),
                  pl.BlockSpec((1, A_DV), lambda b, i: (0, 0))],
        out_specs=pl.BlockSpec((ts, A_WIDTH), row),
        out_shape=jax.ShapeDtypeStruct((t, A_WIDTH), BF16),
        scratch_shapes=[pltpu.VMEM((A_HEADS, A_DK, A_DV), F32)],
        compiler_params=_cparams(("parallel", "arbitrary")),
        name="delta_rec",
    )(u, wq, kd, at, egl, z, out_gain.reshape(1, A_DV).astype(F32))


def _gmlp_kernel(u_ref, v_ref, gain_ref, w_ref, bias_ref, o_ref, *, nb):
    ri = lax.broadcasted_iota(jnp.int32, (GMLP_CHUNK, GMLP_CHUNK), 0)
    ci = lax.broadcasted_iota(jnp.int32, (GMLP_CHUNK, GMLP_CHUNK), 1)
    tril = ri >= ci
    gain = gain_ref[...]
    for blk in range(nb):
        rows = slice(blk * GMLP_CHUNK, (blk + 1) * GMLP_CHUNK)
        u = jax.nn.gelu(u_ref[rows, :].astype(F32))
        v = jax.nn.gelu(v_ref[rows, :].astype(F32))
        vn = (v * lax.rsqrt(jnp.mean(v * v, axis=-1, keepdims=True) + EPS) * gain).astype(BF16)
        for g in range(GMLP_GROUPS):
            cols = slice(g * GMLP_GDIM, (g + 1) * GMLP_GDIM)
            w = jnp.where(tril, w_ref[g], 0.0).astype(BF16)
            mixed = jnp.dot(w, vn[:, cols], preferred_element_type=F32) + bias_ref[:, cols]
            o_ref[rows, cols] = (u[:, cols] * mixed).astype(o_ref.dtype)


def _gmlp(uv, norm_gain, w_spatial, b_spatial, nb=4):
    t = uv.shape[0]
    tm = nb * GMLP_CHUNK
    bias = jnp.repeat(b_spatial.T.astype(F32), GMLP_GDIM, axis=1)
    return pl.pallas_call(
        functools.partial(_gmlp_kernel, nb=nb),
        grid=(t // tm,),
        in_specs=[pl.BlockSpec((tm, GMLP_WIDTH), lambda i: (i, 0)),
                  pl.BlockSpec((tm, GMLP_WIDTH), lambda i: (i, 1)),
                  pl.BlockSpec((1, GMLP_WIDTH), lambda i: (0, 0)),
                  pl.BlockSpec((GMLP_GROUPS, GMLP_CHUNK, GMLP_CHUNK), lambda i: (0, 0, 0)),
                  pl.BlockSpec((GMLP_CHUNK, GMLP_WIDTH), lambda i: (0, 0))],
        out_specs=pl.BlockSpec((tm, GMLP_WIDTH), lambda i: (i, 0)),
        out_shape=jax.ShapeDtypeStruct((t, GMLP_WIDTH), BF16),
        compiler_params=_cparams(("parallel",)),
        name="gmlp",
    )(uv, uv, norm_gain.reshape(1, GMLP_WIDTH).astype(F32), w_spatial.astype(F32), bias)


def _band_kernel(q_ref, kp_ref, kc_ref, vp_ref, vc_ref, qg_ref, kg_ref, bias_ref, o_ref, *, tq):
    i = pl.program_id(2)

    def norm(x, gain):
        x = x.astype(F32)
        return x * lax.rsqrt(jnp.mean(x * x, axis=-1, keepdims=True) + EPS) * gain

    qn = norm(q_ref[...], qg_ref[...]).astype(BF16)
    kcat = jnp.concatenate([norm(kp_ref[...], kg_ref[...]), norm(kc_ref[...], kg_ref[...])], axis=0).astype(BF16)
    vcat = jnp.concatenate([vp_ref[...], vc_ref[...]], axis=0).astype(BF16)
    bias = bias_ref[0]
    pad = LEFT_CHUNKS * CHUNK
    kpos = lax.broadcasted_iota(jnp.int32, (CHUNK, BAND), 1)
    chunks = range(tq // CHUNK)
    lo = [c * CHUNK + (tq - pad) for c in chunks]
    s = [_nt_dot(qn[c * CHUNK:(c + 1) * CHUNK], kcat[lo[c]:lo[c] + BAND]) for c in chunks]
    p = []
    for c in chunks:
        sc = s[c] * (C_DH ** -0.5) + bias
        valid = (i > 0) | (kpos + c * CHUNK >= pad)
        sc = jnp.where(valid, sc, NEG_BIG)
        e = jnp.exp(sc - jnp.max(sc, axis=-1, keepdims=True))
        p.append((e / jnp.sum(e, axis=-1, keepdims=True)).astype(BF16))
    o = [jnp.dot(p[c], vcat[lo[c]:lo[c] + BAND], preferred_element_type=F32) for c in chunks]
    for c in chunks:
        o_ref[c * CHUNK:(c + 1) * CHUNK, :] = o[c].astype(o_ref.dtype)


def _band_bias(rel_bias):
    diag = np.arange(-(CHUNK - 1), BAND)
    idx = np.clip(LEFT_CHUNKS * CHUNK - diag, -MAX_REL, MAX_REL) + MAX_REL
    vec = rel_bias.astype(F32)[:, idx]
    return jnp.stack([vec[:, CHUNK - 1 - i:CHUNK - 1 - i + BAND] for i in range(CHUNK)], axis=1)


def _band_attention(qkv, q_gain, k_gain, rel_bias, batch, seq):
    t = qkv.shape[0]
    tq = LEFT_CHUNKS * CHUNK
    n_s = seq // tq
    bias = _band_bias(rel_bias)
    cur = lambda off: pl.BlockSpec((tq, C_DH), lambda b, h, i: (b * n_s + i, off * C_HEADS + h))
    prv = lambda off: pl.BlockSpec((tq, C_DH), lambda b, h, i: (b * n_s + jnp.maximum(i - 1, 0), off * C_HEADS + h))
    vec = pl.BlockSpec((1, C_DH), lambda b, h, i: (0, 0))
    return pl.pallas_call(
        functools.partial(_band_kernel, tq=tq),
        grid=(batch, C_HEADS, n_s),
        in_specs=[cur(0), prv(1), cur(1), prv(2), cur(2), vec, vec,
                  pl.BlockSpec((1, CHUNK, BAND), lambda b, h, i: (h, 0, 0))],
        out_specs=pl.BlockSpec((tq, C_DH), lambda b, h, i: (b * n_s + i, h)),
        out_shape=jax.ShapeDtypeStruct((t, C_WIDTH), BF16),
        compiler_params=_cparams(("parallel", "parallel", "parallel")),
        name="band_attention",
    )(qkv, qkv, qkv, qkv, qkv, q_gain.reshape(1, C_DH).astype(F32), k_gain.reshape(1, C_DH).astype(F32), bias)


def _merge_kernel(ya_ref, yb_ref, yc_ref, pa_ref, pb_ref, pc_ref, ga_ref, gb_ref, gc_ref, o_ref):
    def branch(y_ref, p_ref, g_ref):
        gate = jax.nn.sigmoid(g_ref[...].astype(F32))
        return gate * jnp.dot(y_ref[...], p_ref[...], preferred_element_type=F32)

    merged = branch(ya_ref, pa_ref, ga_ref) + branch(yb_ref, pb_ref, gb_ref) + branch(yc_ref, pc_ref, gc_ref)
    o_ref[...] = merged.astype(o_ref.dtype)


def _merge(ya, yb, yc, pa, pb, pc, gate, tm=1024, tn=512):
    t, k = ya.shape
    d = pa.shape[1]
    tm, tn = _pick(t, tm), _pick(d, tn)
    nd = d // tn
    ysp = pl.BlockSpec((tm, k), lambda j, i: (i, 0))
    psp = pl.BlockSpec((k, tn), lambda j, i: (0, j))
    gsp = lambda br: pl.BlockSpec((tm, tn), lambda j, i: (i, br * nd + j))
    return pl.pallas_call(
        _merge_kernel,
        grid=(nd, t // tm),
        in_specs=[ysp, ysp, ysp, psp, psp, psp, gsp(0), gsp(1), gsp(2)],
        out_specs=pl.BlockSpec((tm, tn), lambda j, i: (i, j)),
        out_shape=jax.ShapeDtypeStruct((t, d), BF16),
        compiler_params=_cparams(("parallel", "parallel")),
        name="merge",
    )(ya, yb, yc, pa, pb, pc, gate, gate, gate)


def _top_values(x, k, out_ref, want_rank=False):
    cur = x
    rank = jnp.full(x.shape, float(k), F32) if want_rank else None
    for r in range(k):
        m = jnp.max(cur, axis=0, keepdims=True)
        out_ref[r:r + 1, :] = m
        hit = cur == m
        if want_rank:
            rank = jnp.where(hit, float(r), rank)
        if r + 1 < k:
            cur = jnp.where(hit, NEG_BIG, cur)
    return rank


def _peer_select_kernel(q_ref, keys_ref, cnt_ref, f1_ref, r2_ref, e2_ref, a_ref, b_ref, c_ref, *, tt):
    k = PEER_TOPK

    def head(h, carry):
        qh = q_ref[:, pl.ds(pl.multiple_of(h * PEER_QDIM, PEER_QDIM), PEER_QDIM)]
        s1 = _nt_dot(keys_ref[h, 0], qh[:, :PEER_QHALF], precision=HIGHEST)
        s2 = _nt_dot(keys_ref[h, 1], qh[:, PEER_QHALF:], precision=HIGHEST)
        _top_values(s1, k, a_ref)
        rank2 = _top_values(s2, k, b_ref, want_rank=True)
        av = a_ref[...]
        bv = b_ref[...]
        half = k // 2
        cand = jnp.concatenate([av[0:1] + bv] + [av[r:r + 1] + bv[:half] for r in range(1, half)]
                               + [av[half:] + bv[0:1]], axis=0)
        _top_values(cand, k, c_ref)
        cv = c_ref[...]
        z = jnp.sum(jnp.exp(cv - cv[0:1, :]), axis=0, keepdims=True)
        tau = cv[k - 1:k, :]
        cnt = jnp.zeros(s1.shape, F32)
        for c in range(k):
            ok = av + bv[c:c + 1] >= tau
            thr = jnp.min(jnp.where(ok, av, -NEG_BIG), axis=0, keepdims=True)
            cnt = cnt + jnp.where(s1 >= thr, 1.0, 0.0)
        cnt_ref[:, pl.ds(h, 1), :] = cnt[:, None, :]
        f1_ref[:, pl.ds(h, 1), :] = (jnp.exp(s1 - av[0:1]) / z)[:, None, :]
        rank2 = rank2.astype(r2_ref.dtype)
        e2 = jnp.exp(s2 - bv[0:1]).astype(e2_ref.dtype)
        rb = 2 * SUBLANES
        for g in range(PEER_NKEYS // rb):
            for tl in range(tt // LANES):
                r2_ref[h * (PEER_NKEYS // rb) + g, tl] = rank2[g * rb:(g + 1) * rb, tl * LANES:(tl + 1) * LANES]
                e2_ref[h * (PEER_NKEYS // rb) + g, tl] = e2[g * rb:(g + 1) * rb, tl * LANES:(tl + 1) * LANES]
        return carry

    lax.fori_loop(0, PEER_HEADS, head, 0)


def _peer_select(q, keys, tt=256):
    t = q.shape[0]
    tt = _pick(t, tt)
    shape = (PEER_NKEYS, PEER_HEADS, t)
    bspec = pl.BlockSpec((PEER_NKEYS, PEER_HEADS, tt), lambda i: (0, 0, i))
    n_slab = PEER_HEADS * PEER_NKEYS // (2 * SUBLANES)
    flat = (n_slab, t // LANES, 2 * SUBLANES, LANES)
    flat_spec = pl.BlockSpec((n_slab, tt // LANES, 2 * SUBLANES, LANES), lambda i: (0, i, 0, 0))
    return pl.pallas_call(
        functools.partial(_peer_select_kernel, tt=tt),
        grid=(t // tt,),
        in_specs=[pl.BlockSpec((tt, PEER_HEADS * PEER_QDIM), lambda i: (i, 0)),
                  pl.BlockSpec((PEER_HEADS, 2, PEER_NKEYS, PEER_QHALF), lambda i: (0, 0, 0, 0))],
        out_specs=[bspec, bspec, flat_spec, flat_spec],
        out_shape=[jax.ShapeDtypeStruct(shape, F32), jax.ShapeDtypeStruct(shape, F32),
                   jax.ShapeDtypeStruct(flat, BF16), jax.ShapeDtypeStruct(flat, BF16)],
        scratch_shapes=[pltpu.VMEM((PEER_TOPK, tt), F32), pltpu.VMEM((PEER_TOPK, tt), F32),
                        pltpu.VMEM((PEER_TOPK, tt), F32)],
        compiler_params=_cparams(("parallel",)),
        name="peer_select",
    )(q, keys.astype(F32))


def _peer_dense_kernel(hn_ref, u_ref, vt_ref, cnt_ref, f1_ref, r2_ref, e2_ref, x_ref, o_ref,
                       acc_ref, ht0_ref, ht1_ref, g0_ref, g1_ref, *, tt, ec, nc, d):
    s = pl.program_id(0)
    c_out = lax.rem(jnp.maximum(s - 2, 0), nc)

    @pl.when(s == 0)
    def _():
        for ref in (ht0_ref, ht1_ref, g0_ref, g1_ref):
            ref[...] = jnp.zeros_like(ref)

    @pl.when(c_out == 0)
    def _():
        acc_ref[...] = jnp.zeros_like(acc_ref)

    nk = PEER_NKEYS

    def stages(ht_w, ht_r, g_w, g_r):
        ht_w[...] = _nt_dot(u_ref[...], hn_ref[...]).astype(ht_w.dtype)
        acc_ref[...] += jnp.dot(vt_ref[...], g_r[...], preferred_element_type=F32)
        rb = 2 * SUBLANES
        for ii in range(ec // nk):
            for tg in range(tt // LANES):
                lanes = slice(tg * LANES, (tg + 1) * LANES)
                bcast = lambda ref, h: jnp.broadcast_to(ref[ii, h:h + 1, lanes], (rb, LANES)).astype(BF16)
                cnt = [bcast(cnt_ref, h) for h in range(PEER_HEADS)]
                f1 = [bcast(f1_ref, h) for h in range(PEER_HEADS)]
                for j0 in range(0, nk, rb):
                    wsel = jnp.zeros((rb, LANES), BF16)
                    for h in range(PEER_HEADS):
                        slab = (h * nk + j0) // rb
                        picked = jnp.clip(cnt[h] - r2_ref[slab, tg], 0, 1)
                        wsel = wsel + picked * (f1[h] * e2_ref[slab, tg])
                    rows = slice(ii * nk + j0, ii * nk + j0 + rb)
                    g_w[rows, lanes] = wsel * jax.nn.gelu(ht_r[rows, lanes])

    parity = lax.rem(s, 2)
    pl.when(parity == 0)(functools.partial(stages, ht0_ref, ht1_ref, g1_ref, g0_ref))
    pl.when(parity == 1)(functools.partial(stages, ht1_ref, ht0_ref, g0_ref, g1_ref))

    @pl.when((c_out == nc - 1) & (s >= 2))
    def _():
        step = 512
        for d0 in range(0, d, step):
            o_ref[:, d0:d0 + step] = x_ref[:, d0:d0 + step] + acc_ref[d0:d0 + step, :].T


def _peer_dense(hn, u_tab, vt_tab, cntr, f1r, r2, e2, x, tt=512, ec=1024):
    t, d = hn.shape
    e = u_tab.shape[0]
    tt, ec = _pick(t, tt), _pick(e, ec)
    ni = ec // PEER_NKEYS
    nc = e // ec
    n_steps = (t // tt) * nc
    n_slab = r2.shape[0]
    tile = lambda s, lag: jnp.clip(s - lag, 0, n_steps - 1) // nc
    chunk = lambda s, lag: jnp.clip(s - lag, 0, n_steps - 1) % nc
    once = dict(pipeline_mode=pl.Buffered(1))
    return pl.pallas_call(
        functools.partial(_peer_dense_kernel, tt=tt, ec=ec, nc=nc, d=d),
        grid=(n_steps + 2,),
        in_specs=[pl.BlockSpec((tt, d), lambda s: (tile(s, 0), 0), **once),
                  pl.BlockSpec((ec, d), lambda s: (chunk(s, 0), 0)),
                  pl.BlockSpec((d, ec), lambda s: (0, chunk(s, 2))),
                  pl.BlockSpec((ni, PEER_HEADS, tt), lambda s: (chunk(s, 1), 0, tile(s, 1))),
                  pl.BlockSpec((ni, PEER_HEADS, tt), lambda s: (chunk(s, 1), 0, tile(s, 1))),
                  pl.BlockSpec((n_slab, tt // LANES, 2 * SUBLANES, LANES), lambda s: (0, tile(s, 1), 0, 0)),
                  pl.BlockSpec((n_slab, tt // LANES, 2 * SUBLANES, LANES), lambda s: (0, tile(s, 1), 0, 0)),
                  pl.BlockSpec((tt, d), lambda s: (tile(s, 2), 0), **once)],
        out_specs=pl.BlockSpec((tt, d), lambda s: (tile(s, 2), 0)),
        out_shape=jax.ShapeDtypeStruct((t, d), F32),
        scratch_shapes=[pltpu.VMEM((d, tt), F32)] + [pltpu.VMEM((ec, tt), BF16)] * 4,
        compiler_params=_cparams(("arbitrary",)),
        name="peer_dense",
    )(hn, u_tab, vt_tab, cntr, f1r, r2, e2, x)


def _layer(x, batch, seq, w_in, conv_w, a_log, dt_bias, a_out_gain, gmlp_norm, w_spatial, b_spatial,
           c_q_gain, c_k_gain, rel_bias, p_a, p_b, p_c, w_out, norm_mix, norm_ffn,
           peer_wq, peer_keys, peer_u, peer_v):
    bf = lambda w: w.astype(BF16)
    o_qkv_a = 0
    o_z = o_qkv_a + 3 * A_WIDTH
    o_beta = o_z + A_WIDTH
    o_alpha = o_beta + A_HEADS
    o_uv = o_alpha + A_HEADS
    o_qkv_c = o_uv + 2 * GMLP_WIDTH
    o_gate = o_qkv_c + 3 * C_WIDTH

    h = _rmsnorm(x, norm_mix)
    qkv_a = _matmul(h, bf(w_in[:, o_qkv_a:o_z]), BF16, name="proj_qkv_a")
    z_a = _matmul(h, bf(w_in[:, o_z:o_beta]), BF16, name="proj_z_a")
    pad_cols = lambda w: jnp.pad(w, ((0, 0), (0, LANES - w.shape[1])))
    w_ba = jnp.concatenate([pad_cols(w_in[:, o_beta:o_alpha]), pad_cols(w_in[:, o_alpha:o_uv])], axis=1)
    ba = _matmul(h, bf(w_ba), F32, name="proj_beta_alpha")
    uv_b = _matmul(h, bf(w_in[:, o_uv:o_qkv_c]), BF16, name="proj_uv_b")
    qkv_c = _matmul(h, bf(w_in[:, o_qkv_c:o_gate]), BF16, name="proj_qkv_c")
    gate = _matmul(h, bf(w_in[:, o_gate:]), BF16, name="proj_gate")

    qkv_prep = _conv_prep(qkv_a, conv_w.astype(F32), batch, seq)
    gcum, beta, dec = _gates(ba, a_log, dt_bias)
    u, wq, kd, at, egl = _delta_solve(qkv_prep, gcum, beta, dec)
    y_a = _delta_rec(u, wq, kd, at, egl, z_a, a_out_gain, batch, seq)
    y_b = _gmlp(uv_b, gmlp_norm, w_spatial, b_spatial)
    y_c = _band_attention(qkv_c, c_q_gain, c_k_gain, rel_bias, batch, seq)

    merged = _merge(y_a, y_b, y_c, bf(p_a), bf(p_b), bf(p_c), gate)
    x = _matmul(merged, bf(w_out), F32, residual=x, name="out_proj")

    hn = _rmsnorm(x, norm_ffn)
    q = _matmul(hn, bf(peer_wq), F32, name="peer_query")
    cnt, f1, r2, e2 = _peer_select(q, peer_keys)
    return _peer_dense(hn, bf(peer_u), bf(peer_v).T, cnt, f1, r2, e2, x)


def kernel(x, w_in, conv_w, a_log, dt_bias, a_out_gain, gmlp_norm, w_spatial, b_spatial, c_q_gain, c_k_gain,
           rel_bias, p_a, p_b, p_c, w_out, norm_mix, norm_ffn, peer_wq, peer_keys, peer_u, peer_v):
    batch, seq, d_model = x.shape
    xt = x.reshape(batch * seq, d_model)
    for l in range(w_in.shape[0]):
        xt = _layer(xt, batch, seq, w_in[l], conv_w[l], a_log[l], dt_bias[l], a_out_gain[l], gmlp_norm[l],
                    w_spatial[l], b_spatial[l], c_q_gain[l], c_k_gain[l], rel_bias[l], p_a[l], p_b[l], p_c[l],
                    w_out[l], norm_mix[l], norm_ffn[l], peer_wq[l], peer_keys[l], peer_u[l], peer_v[l])
    return xt.reshape(batch, seq, d_model)
```

```python
import functools

import jax
import jax.numpy as jnp
import numpy as np
from jax import lax
from jax.experimental import pallas as pl
from jax.experimental.pallas import tpu as pltpu

F32 = jnp.float32
BF16 = jnp.bfloat16
HIGHEST = lax.Precision.HIGHEST

CHUNK = 64
EPS = 1e-6
A_HEADS = 8
A_DK = 128
A_DV = 128
A_WIDTH = A_HEADS * A_DV
CONV_W = 4
GMLP_CHUNK = 128
GMLP_GROUPS = 8
GMLP_GDIM = 128
GMLP_WIDTH = GMLP_GROUPS * GMLP_GDIM
C_HEADS = 8
C_DH = 128
C_WIDTH = C_HEADS * C_DH
LEFT_CHUNKS = 8
BAND = (LEFT_CHUNKS + 1) * CHUNK
MAX_REL = 128
N_BRANCH = 3
PEER_HEADS = 8
PEER_NKEYS = 128
PEER_TOPK = 16
PEER_QDIM = 256
PEER_QHALF = PEER_QDIM // 2

LANES = 128
SUBLANES = 8
NEG_BIG = -1e30
VMEM_LIMIT = 56 * 1024 * 1024
CHUNK_SHIFT = CHUNK.bit_length() - 1


def _cparams(sem, vmem_limit=VMEM_LIMIT, flags=None):
    return pltpu.CompilerParams(dimension_semantics=sem, vmem_limit_bytes=vmem_limit, flags=flags)


def _nt_dot(a, b, precision=None):
    return lax.dot_general(a, b, (((1,), (1,)), ((), ())), precision=precision,
                           preferred_element_type=F32)


def _tn_dot(a, b, precision=None):
    return lax.dot_general(a, b, (((0,), (0,)), ((), ())), precision=precision,
                           preferred_element_type=F32)


def _bdot(a, b):
    return jnp.dot(a.astype(BF16), b.astype(BF16), preferred_element_type=F32)


def _pick(n, pref):
    t = min(pref, n)
    while n % t:
        t -= LANES if t > LANES else 8
    return t


def _rmsnorm_kernel(x_ref, g_ref, o_ref):
    x = x_ref[...]
    ms = jnp.mean(x * x, axis=-1, keepdims=True)
    o_ref[...] = (x * lax.rsqrt(ms + EPS) * g_ref[...]).astype(o_ref.dtype)


def _rmsnorm(x, gain, out_dtype=BF16, tm=512):
    m, d = x.shape
    tm = _pick(m, tm)
    return pl.pallas_call(
        _rmsnorm_kernel,
        grid=(m // tm,),
        in_specs=[pl.BlockSpec((tm, d), lambda i: (i, 0)), pl.BlockSpec((1, d), lambda i: (0, 0))],
        out_specs=pl.BlockSpec((tm, d), lambda i: (i, 0)),
        out_shape=jax.ShapeDtypeStruct((m, d), out_dtype),
        compiler_params=_cparams(("parallel",)),
        name="rmsnorm",
    )(x, gain.reshape(1, d))


def _mm_w32_kernel(*refs, shift, has_res):
    refs = list(refs)
    a_ref, w_ref = refs[:2]
    tail_ref = refs[2] if shift else None
    rest = refs[3:] if shift else refs[2:]
    r_ref = rest[0] if has_res else None
    o_ref, wb_ref = rest[-2:]

    @pl.when(pl.program_id(1) == 0)
    def _():
        w = w_ref[...]
        if shift:
            full = jnp.concatenate([w, tail_ref[...]], axis=1)
            w = pltpu.roll(full, full.shape[1] - shift, axis=1)[:, :w.shape[1]]
        wb_ref[...] = w.astype(wb_ref.dtype)

    acc = jnp.dot(a_ref[...], wb_ref[...], preferred_element_type=F32)
    if has_res:
        acc = r_ref[...] + acc
    o_ref[...] = acc.astype(o_ref.dtype)


def _matmul(a, w, layer, col0, n, out_dtype, residual=None, tm=1024, tn=1024, name="matmul"):
    m, k = a.shape
    tm, tn = _pick(m, tm), _pick(n, tn)
    shift = col0 % LANES
    base = col0 - shift
    assert base % tn == 0, (col0, tn)
    in_specs = [pl.BlockSpec((tm, k), lambda j, i: (i, 0)),
                pl.BlockSpec((None, k, tn), lambda j, i: (layer, 0, base // tn + j))]
    args = [a, w]
    if shift:
        in_specs.append(pl.BlockSpec((None, k, LANES), lambda j, i: (layer, 0, (base + (j + 1) * tn) // LANES)))
        args.append(w)
    if residual is not None:
        in_specs.append(pl.BlockSpec((tm, tn), lambda j, i: (i, j)))
        args.append(residual)
    return pl.pallas_call(
        functools.partial(_mm_w32_kernel, shift=shift, has_res=residual is not None),
        grid=(n // tn, m // tm),
        in_specs=in_specs,
        out_specs=pl.BlockSpec((tm, tn), lambda j, i: (i, j)),
        out_shape=jax.ShapeDtypeStruct((m, n), out_dtype),
        scratch_shapes=[pltpu.VMEM((k, tn), BF16)],
        compiler_params=_cparams(("parallel", "arbitrary")),
        name=name,
    )(*args)


def _conv_prep_kernel(cur_ref, prev_ref, w_ref, o_ref, ext_ref, *, ts, tc, prev_rows, n_qk_tiles, n_q_tiles):
    i = pl.program_id(1)
    c = pl.program_id(2)
    prev = prev_ref[...].astype(F32)
    prev = jnp.where(i == 0, 0.0, prev)
    cur = cur_ref[...].astype(F32)
    ext_ref[0:prev_rows, :] = prev
    ext_ref[prev_rows:prev_rows + ts, :] = cur
    w = w_ref[...]
    acc = cur * w[CONV_W - 1:CONV_W, :]
    for d in range(1, CONV_W):
        acc = acc + ext_ref[prev_rows - d:prev_rows - d + ts, :] * w[CONV_W - 1 - d:CONV_W - d, :]
    y = acc * jax.nn.sigmoid(acc)
    parts = []
    for h in range(tc // A_DK):
        yh = y[:, h * A_DK:(h + 1) * A_DK]
        parts.append(yh * lax.rsqrt(jnp.sum(yh * yh, axis=-1, keepdims=True) + EPS))
    yn = jnp.concatenate(parts, axis=-1)
    scale = jnp.where(c < n_q_tiles, A_DK ** -0.5, 1.0).astype(F32)
    o_ref[...] = jnp.where(c < n_qk_tiles, yn * scale, y).astype(o_ref.dtype)


def _conv_prep(qkv, conv_w, batch, seq, ts=512, tc=512):
    t, ch = qkv.shape
    ts = _pick(seq, ts)
    prev_rows = 16
    n_s = seq // ts
    kern = functools.partial(_conv_prep_kernel, ts=ts, tc=tc, prev_rows=prev_rows,
                             n_qk_tiles=2 * A_WIDTH // tc, n_q_tiles=A_WIDTH // tc)
    rpb = ts // prev_rows
    return pl.pallas_call(
        kern,
        grid=(batch, n_s, ch // tc),
        in_specs=[
            pl.BlockSpec((ts, tc), lambda b, i, c: (b * n_s + i, c)),
            pl.BlockSpec((prev_rows, tc), lambda b, i, c: (jnp.maximum((b * n_s + i) * rpb - 1, 0), c)),
            pl.BlockSpec((CONV_W, tc), lambda b, i, c: (0, c)),
        ],
        out_specs=pl.BlockSpec((ts, tc), lambda b, i, c: (b * n_s + i, c)),
        out_shape=jax.ShapeDtypeStruct((t, ch), BF16),
        scratch_shapes=[pltpu.VMEM((prev_rows + ts, tc), F32)],
        compiler_params=_cparams(("parallel", "parallel", "parallel")),
        name="conv_prep",
    )(qkv, qkv, conv_w)


def _gates_kernel(ba_ref, alog_ref, dtb_ref, g_ref, b_ref, dec_ref, *, ts):
    hdot = functools.partial(jnp.dot, precision=HIGHEST, preferred_element_type=F32)
    ba = ba_ref[...]
    lane = lax.broadcasted_iota(jnp.int32, (ts, LANES), 1)
    head_lane = lane < A_HEADS
    beta = jnp.where(head_lane, jax.nn.sigmoid(ba), 0.0)
    alpha = pltpu.roll(ba, LANES - A_HEADS, axis=1)
    g = -jnp.exp(alog_ref[...]) * jax.nn.softplus(alpha + dtb_ref[...])
    g = jnp.where(head_lane, g, 0.0)
    r = lax.broadcasted_iota(jnp.int32, (ts, ts), 0)
    c = lax.broadcasted_iota(jnp.int32, (ts, ts), 1)
    same_chunk = (r >> CHUNK_SHIFT) == (c >> CHUNK_SHIFT)
    tri = jnp.where((c <= r) & same_chunk, 1.0, 0.0).astype(F32)
    gcum = hdot(tri, g)
    er = lax.broadcasted_iota(jnp.int32, (LANES, A_WIDTH), 0)
    ec = lax.broadcasted_iota(jnp.int32, (LANES, A_WIDTH), 1)
    spread = jnp.where(er == (ec >> (A_DV.bit_length() - 1)), 1.0, 0.0).astype(F32)
    g_ref[...] = hdot(gcum, spread)
    b_ref[...] = hdot(beta, spread)
    wd = A_HEADS * CHUNK
    er = lax.broadcasted_iota(jnp.int32, (LANES, wd), 0)
    ec = lax.broadcasted_iota(jnp.int32, (LANES, wd), 1)
    gi = hdot(gcum, jnp.where(er == (ec >> CHUNK_SHIFT), 1.0, 0.0).astype(F32))
    ipos = lax.broadcasted_iota(jnp.int32, (ts, wd), 0) & (CHUNK - 1)
    jpos = lax.broadcasted_iota(jnp.int32, (ts, wd), 1) & (CHUNK - 1)
    blk = jnp.where(same_chunk, 1.0, 0.0).astype(F32)
    gj = hdot(blk, jnp.where(ipos == jpos, gi, 0.0))
    dec_ref[...] = jnp.exp(jnp.where(ipos >= jpos, gi - gj, NEG_BIG))


def _gates(ba, a_log, dt_bias, ts=256):
    t = ba.shape[0]
    ts = _pick(t, ts)
    pad = lambda v: jnp.pad(v.astype(F32), (0, LANES - A_HEADS)).reshape(1, LANES)
    out = jax.ShapeDtypeStruct((t, A_WIDTH), F32)
    wd = A_HEADS * CHUNK
    return pl.pallas_call(
        functools.partial(_gates_kernel, ts=ts),
        grid=(t // ts,),
        in_specs=[pl.BlockSpec((ts, LANES), lambda i: (i, 0)),
                  pl.BlockSpec((1, LANES), lambda i: (0, 0)), pl.BlockSpec((1, LANES), lambda i: (0, 0))],
        out_specs=[pl.BlockSpec((ts, A_WIDTH), lambda i: (i, 0)), pl.BlockSpec((ts, A_WIDTH), lambda i: (i, 0)),
                   pl.BlockSpec((ts, wd), lambda i: (i, 0))],
        out_shape=[out, out, jax.ShapeDtypeStruct((t, wd), F32)],
        compiler_params=_cparams(("parallel",)),
        name="gates",
    )(ba, pad(a_log), pad(dt_bias))


def _delta_solve_kernel(q_ref, k_ref, v_ref, g_ref, b_ref, dec_ref, u_ref, wq_ref, kd_ref, at_ref, egl_ref,
                        *, ts, hp, group=8):
    ri = lax.broadcasted_iota(jnp.int32, (CHUNK, CHUNK), 0)
    ci = lax.broadcasted_iota(jnp.int32, (CHUNK, CHUNK), 1)
    strict = ri > ci
    ident = jnp.where(ri == ci, 1.0, 0.0).astype(F32)

    def load(n, h):
        rows = slice(n * CHUNK, (n + 1) * CHUNK)
        cols = slice(h * A_DK, (h + 1) * A_DK)
        c = dict(n=n, rows=rows, cols=cols, dcols=slice(h * CHUNK, (h + 1) * CHUNK))
        c["q"] = q_ref[rows, cols].astype(F32)
        c["k"] = k_ref[rows, cols].astype(F32)
        c["gb"] = g_ref[rows, cols]
        c["bt"] = b_ref[rows, cols]
        c["kb"] = c["k"] * c["bt"]
        c["eg"] = jnp.exp(c["gb"])
        return c

    chains = [(n, h) for n in range(ts // CHUNK) for h in range(hp)]
    for g0 in range(0, len(chains), group):
        cs = [load(n, h) for n, h in chains[g0:g0 + group]]
        for c in cs:
            qk_kk = _nt_dot(jnp.concatenate([c["q"], c["kb"]], axis=0).astype(BF16), c["k"].astype(BF16))
            dec = dec_ref[c["rows"], c["dcols"]]
            c["attn"] = qk_kk[:CHUNK] * dec
            c["a"] = jnp.where(strict, qk_kk[CHUNK:] * dec, 0.0)
        for c in cs:
            c["inv"] = ident - c["a"]
            c["p"] = _bdot(c["a"], c["a"])
        for _ in range(CHUNK_SHIFT - 2):
            for c in cs:
                y = _bdot(jnp.concatenate([c["inv"], c["p"]], axis=0), c["p"])
                c["inv"] = c["inv"] + y[:CHUNK]
                c["p"] = y[CHUNK:]
        for c in cs:
            c["inv"] = c["inv"] + _bdot(c["inv"], c["p"])
        for c in cs:
            v = v_ref[c["rows"], c["cols"]].astype(F32)
            c["uw"] = _bdot(c["inv"], jnp.concatenate([v * c["bt"], c["kb"] * c["eg"]], axis=1))
        for c in cs:
            n, rows, cols = c["n"], c["rows"], c["cols"]
            g_last = c["gb"][CHUNK - 1:CHUNK, :]
            u_ref[rows, cols] = c["uw"][:, :A_DV].astype(u_ref.dtype)
            wq_ref[2 * n * CHUNK:(2 * n + 1) * CHUNK, cols] = c["uw"][:, A_DV:].astype(wq_ref.dtype)
            wq_ref[(2 * n + 1) * CHUNK:(2 * n + 2) * CHUNK, cols] = (c["q"] * c["eg"]).astype(wq_ref.dtype)
            kd_ref[rows, cols] = (c["k"] * jnp.exp(g_last - c["gb"])).astype(kd_ref.dtype)
            at_ref[rows, c["dcols"]] = c["attn"].astype(at_ref.dtype)
            egl_ref[n * SUBLANES:(n + 1) * SUBLANES, cols] = jnp.broadcast_to(jnp.exp(g_last), (SUBLANES, A_DV))


def _delta_solve(qkv, gcum, beta, dec, ts=512, hp=4):
    t = qkv.shape[0]
    ts = _pick(t, ts)
    hw = hp * A_DK
    n_hb = A_WIDTH // hw
    spec = lambda off: pl.BlockSpec((ts, hw), lambda i, hb: (i, off * n_hb + hb))
    wide = jax.ShapeDtypeStruct((t, A_WIDTH), BF16)
    return pl.pallas_call(
        functools.partial(_delta_solve_kernel, ts=ts, hp=hp),
        grid=(t // ts, n_hb),
        in_specs=[spec(0), spec(1), spec(2), spec(0), spec(0),
                  pl.BlockSpec((ts, hp * CHUNK), lambda i, hb: (i, hb))],
        out_specs=[spec(0),
                   pl.BlockSpec((2 * ts, hw), lambda i, hb: (i, hb)),
                   spec(0),
                   pl.BlockSpec((ts, hp * CHUNK), lambda i, hb: (i, hb)),
                   pl.BlockSpec((ts // CHUNK * SUBLANES, hw), lambda i, hb: (i, hb))],
        out_shape=[wide, jax.ShapeDtypeStruct((2 * t, A_WIDTH), BF16), wide,
                   jax.ShapeDtypeStruct((t, A_HEADS * CHUNK), BF16),
                   jax.ShapeDtypeStruct((t // CHUNK * SUBLANES, A_WIDTH), F32)],
        compiler_params=_cparams(("parallel", "parallel")),
        name="delta_solve",
    )(qkv, qkv, qkv, gcum, beta, dec)


def _delta_rec_kernel(u_ref, wq_ref, kd_ref, at_ref, egl_ref, z_ref, gain_ref, o_ref, s_ref, *, ts):
    @pl.when(pl.program_id(1) == 0)
    def _():
        s_ref[...] = jnp.zeros_like(s_ref)

    gain = gain_ref[...]

    heads = range(A_HEADS)
    col = lambda h: slice(h * A_DK, (h + 1) * A_DK)
    for n in range(ts // CHUNK):
        rows = slice(n * CHUNK, (n + 1) * CHUNK)
        rows2 = slice(2 * n * CHUNK, (2 * n + 2) * CHUNK)
        s = [s_ref[h] for h in heads]
        ws = [jnp.dot(wq_ref[rows2, col(h)], s[h].astype(BF16), preferred_element_type=F32)
              for h in heads]
        v_new = [(u_ref[rows, col(h)].astype(F32) - ws[h][:CHUNK]).astype(BF16) for h in heads]
        o = [ws[h][CHUNK:] + jnp.dot(at_ref[rows, h * CHUNK:(h + 1) * CHUNK], v_new[h], preferred_element_type=F32)
             for h in heads]
        for h in heads:
            eg_last = egl_ref[n * SUBLANES:n * SUBLANES + 1, col(h)]
            s_ref[h] = s[h] * eg_last + _tn_dot(kd_ref[rows, col(h)], v_new[h])
        for h in heads:
            on = o[h] * lax.rsqrt(jnp.mean(o[h] * o[h], axis=-1, keepdims=True) + EPS) * gain
            z = z_ref[rows, col(h)].astype(F32)
            o_ref[rows, col(h)] = (on * (z * jax.nn.sigmoid(z))).astype(o_ref.dtype)


def _delta_rec(u, wq, kd, at, egl, z, out_gain, batch, seq, ts=512):
    t = u.shape[0]
    ts = _pick(seq, ts)
    n_s = seq // ts
    row = lambda b, i: (b * n_s + i, 0)
    return pl.pallas_call(
        functools.partial(_delta_rec_kernel, ts=ts),
        grid=(batch, n_s),
        in_specs=[pl.BlockSpec((ts, A_WIDTH), row), pl.BlockSpec((2 * ts, A_WIDTH), row),
                  pl.BlockSpec((ts, A_WIDTH), row), pl.BlockSpec((ts, A_HEADS * CHUNK), row),
                  pl.BlockSpec((ts // CHUNK * SUBLANES, A_WIDTH), row), pl.BlockSpec((ts, A_WIDTH), row),
                  pl.BlockSpec((1, A_DV), lambda b, i: (0, 0))],
        out_specs=pl.BlockSpec((ts, A_WIDTH), row),
        out_shape=jax.ShapeDtypeStruct((t, A_WIDTH), BF16),
        scratch_shapes=[pltpu.VMEM((A_HEADS, A_DK, A_DV), F32)],
        compiler_params=_cparams(("parallel", "arbitrary")),
        name="delta_rec",
    )(u, wq, kd, at, egl, z, out_gain.reshape(1, A_DV).astype(F32))


def _gmlp_kernel(u_ref, v_ref, gain_ref, w_ref, bias_ref, o_ref, *, nb):
    ri = lax.broadcasted_iota(jnp.int32, (GMLP_CHUNK, GMLP_CHUNK), 0)
    ci = lax.broadcasted_iota(jnp.int32, (GMLP_CHUNK, GMLP_CHUNK), 1)
    tril = ri >= ci
    gain = gain_ref[...]
    for blk in range(nb):
        rows = slice(blk * GMLP_CHUNK, (blk + 1) * GMLP_CHUNK)
        u = jax.nn.gelu(u_ref[rows, :].astype(F32))
        v = jax.nn.gelu(v_ref[rows, :].astype(F32))
        vn = (v * lax.rsqrt(jnp.mean(v * v, axis=-1, keepdims=True) + EPS) * gain).astype(BF16)
        for g in range(GMLP_GROUPS):
            cols = slice(g * GMLP_GDIM, (g + 1) * GMLP_GDIM)
            w = jnp.where(tril, w_ref[g], 0.0).astype(BF16)
            mixed = jnp.dot(w, vn[:, cols], preferred_element_type=F32) + bias_ref[:, cols]
            o_ref[rows, cols] = (u[:, cols] * mixed).astype(o_ref.dtype)


def _gmlp(uv, norm_gain, w_spatial, b_spatial, nb=4):
    t = uv.shape[0]
    tm = nb * GMLP_CHUNK
    bias = jnp.repeat(b_spatial.T.astype(F32), GMLP_GDIM, axis=1)
    return pl.pallas_call(
        functools.partial(_gmlp_kernel, nb=nb),
        grid=(t // tm,),
        in_specs=[pl.BlockSpec((tm, GMLP_WIDTH), lambda i: (i, 0)),
                  pl.BlockSpec((tm, GMLP_WIDTH), lambda i: (i, 1)),
                  pl.BlockSpec((1, GMLP_WIDTH), lambda i: (0, 0)),
                  pl.BlockSpec((GMLP_GROUPS, GMLP_CHUNK, GMLP_CHUNK), lambda i: (0, 0, 0)),
                  pl.BlockSpec((GMLP_CHUNK, GMLP_WIDTH), lambda i: (0, 0))],
        out_specs=pl.BlockSpec((tm, GMLP_WIDTH), lambda i: (i, 0)),
        out_shape=jax.ShapeDtypeStruct((t, GMLP_WIDTH), BF16),
        compiler_params=_cparams(("parallel",)),
        name="gmlp",
    )(uv, uv, norm_gain.reshape(1, GMLP_WIDTH).astype(F32), w_spatial.astype(F32), bias)


def _band_kernel(q_ref, kp_ref, kc_ref, vp_ref, vc_ref, qg_ref, kg_ref, bias_ref, o_ref, *, tq):
    i = pl.program_id(2)

    def norm(x, gain):
        x = x.astype(F32)
        return x * lax.rsqrt(jnp.mean(x * x, axis=-1, keepdims=True) + EPS) * gain

    qn = norm(q_ref[...], qg_ref[...]).astype(BF16)
    kcat = jnp.concatenate([norm(kp_ref[...], kg_ref[...]), norm(kc_ref[...], kg_ref[...])], axis=0).astype(BF16)
    vcat = jnp.concatenate([vp_ref[...], vc_ref[...]], axis=0).astype(BF16)
    bias = bias_ref[0]
    pad = LEFT_CHUNKS * CHUNK
    kpos = lax.broadcasted_iota(jnp.int32, (CHUNK, BAND), 1)
    chunks = range(tq // CHUNK)
    lo = [c * CHUNK + (tq - pad) for c in chunks]
    s = [_nt_dot(qn[c * CHUNK:(c + 1) * CHUNK], kcat[lo[c]:lo[c] + BAND]) for c in chunks]
    p = []
    for c in chunks:
        sc = s[c] * (C_DH ** -0.5) + bias
        valid = (i > 0) | (kpos + c * CHUNK >= pad)
        sc = jnp.where(valid, sc, NEG_BIG)
        e = jnp.exp(sc - jnp.max(sc, axis=-1, keepdims=True))
        p.append((e / jnp.sum(e, axis=-1, keepdims=True)).astype(BF16))
    o = [jnp.dot(p[c], vcat[lo[c]:lo[c] + BAND], preferred_element_type=F32) for c in chunks]
    for c in chunks:
        o_ref[c * CHUNK:(c + 1) * CHUNK, :] = o[c].astype(o_ref.dtype)


def _band_bias(rel_bias):
    diag = np.arange(-(CHUNK - 1), BAND)
    idx = np.clip(LEFT_CHUNKS * CHUNK - diag, -MAX_REL, MAX_REL) + MAX_REL
    vec = rel_bias.astype(F32)[:, idx]
    return jnp.stack([vec[:, CHUNK - 1 - i:CHUNK - 1 - i + BAND] for i in range(CHUNK)], axis=1)


def _band_attention(qkv, q_gain, k_gain, rel_bias, batch, seq):
    t = qkv.shape[0]
    tq = LEFT_CHUNKS * CHUNK
    n_s = seq // tq
    bias = _band_bias(rel_bias)
    cur = lambda off: pl.BlockSpec((tq, C_DH), lambda b, h, i: (b * n_s + i, off * C_HEADS + h))
    prv = lambda off: pl.BlockSpec((tq, C_DH), lambda b, h, i: (b * n_s + jnp.maximum(i - 1, 0), off * C_HEADS + h))
    vec = pl.BlockSpec((1, C_DH), lambda b, h, i: (0, 0))
    return pl.pallas_call(
        functools.partial(_band_kernel, tq=tq),
        grid=(batch, C_HEADS, n_s),
        in_specs=[cur(0), prv(1), cur(1), prv(2), cur(2), vec, vec,
                  pl.BlockSpec((1, CHUNK, BAND), lambda b, h, i: (h, 0, 0))],
        out_specs=pl.BlockSpec((tq, C_DH), lambda b, h, i: (b * n_s + i, h)),
        out_shape=jax.ShapeDtypeStruct((t, C_WIDTH), BF16),
        compiler_params=_cparams(("parallel", "parallel", "parallel")),
        name="band_attention",
    )(qkv, qkv, qkv, qkv, qkv, q_gain.reshape(1, C_DH).astype(F32), k_gain.reshape(1, C_DH).astype(F32), bias)


def _merge_kernel(ya_ref, yb_ref, yc_ref, pa_ref, pb_ref, pc_ref, ga_ref, gb_ref, gc_ref, o_ref,
                  wa_ref, wb_ref, wc_ref):
    @pl.when(pl.program_id(1) == 0)
    def _():
        for p_ref, w_ref in ((pa_ref, wa_ref), (pb_ref, wb_ref), (pc_ref, wc_ref)):
            w_ref[...] = p_ref[...].astype(w_ref.dtype)

    def branch(y_ref, w_ref, g_ref):
        gate = jax.nn.sigmoid(g_ref[...].astype(F32))
        return gate * jnp.dot(y_ref[...], w_ref[...], preferred_element_type=F32)

    merged = branch(ya_ref, wa_ref, ga_ref) + branch(yb_ref, wb_ref, gb_ref) + branch(yc_ref, wc_ref, gc_ref)
    o_ref[...] = merged.astype(o_ref.dtype)


def _merge(ya, yb, yc, pa, pb, pc, layer, gate, tm=1024, tn=512):
    t, k = ya.shape
    d = pa.shape[2]
    tm, tn = _pick(t, tm), _pick(d, tn)
    nd = d // tn
    ysp = pl.BlockSpec((tm, k), lambda j, i: (i, 0))
    psp = pl.BlockSpec((None, k, tn), lambda j, i: (layer, 0, j))
    gsp = lambda br: pl.BlockSpec((tm, tn), lambda j, i: (i, br * nd + j))
    return pl.pallas_call(
        _merge_kernel,
        grid=(nd, t // tm),
        in_specs=[ysp, ysp, ysp, psp, psp, psp, gsp(0), gsp(1), gsp(2)],
        out_specs=pl.BlockSpec((tm, tn), lambda j, i: (i, j)),
        out_shape=jax.ShapeDtypeStruct((t, d), BF16),
        scratch_shapes=[pltpu.VMEM((k, tn), BF16)] * 3,
        compiler_params=_cparams(("parallel", "arbitrary")),
        name="merge",
    )(ya, yb, yc, pa, pb, pc, gate, gate, gate)


def _top_values(x, k, out_ref, want_rank=False):
    cur = x
    rank = jnp.full(x.shape, float(k), F32) if want_rank else None
    for r in range(k):
        m = jnp.max(cur, axis=0, keepdims=True)
        out_ref[r:r + 1, :] = m
        hit = cur == m
        if want_rank:
            rank = jnp.where(hit, float(r), rank)
        if r + 1 < k:
            cur = jnp.where(hit, NEG_BIG, cur)
    return rank


def _peer_select_kernel(q_ref, keys_ref, cnt_ref, f1_ref, r2_ref, e2_ref, a_ref, b_ref, c_ref, *, tt):
    k = PEER_TOPK

    def head(h, carry):
        qh = q_ref[:, pl.ds(pl.multiple_of(h * PEER_QDIM, PEER_QDIM), PEER_QDIM)]
        s1 = _nt_dot(keys_ref[h, 0], qh[:, :PEER_QHALF], precision=HIGHEST)
        s2 = _nt_dot(keys_ref[h, 1], qh[:, PEER_QHALF:], precision=HIGHEST)
        _top_values(s1, k, a_ref)
        rank2 = _top_values(s2, k, b_ref, want_rank=True)
        av = a_ref[...]
        bv = b_ref[...]
        half = k // 2
        cand = jnp.concatenate([av[0:1] + bv] + [av[r:r + 1] + bv[:half] for r in range(1, half)]
                               + [av[half:] + bv[0:1]], axis=0)
        _top_values(cand, k, c_ref)
        cv = c_ref[...]
        z = jnp.sum(jnp.exp(cv - cv[0:1, :]), axis=0, keepdims=True)
        tau = cv[k - 1:k, :]
        cnt = jnp.zeros(s1.shape, F32)
        for c in range(k):
            ok = av + bv[c:c + 1] >= tau
            thr = jnp.min(jnp.where(ok, av, -NEG_BIG), axis=0, keepdims=True)
            cnt = cnt + jnp.where(s1 >= thr, 1.0, 0.0)
        cnt_ref[:, pl.ds(h, 1), :] = cnt[:, None, :]
        f1_ref[:, pl.ds(h, 1), :] = (jnp.exp(s1 - av[0:1]) / z)[:, None, :]
        rank2 = rank2.astype(r2_ref.dtype)
        e2 = jnp.exp(s2 - bv[0:1]).astype(e2_ref.dtype)
        rb = 2 * SUBLANES
        for g in range(PEER_NKEYS // rb):
            for tl in range(tt // LANES):
                r2_ref[h * (PEER_NKEYS // rb) + g, tl] = rank2[g * rb:(g + 1) * rb, tl * LANES:(tl + 1) * LANES]
                e2_ref[h * (PEER_NKEYS // rb) + g, tl] = e2[g * rb:(g + 1) * rb, tl * LANES:(tl + 1) * LANES]
        return carry

    lax.fori_loop(0, PEER_HEADS, head, 0)


def _peer_select(q, keys, tt=256):
    t = q.shape[0]
    tt = _pick(t, tt)
    shape = (PEER_NKEYS, PEER_HEADS, t)
    bspec = pl.BlockSpec((PEER_NKEYS, PEER_HEADS, tt), lambda i: (0, 0, i))
    n_slab = PEER_HEADS * PEER_NKEYS // (2 * SUBLANES)
    flat = (n_slab, t // LANES, 2 * SUBLANES, LANES)
    flat_spec = pl.BlockSpec((n_slab, tt // LANES, 2 * SUBLANES, LANES), lambda i: (0, i, 0, 0))
    return pl.pallas_call(
        functools.partial(_peer_select_kernel, tt=tt),
        grid=(t // tt,),
        in_specs=[pl.BlockSpec((tt, PEER_HEADS * PEER_QDIM), lambda i: (i, 0)),
                  pl.BlockSpec((PEER_HEADS, 2, PEER_NKEYS, PEER_QHALF), lambda i: (0, 0, 0, 0))],
        out_specs=[bspec, bspec, flat_spec, flat_spec],
        out_shape=[jax.ShapeDtypeStruct(shape, F32), jax.ShapeDtypeStruct(shape, F32),
                   jax.ShapeDtypeStruct(flat, BF16), jax.ShapeDtypeStruct(flat, BF16)],
        scratch_shapes=[pltpu.VMEM((PEER_TOPK, tt), F32), pltpu.VMEM((PEER_TOPK, tt), F32),
                        pltpu.VMEM((PEER_TOPK, tt), F32)],
        compiler_params=_cparams(("parallel",)),
        name="peer_select",
    )(q, keys.astype(F32))


def _peer_dense_kernel(hn_ref, u_ref, vt_ref, cnt_ref, f1_ref, r2_ref, e2_ref, x_ref, o_ref,
                       acc_ref, ht0_ref, ht1_ref, g0_ref, g1_ref, *, tt, ec, nc, d):
    s = pl.program_id(0)
    c_out = lax.rem(jnp.maximum(s - 2, 0), nc)

    @pl.when(s == 0)
    def _():
        for ref in (ht0_ref, ht1_ref, g0_ref, g1_ref):
            ref[...] = jnp.zeros_like(ref)

    @pl.when(c_out == 0)
    def _():
        acc_ref[...] = jnp.zeros_like(acc_ref)

    nk = PEER_NKEYS

    def stages(ht_w, ht_r, g_w, g_r):
        halves = [slice(0, tt // 2), slice(tt // 2, tt)]

        def stage_a(hs):
            ht_w[:, hs] = _nt_dot(u_ref[...], hn_ref[hs, :]).astype(ht_w.dtype)

        def stage_c(hs):
            acc_ref[:, hs] += jnp.dot(vt_ref[...], g_r[:, hs], preferred_element_type=F32)

        mxu_pieces = [functools.partial(stage_a, halves[0]), functools.partial(stage_a, halves[1]),
                      functools.partial(stage_c, halves[0]), functools.partial(stage_c, halves[1])]
        rb = 2 * SUBLANES
        tiles = [(ii, tg) for ii in range(ec // nk) for tg in range(tt // LANES)]
        per_piece = len(tiles) // len(mxu_pieces)
        for t_idx, (ii, tg) in enumerate(tiles):
            if t_idx % per_piece == 0:
                mxu_pieces[t_idx // per_piece]()
            if True:
                lanes = slice(tg * LANES, (tg + 1) * LANES)
                bcast = lambda ref, h: jnp.broadcast_to(ref[ii, h:h + 1, lanes], (rb, LANES)).astype(BF16)
                cnt = [bcast(cnt_ref, h) for h in range(PEER_HEADS)]
                f1 = [bcast(f1_ref, h) for h in range(PEER_HEADS)]
                for j0 in range(0, nk, rb):
                    wsel = jnp.zeros((rb, LANES), BF16)
                    for h in range(PEER_HEADS):
                        slab = (h * nk + j0) // rb
                        picked = jnp.clip(cnt[h] - r2_ref[slab, tg], 0, 1)
                        wsel = wsel + picked * (f1[h] * e2_ref[slab, tg])
                    rows = slice(ii * nk + j0, ii * nk + j0 + rb)
                    g_w[rows, lanes] = wsel * jax.nn.gelu(ht_r[rows, lanes])

    parity = lax.rem(s, 2)
    pl.when(parity == 0)(functools.partial(stages, ht0_ref, ht1_ref, g1_ref, g0_ref))
    pl.when(parity == 1)(functools.partial(stages, ht1_ref, ht0_ref, g0_ref, g1_ref))

    @pl.when((c_out == nc - 1) & (s >= 2))
    def _():
        step = 512
        for d0 in range(0, d, step):
            o_ref[:, d0:d0 + step] = x_ref[:, d0:d0 + step] + acc_ref[d0:d0 + step, :].T


def _peer_dense(hn, u_tab, vt_tab, cntr, f1r, r2, e2, x, tt=512, ec=1024):
    t, d = hn.shape
    e = u_tab.shape[0]
    tt, ec = _pick(t, tt), _pick(e, ec)
    ni = ec // PEER_NKEYS
    nc = e // ec
    n_steps = (t // tt) * nc
    n_slab = r2.shape[0]
    tile = lambda s, lag: jnp.clip(s - lag, 0, n_steps - 1) // nc
    chunk = lambda s, lag: jnp.clip(s - lag, 0, n_steps - 1) % nc
    once = dict(pipeline_mode=pl.Buffered(1))
    return pl.pallas_call(
        functools.partial(_peer_dense_kernel, tt=tt, ec=ec, nc=nc, d=d),
        grid=(n_steps + 2,),
        in_specs=[pl.BlockSpec((tt, d), lambda s: (tile(s, 0), 0), **once),
                  pl.BlockSpec((ec, d), lambda s: (chunk(s, 0), 0)),
                  pl.BlockSpec((d, ec), lambda s: (0, chunk(s, 2))),
                  pl.BlockSpec((ni, PEER_HEADS, tt), lambda s: (chunk(s, 1), 0, tile(s, 1))),
                  pl.BlockSpec((ni, PEER_HEADS, tt), lambda s: (chunk(s, 1), 0, tile(s, 1))),
                  pl.BlockSpec((n_slab, tt // LANES, 2 * SUBLANES, LANES), lambda s: (0, tile(s, 1), 0, 0)),
                  pl.BlockSpec((n_slab, tt // LANES, 2 * SUBLANES, LANES), lambda s: (0, tile(s, 1), 0, 0)),
                  pl.BlockSpec((tt, d), lambda s: (tile(s, 2), 0), **once)],
        out_specs=pl.BlockSpec((tt, d), lambda s: (tile(s, 2), 0)),
        out_shape=jax.ShapeDtypeStruct((t, d), F32),
        scratch_shapes=[pltpu.VMEM((d, tt), F32)] + [pltpu.VMEM((ec, tt), BF16)] * 4,
        compiler_params=_cparams(("arbitrary",)),
        name="peer_dense",
    )(hn, u_tab, vt_tab, cntr, f1r, r2, e2, x)


def _layer(x, batch, seq, layer, w_in, conv_w, a_log, dt_bias, a_out_gain, gmlp_norm, w_spatial, b_spatial,
           c_q_gain, c_k_gain, rel_bias, p_a, p_b, p_c, w_out, norm_mix, norm_ffn,
           peer_wq, peer_keys, peer_u, peer_v):
    bf = lambda w: w.astype(BF16)
    o_qkv_a = 0
    o_z = o_qkv_a + 3 * A_WIDTH
    o_beta = o_z + A_WIDTH
    o_uv = o_beta + 2 * A_HEADS
    o_qkv_c = o_uv + 2 * GMLP_WIDTH
    o_gate = o_qkv_c + 3 * C_WIDTH
    d_model = x.shape[1]

    h = _rmsnorm(x, norm_mix)
    proj = lambda col0, n, dtype, name: _matmul(h, w_in, layer, col0, n, dtype, name=name)
    qkv_a = proj(o_qkv_a, 3 * A_WIDTH, BF16, "proj_qkv_a")
    z_a = proj(o_z, A_WIDTH, BF16, "proj_z_a")
    ba = proj(o_beta, LANES, F32, "proj_beta_alpha")
    uv_b = proj(o_uv, 2 * GMLP_WIDTH, BF16, "proj_uv_b")
    qkv_c = proj(o_qkv_c, 3 * C_WIDTH, BF16, "proj_qkv_c")
    gate = proj(o_gate, N_BRANCH * d_model, BF16, "proj_gate")

    qkv_prep = _conv_prep(qkv_a, conv_w.astype(F32), batch, seq)
    gcum, beta, dec = _gates(ba, a_log, dt_bias)
    u, wq, kd, at, egl = _delta_solve(qkv_prep, gcum, beta, dec)
    y_a = _delta_rec(u, wq, kd, at, egl, z_a, a_out_gain, batch, seq)
    y_b = _gmlp(uv_b, gmlp_norm, w_spatial, b_spatial)
    y_c = _band_attention(qkv_c, c_q_gain, c_k_gain, rel_bias, batch, seq)

    merged = _merge(y_a, y_b, y_c, p_a, p_b, p_c, layer, gate)
    x = _matmul(merged, w_out, layer, 0, d_model, F32, residual=x, name="out_proj")

    hn = _rmsnorm(x, norm_ffn)
    q = _matmul(hn, peer_wq, layer, 0, peer_wq.shape[2], F32, name="peer_query")
    cnt, f1, r2, e2 = _peer_select(q, peer_keys)
    return _peer_dense(hn, bf(peer_u), bf(peer_v).T, cnt, f1, r2, e2, x)


def kernel(x, w_in, conv_w, a_log, dt_bias, a_out_gain, gmlp_norm, w_spatial, b_spatial, c_q_gain, c_k_gain,
           rel_bias, p_a, p_b, p_c, w_out, norm_mix, norm_ffn, peer_wq, peer_keys, peer_u, peer_v):
    batch, seq, d_model = x.shape
    xt = x.reshape(batch * seq, d_model)
    for l in range(w_in.shape[0]):
        xt = _layer(xt, batch, seq, l, w_in, conv_w[l], a_log[l], dt_bias[l], a_out_gain[l], gmlp_norm[l],
                    w_spatial[l], b_spatial[l], c_q_gain[l], c_k_gain[l], rel_bias[l], p_a, p_b, p_c,
                    w_out, norm_mix[l], norm_ffn[l], peer_wq, peer_keys[l], peer_u[l], peer_v[l])
    return xt.reshape(batch, seq, d_model)
```

```python
import functools

import jax
import jax.numpy as jnp
import numpy as np
from jax import lax
from jax.experimental import pallas as pl
from jax.experimental.pallas import tpu as pltpu

F32 = jnp.float32
BF16 = jnp.bfloat16
HIGHEST = lax.Precision.HIGHEST

CHUNK = 64
EPS = 1e-6
A_HEADS = 8
A_DK = 128
A_DV = 128
A_WIDTH = A_HEADS * A_DV
CONV_W = 4
GMLP_CHUNK = 128
GMLP_GROUPS = 8
GMLP_GDIM = 128
GMLP_WIDTH = GMLP_GROUPS * GMLP_GDIM
C_HEADS = 8
C_DH = 128
C_WIDTH = C_HEADS * C_DH
LEFT_CHUNKS = 8
BAND = (LEFT_CHUNKS + 1) * CHUNK
MAX_REL = 128
N_BRANCH = 3
PEER_HEADS = 8
PEER_NKEYS = 128
PEER_TOPK = 16
PEER_QDIM = 256
PEER_QHALF = PEER_QDIM // 2

LANES = 128
SUBLANES = 8
NEG_BIG = -1e30
VMEM_LIMIT = 56 * 1024 * 1024
CHUNK_SHIFT = CHUNK.bit_length() - 1


def _cparams(sem, vmem_limit=VMEM_LIMIT, flags=None):
    return pltpu.CompilerParams(dimension_semantics=sem, vmem_limit_bytes=vmem_limit, flags=flags)


def _nt_dot(a, b, precision=None):
    return lax.dot_general(a, b, (((1,), (1,)), ((), ())), precision=precision,
                           preferred_element_type=F32)


def _tn_dot(a, b, precision=None):
    return lax.dot_general(a, b, (((0,), (0,)), ((), ())), precision=precision,
                           preferred_element_type=F32)


def _bdot(a, b):
    return jnp.dot(a.astype(BF16), b.astype(BF16), preferred_element_type=F32)


def _pick(n, pref):
    t = min(pref, n)
    while n % t:
        t -= LANES if t > LANES else 8
    return t


def _rmsnorm_kernel(x_ref, g_ref, o_ref):
    x = x_ref[...]
    ms = jnp.mean(x * x, axis=-1, keepdims=True)
    o_ref[...] = (x * lax.rsqrt(ms + EPS) * g_ref[...]).astype(o_ref.dtype)


def _rmsnorm(x, gain, out_dtype=BF16, tm=512):
    m, d = x.shape
    tm = _pick(m, tm)
    return pl.pallas_call(
        _rmsnorm_kernel,
        grid=(m // tm,),
        in_specs=[pl.BlockSpec((tm, d), lambda i: (i, 0)), pl.BlockSpec((1, d), lambda i: (0, 0))],
        out_specs=pl.BlockSpec((tm, d), lambda i: (i, 0)),
        out_shape=jax.ShapeDtypeStruct((m, d), out_dtype),
        compiler_params=_cparams(("parallel",)),
        name="rmsnorm",
    )(x, gain.reshape(1, d))


def _mm_w32_kernel(*refs, shift, has_res, w_is_nk):
    refs = list(refs)
    a_ref, w_ref = refs[:2]
    tail_ref = refs[2] if shift else None
    rest = refs[3:] if shift else refs[2:]
    r_ref = rest[0] if has_res else None
    o_ref, wb_ref = rest[-2:]

    @pl.when(pl.program_id(1) == 0)
    def _():
        w = w_ref[...]
        if shift:
            w = jnp.concatenate([w[shift:], tail_ref[:shift]], axis=0)
        wb_ref[...] = w.astype(wb_ref.dtype)

    if w_is_nk:
        acc = _nt_dot(a_ref[...], wb_ref[...])
    else:
        acc = jnp.dot(a_ref[...], wb_ref[...], preferred_element_type=F32)
    if has_res:
        acc = r_ref[...] + acc
    o_ref[...] = acc.astype(o_ref.dtype)


def _matmul(a, w, layer, col0, n, out_dtype, residual=None, w_is_nk=False, tm=1024, tn=1024, name="matmul"):
    m, k = a.shape
    tm, tn = _pick(m, tm), _pick(n, tn)
    shift = col0 % LANES
    base = col0 - shift
    assert base % tn == 0 and shift % SUBLANES == 0 and (w_is_nk or not shift), (col0, tn)
    in_specs = [pl.BlockSpec((tm, k), lambda j, i: (i, 0))]
    if w_is_nk:
        in_specs.append(pl.BlockSpec((None, tn, k), lambda j, i: (layer, base // tn + j, 0)))
        wb_shape = (tn, k)
    else:
        in_specs.append(pl.BlockSpec((None, k, tn), lambda j, i: (layer, 0, base // tn + j)))
        wb_shape = (k, tn)
    args = [a, w]
    if shift:
        in_specs.append(pl.BlockSpec((None, LANES, k), lambda j, i: (layer, (base + (j + 1) * tn) // LANES, 0)))
        args.append(w)
    if residual is not None:
        in_specs.append(pl.BlockSpec((tm, tn), lambda j, i: (i, j)))
        args.append(residual)
    return pl.pallas_call(
        functools.partial(_mm_w32_kernel, shift=shift, has_res=residual is not None, w_is_nk=w_is_nk),
        grid=(n // tn, m // tm),
        in_specs=in_specs,
        out_specs=pl.BlockSpec((tm, tn), lambda j, i: (i, j)),
        out_shape=jax.ShapeDtypeStruct((m, n), out_dtype),
        scratch_shapes=[pltpu.VMEM(wb_shape, BF16)],
        compiler_params=_cparams(("parallel", "arbitrary")),
        name=name,
    )(*args)


def _conv_prep_kernel(cur_ref, prev_ref, w_ref, o_ref, ext_ref, *, ts, tc, prev_rows, n_qk_tiles, n_q_tiles):
    i = pl.program_id(1)
    c = pl.program_id(2)
    prev = prev_ref[...].astype(F32)
    prev = jnp.where(i == 0, 0.0, prev)
    cur = cur_ref[...].astype(F32)
    ext_ref[0:prev_rows, :] = prev
    ext_ref[prev_rows:prev_rows + ts, :] = cur
    w = w_ref[...]
    acc = cur * w[CONV_W - 1:CONV_W, :]
    for d in range(1, CONV_W):
        acc = acc + ext_ref[prev_rows - d:prev_rows - d + ts, :] * w[CONV_W - 1 - d:CONV_W - d, :]
    y = acc * jax.nn.sigmoid(acc)
    parts = []
    for h in range(tc // A_DK):
        yh = y[:, h * A_DK:(h + 1) * A_DK]
        parts.append(yh * lax.rsqrt(jnp.sum(yh * yh, axis=-1, keepdims=True) + EPS))
    yn = jnp.concatenate(parts, axis=-1)
    scale = jnp.where(c < n_q_tiles, A_DK ** -0.5, 1.0).astype(F32)
    o_ref[...] = jnp.where(c < n_qk_tiles, yn * scale, y).astype(o_ref.dtype)


def _conv_prep(qkv, conv_w, batch, seq, ts=512, tc=512):
    t, ch = qkv.shape
    ts = _pick(seq, ts)
    prev_rows = 16
    n_s = seq // ts
    kern = functools.partial(_conv_prep_kernel, ts=ts, tc=tc, prev_rows=prev_rows,
                             n_qk_tiles=2 * A_WIDTH // tc, n_q_tiles=A_WIDTH // tc)
    rpb = ts // prev_rows
    return pl.pallas_call(
        kern,
        grid=(batch, n_s, ch // tc),
        in_specs=[
            pl.BlockSpec((ts, tc), lambda b, i, c: (b * n_s + i, c)),
            pl.BlockSpec((prev_rows, tc), lambda b, i, c: (jnp.maximum((b * n_s + i) * rpb - 1, 0), c)),
            pl.BlockSpec((CONV_W, tc), lambda b, i, c: (0, c)),
        ],
        out_specs=pl.BlockSpec((ts, tc), lambda b, i, c: (b * n_s + i, c)),
        out_shape=jax.ShapeDtypeStruct((t, ch), BF16),
        scratch_shapes=[pltpu.VMEM((prev_rows + ts, tc), F32)],
        compiler_params=_cparams(("parallel", "parallel", "parallel")),
        name="conv_prep",
    )(qkv, qkv, conv_w)


def _split3(x):
    hi = x.astype(BF16)
    r = x - hi.astype(F32)
    mid = r.astype(BF16)
    return hi, mid, (r - mid.astype(F32)).astype(BF16)


def _gates_kernel(ba_ref, alog_ref, dtb_ref, g_ref, b_ref, dec_ref, *, ts):
    def hdot(a, b):
        if a.dtype == BF16:
            return sum(jnp.dot(a, p, preferred_element_type=F32) for p in _split3(b))
        return sum(jnp.dot(p, b, preferred_element_type=F32) for p in _split3(a))

    ba = ba_ref[...]
    lane = lax.broadcasted_iota(jnp.int32, (ts, LANES), 1)
    head_lane = lane < A_HEADS
    beta = jnp.where(head_lane, jax.nn.sigmoid(ba), 0.0)
    alpha = pltpu.roll(ba, LANES - A_HEADS, axis=1)
    g = -jnp.exp(alog_ref[...]) * jax.nn.softplus(alpha + dtb_ref[...])
    g = jnp.where(head_lane, g, 0.0)
    r = lax.broadcasted_iota(jnp.int32, (ts, ts), 0)
    c = lax.broadcasted_iota(jnp.int32, (ts, ts), 1)
    same_chunk = (r >> CHUNK_SHIFT) == (c >> CHUNK_SHIFT)
    tri = jnp.where((c <= r) & same_chunk, 1.0, 0.0).astype(BF16)
    gcum = hdot(tri, g)
    er = lax.broadcasted_iota(jnp.int32, (LANES, A_WIDTH), 0)
    ec = lax.broadcasted_iota(jnp.int32, (LANES, A_WIDTH), 1)
    spread = jnp.where(er == (ec >> (A_DV.bit_length() - 1)), 1.0, 0.0).astype(BF16)
    g_ref[...] = hdot(gcum, spread)
    b_ref[...] = hdot(beta, spread)
    wd = A_HEADS * CHUNK
    er = lax.broadcasted_iota(jnp.int32, (LANES, wd), 0)
    ec = lax.broadcasted_iota(jnp.int32, (LANES, wd), 1)
    gi = hdot(gcum, jnp.where(er == (ec >> CHUNK_SHIFT), 1.0, 0.0).astype(BF16))
    ipos = lax.broadcasted_iota(jnp.int32, (ts, wd), 0) & (CHUNK - 1)
    jpos = lax.broadcasted_iota(jnp.int32, (ts, wd), 1) & (CHUNK - 1)
    blk = jnp.where(same_chunk, 1.0, 0.0).astype(BF16)
    gj = hdot(blk, jnp.where(ipos == jpos, gi, 0.0))
    dec_ref[...] = jnp.exp(jnp.where(ipos >= jpos, gi - gj, NEG_BIG))


def _gates(ba, a_log, dt_bias, ts=256):
    t = ba.shape[0]
    ts = _pick(t, ts)
    pad = lambda v: jnp.pad(v.astype(F32), (0, LANES - A_HEADS)).reshape(1, LANES)
    out = jax.ShapeDtypeStruct((t, A_WIDTH), F32)
    wd = A_HEADS * CHUNK
    return pl.pallas_call(
        functools.partial(_gates_kernel, ts=ts),
        grid=(t // ts,),
        in_specs=[pl.BlockSpec((ts, LANES), lambda i: (i, 0)),
                  pl.BlockSpec((1, LANES), lambda i: (0, 0)), pl.BlockSpec((1, LANES), lambda i: (0, 0))],
        out_specs=[pl.BlockSpec((ts, A_WIDTH), lambda i: (i, 0)), pl.BlockSpec((ts, A_WIDTH), lambda i: (i, 0)),
                   pl.BlockSpec((ts, wd), lambda i: (i, 0))],
        out_shape=[out, out, jax.ShapeDtypeStruct((t, wd), F32)],
        compiler_params=_cparams(("parallel",)),
        name="gates",
    )(ba, pad(a_log), pad(dt_bias))


def _delta_solve_kernel(q_ref, k_ref, v_ref, g_ref, b_ref, dec_ref, u_ref, wq_ref, kd_ref, at_ref, egl_ref,
                        *, ts, hp, group=8):
    ri = lax.broadcasted_iota(jnp.int32, (CHUNK, CHUNK), 0)
    ci = lax.broadcasted_iota(jnp.int32, (CHUNK, CHUNK), 1)
    strict = ri > ci
    ident = jnp.where(ri == ci, 1.0, 0.0).astype(F32)

    def load(n, h):
        rows = slice(n * CHUNK, (n + 1) * CHUNK)
        cols = slice(h * A_DK, (h + 1) * A_DK)
        c = dict(n=n, rows=rows, cols=cols, dcols=slice(h * CHUNK, (h + 1) * CHUNK))
        c["q"] = q_ref[rows, cols].astype(F32)
        c["k"] = k_ref[rows, cols].astype(F32)
        c["gb"] = g_ref[rows, cols]
        c["bt"] = b_ref[rows, cols]
        c["kb"] = c["k"] * c["bt"]
        c["eg"] = jnp.exp(c["gb"])
        return c

    chains = [(n, h) for n in range(ts // CHUNK) for h in range(hp)]
    for g0 in range(0, len(chains), group):
        cs = [load(n, h) for n, h in chains[g0:g0 + group]]
        for c in cs:
            qk_kk = _nt_dot(jnp.concatenate([c["q"], c["kb"]], axis=0).astype(BF16), c["k"].astype(BF16))
            dec = dec_ref[c["rows"], c["dcols"]]
            c["attn"] = qk_kk[:CHUNK] * dec
            c["a"] = jnp.where(strict, qk_kk[CHUNK:] * dec, 0.0)
        for c in cs:
            c["inv"] = ident - c["a"]
            c["p"] = _bdot(c["a"], c["a"])
        for _ in range(CHUNK_SHIFT - 2):
            for c in cs:
                y = _bdot(jnp.concatenate([c["inv"], c["p"]], axis=0), c["p"])
                c["inv"] = c["inv"] + y[:CHUNK]
                c["p"] = y[CHUNK:]
        for c in cs:
            c["inv"] = c["inv"] + _bdot(c["inv"], c["p"])
        for c in cs:
            v = v_ref[c["rows"], c["cols"]].astype(F32)
            c["uw"] = _bdot(c["inv"], jnp.concatenate([v * c["bt"], c["kb"] * c["eg"]], axis=1))
        for c in cs:
            n, rows, cols = c["n"], c["rows"], c["cols"]
            g_last = c["gb"][CHUNK - 1:CHUNK, :]
            u_ref[rows, cols] = c["uw"][:, :A_DV].astype(u_ref.dtype)
            wq_ref[2 * n * CHUNK:(2 * n + 1) * CHUNK, cols] = c["uw"][:, A_DV:].astype(wq_ref.dtype)
            wq_ref[(2 * n + 1) * CHUNK:(2 * n + 2) * CHUNK, cols] = (c["q"] * c["eg"]).astype(wq_ref.dtype)
            kd_ref[rows, cols] = (c["k"] * jnp.exp(g_last - c["gb"])).astype(kd_ref.dtype)
            at_ref[rows, c["dcols"]] = c["attn"].astype(at_ref.dtype)
            egl_ref[n * SUBLANES:(n + 1) * SUBLANES, cols] = jnp.broadcast_to(jnp.exp(g_last), (SUBLANES, A_DV))


def _delta_solve(qkv, gcum, beta, dec, ts=512, hp=4):
    t = qkv.shape[0]
    ts = _pick(t, ts)
    hw = hp * A_DK
    n_hb = A_WIDTH // hw
    spec = lambda off: pl.BlockSpec((ts, hw), lambda i, hb: (i, off * n_hb + hb))
    wide = jax.ShapeDtypeStruct((t, A_WIDTH), BF16)
    return pl.pallas_call(
        functools.partial(_delta_solve_kernel, ts=ts, hp=hp),
        grid=(t // ts, n_hb),
        in_specs=[spec(0), spec(1), spec(2), spec(0), spec(0),
                  pl.BlockSpec((ts, hp * CHUNK), lambda i, hb: (i, hb))],
        out_specs=[spec(0),
                   pl.BlockSpec((2 * ts, hw), lambda i, hb: (i, hb)),
                   spec(0),
                   pl.BlockSpec((ts, hp * CHUNK), lambda i, hb: (i, hb)),
                   pl.BlockSpec((ts // CHUNK * SUBLANES, hw), lambda i, hb: (i, hb))],
        out_shape=[wide, jax.ShapeDtypeStruct((2 * t, A_WIDTH), BF16), wide,
                   jax.ShapeDtypeStruct((t, A_HEADS * CHUNK), BF16),
                   jax.ShapeDtypeStruct((t // CHUNK * SUBLANES, A_WIDTH), F32)],
        compiler_params=_cparams(("parallel", "parallel")),
        name="delta_solve",
    )(qkv, qkv, qkv, gcum, beta, dec)


def _delta_rec_kernel(u_ref, wq_ref, kd_ref, at_ref, egl_ref, z_ref, gain_ref, o_ref, s_ref, *, ts):
    @pl.when(pl.program_id(1) == 0)
    def _():
        s_ref[...] = jnp.zeros_like(s_ref)

    gain = gain_ref[...]

    heads = range(A_HEADS)
    col = lambda h: slice(h * A_DK, (h + 1) * A_DK)
    for n in range(ts // CHUNK):
        rows = slice(n * CHUNK, (n + 1) * CHUNK)
        rows2 = slice(2 * n * CHUNK, (2 * n + 2) * CHUNK)
        s = [s_ref[h] for h in heads]
        ws = [jnp.dot(wq_ref[rows2, col(h)], s[h].astype(BF16), preferred_element_type=F32)
              for h in heads]
        v_new = [(u_ref[rows, col(h)].astype(F32) - ws[h][:CHUNK]).astype(BF16) for h in heads]
        o = [ws[h][CHUNK:] + jnp.dot(at_ref[rows, h * CHUNK:(h + 1) * CHUNK], v_new[h], preferred_element_type=F32)
             for h in heads]
        for h in heads:
            eg_last = egl_ref[n * SUBLANES:n * SUBLANES + 1, col(h)]
            s_ref[h] = s[h] * eg_last + _tn_dot(kd_ref[rows, col(h)], v_new[h])
        for h in heads:
            on = o[h] * lax.rsqrt(jnp.mean(o[h] * o[h], axis=-1, keepdims=True) + EPS) * gain
            z = z_ref[rows, col(h)].astype(F32)
            o_ref[rows, col(h)] = (on * (z * jax.nn.sigmoid(z))).astype(o_ref.dtype)


def _delta_rec(u, wq, kd, at, egl, z, out_gain, batch, seq, ts=512):
    t = u.shape[0]
    ts = _pick(seq, ts)
    n_s = seq // ts
    row = lambda b, i: (b * n_s + i, 0)
    return pl.pallas_call(
        functools.partial(_delta_rec_kernel, ts=ts),
        grid=(batch, n_s),
        in_specs=[pl.BlockSpec((ts, A_WIDTH), row), pl.BlockSpec((2 * ts, A_WIDTH), row),
                  pl.BlockSpec((ts, A_WIDTH), row), pl.BlockSpec((ts, A_HEADS * CHUNK), row),
                  pl.BlockSpec((ts // CHUNK * SUBLANES, A_WIDTH), row), pl.BlockSpec((ts, A_WIDTH), row),
                  pl.BlockSpec((1, A_DV), lambda b, i: (0, 0))],
        out_specs=pl.BlockSpec((ts, A_WIDTH), row),
        out_shape=jax.ShapeDtypeStruct((t, A_WIDTH), BF16),
        scratch_shapes=[pltpu.VMEM((A_HEADS, A_DK, A_DV), F32)],
        compiler_params=_cparams(("parallel", "arbitrary")),
        name="delta_rec",
    )(u, wq, kd, at, egl, z, out_gain.reshape(1, A_DV).astype(F32))


def _gmlp_kernel(u_ref, v_ref, gain_ref, w_ref, bias_ref, o_ref, *, nb):
    ri = lax.broadcasted_iota(jnp.int32, (GMLP_CHUNK, GMLP_CHUNK), 0)
    ci = lax.broadcasted_iota(jnp.int32, (GMLP_CHUNK, GMLP_CHUNK), 1)
    tril = ri >= ci
    gain = gain_ref[...]
    for blk in range(nb):
        rows = slice(blk * GMLP_CHUNK, (blk + 1) * GMLP_CHUNK)
        u = jax.nn.gelu(u_ref[rows, :].astype(F32))
        v = jax.nn.gelu(v_ref[rows, :].astype(F32))
        vn = (v * lax.rsqrt(jnp.mean(v * v, axis=-1, keepdims=True) + EPS) * gain).astype(BF16)
        for g in range(GMLP_GROUPS):
            cols = slice(g * GMLP_GDIM, (g + 1) * GMLP_GDIM)
            w = jnp.where(tril, w_ref[g], 0.0).astype(BF16)
            mixed = jnp.dot(w, vn[:, cols], preferred_element_type=F32) + bias_ref[:, cols]
            o_ref[rows, cols] = (u[:, cols] * mixed).astype(o_ref.dtype)


def _gmlp(uv, norm_gain, w_spatial, b_spatial, nb=4):
    t = uv.shape[0]
    tm = nb * GMLP_CHUNK
    bias = jnp.repeat(b_spatial.T.astype(F32), GMLP_GDIM, axis=1)
    return pl.pallas_call(
        functools.partial(_gmlp_kernel, nb=nb),
        grid=(t // tm,),
        in_specs=[pl.BlockSpec((tm, GMLP_WIDTH), lambda i: (i, 0)),
                  pl.BlockSpec((tm, GMLP_WIDTH), lambda i: (i, 1)),
                  pl.BlockSpec((1, GMLP_WIDTH), lambda i: (0, 0)),
                  pl.BlockSpec((GMLP_GROUPS, GMLP_CHUNK, GMLP_CHUNK), lambda i: (0, 0, 0)),
                  pl.BlockSpec((GMLP_CHUNK, GMLP_WIDTH), lambda i: (0, 0))],
        out_specs=pl.BlockSpec((tm, GMLP_WIDTH), lambda i: (i, 0)),
        out_shape=jax.ShapeDtypeStruct((t, GMLP_WIDTH), BF16),
        compiler_params=_cparams(("parallel",)),
        name="gmlp",
    )(uv, uv, norm_gain.reshape(1, GMLP_WIDTH).astype(F32), w_spatial.astype(F32), bias)


def _band_kernel(q_ref, kp_ref, kc_ref, vp_ref, vc_ref, qg_ref, kg_ref, bias_ref, o_ref, *, tq):
    i = pl.program_id(2)

    def norm(x, gain):
        x = x.astype(F32)
        return x * lax.rsqrt(jnp.mean(x * x, axis=-1, keepdims=True) + EPS) * gain

    qn = norm(q_ref[...], qg_ref[...]).astype(BF16)
    kcat = jnp.concatenate([norm(kp_ref[...], kg_ref[...]), norm(kc_ref[...], kg_ref[...])], axis=0).astype(BF16)
    vcat = jnp.concatenate([vp_ref[...], vc_ref[...]], axis=0).astype(BF16)
    bias = bias_ref[0]
    pad = LEFT_CHUNKS * CHUNK
    kpos = lax.broadcasted_iota(jnp.int32, (CHUNK, BAND), 1)
    chunks = range(tq // CHUNK)
    lo = [c * CHUNK + (tq - pad) for c in chunks]
    s = [_nt_dot(qn[c * CHUNK:(c + 1) * CHUNK], kcat[lo[c]:lo[c] + BAND]) for c in chunks]
    p = []
    for c in chunks:
        sc = s[c] * (C_DH ** -0.5) + bias
        valid = (i > 0) | (kpos + c * CHUNK >= pad)
        sc = jnp.where(valid, sc, NEG_BIG)
        e = jnp.exp(sc - jnp.max(sc, axis=-1, keepdims=True))
        p.append((e / jnp.sum(e, axis=-1, keepdims=True)).astype(BF16))
    o = [jnp.dot(p[c], vcat[lo[c]:lo[c] + BAND], preferred_element_type=F32) for c in chunks]
    for c in chunks:
        o_ref[c * CHUNK:(c + 1) * CHUNK, :] = o[c].astype(o_ref.dtype)


def _band_bias(rel_bias):
    diag = np.arange(-(CHUNK - 1), BAND)
    idx = np.clip(LEFT_CHUNKS * CHUNK - diag, -MAX_REL, MAX_REL) + MAX_REL
    vec = rel_bias.astype(F32)[:, idx]
    return jnp.stack([vec[:, CHUNK - 1 - i:CHUNK - 1 - i + BAND] for i in range(CHUNK)], axis=1)


def _band_attention(qkv, q_gain, k_gain, rel_bias, batch, seq):
    t = qkv.shape[0]
    tq = LEFT_CHUNKS * CHUNK
    n_s = seq // tq
    bias = _band_bias(rel_bias)
    cur = lambda off: pl.BlockSpec((tq, C_DH), lambda b, h, i: (b * n_s + i, off * C_HEADS + h))
    prv = lambda off: pl.BlockSpec((tq, C_DH), lambda b, h, i: (b * n_s + jnp.maximum(i - 1, 0), off * C_HEADS + h))
    vec = pl.BlockSpec((1, C_DH), lambda b, h, i: (0, 0))
    return pl.pallas_call(
        functools.partial(_band_kernel, tq=tq),
        grid=(batch, C_HEADS, n_s),
        in_specs=[cur(0), prv(1), cur(1), prv(2), cur(2), vec, vec,
                  pl.BlockSpec((1, CHUNK, BAND), lambda b, h, i: (h, 0, 0))],
        out_specs=pl.BlockSpec((tq, C_DH), lambda b, h, i: (b * n_s + i, h)),
        out_shape=jax.ShapeDtypeStruct((t, C_WIDTH), BF16),
        compiler_params=_cparams(("parallel", "parallel", "parallel")),
        name="band_attention",
    )(qkv, qkv, qkv, qkv, qkv, q_gain.reshape(1, C_DH).astype(F32), k_gain.reshape(1, C_DH).astype(F32), bias)


def _merge_kernel(ya_ref, yb_ref, yc_ref, pa_ref, pb_ref, pc_ref, ga_ref, gb_ref, gc_ref, o_ref,
                  wa_ref, wb_ref, wc_ref):
    @pl.when(pl.program_id(1) == 0)
    def _():
        for p_ref, w_ref in ((pa_ref, wa_ref), (pb_ref, wb_ref), (pc_ref, wc_ref)):
            w_ref[...] = p_ref[...].astype(w_ref.dtype)

    def branch(y_ref, w_ref, g_ref):
        gate = jax.nn.sigmoid(g_ref[...].astype(F32))
        return gate * jnp.dot(y_ref[...], w_ref[...], preferred_element_type=F32)

    merged = branch(ya_ref, wa_ref, ga_ref) + branch(yb_ref, wb_ref, gb_ref) + branch(yc_ref, wc_ref, gc_ref)
    o_ref[...] = merged.astype(o_ref.dtype)


def _merge(ya, yb, yc, pa, pb, pc, layer, gate, tm=1024, tn=512):
    t, k = ya.shape
    d = pa.shape[2]
    tm, tn = _pick(t, tm), _pick(d, tn)
    nd = d // tn
    ysp = pl.BlockSpec((tm, k), lambda j, i: (i, 0))
    psp = pl.BlockSpec((None, k, tn), lambda j, i: (layer, 0, j))
    gsp = lambda br: pl.BlockSpec((tm, tn), lambda j, i: (i, br * nd + j))
    return pl.pallas_call(
        _merge_kernel,
        grid=(nd, t // tm),
        in_specs=[ysp, ysp, ysp, psp, psp, psp, gsp(0), gsp(1), gsp(2)],
        out_specs=pl.BlockSpec((tm, tn), lambda j, i: (i, j)),
        out_shape=jax.ShapeDtypeStruct((t, d), BF16),
        scratch_shapes=[pltpu.VMEM((k, tn), BF16)] * 3,
        compiler_params=_cparams(("parallel", "arbitrary")),
        name="merge",
    )(ya, yb, yc, pa, pb, pc, gate, gate, gate)


def _top_values(x, k, out_ref, want_rank=False):
    cur = x
    rank = jnp.full(x.shape, float(k), F32) if want_rank else None
    for r in range(k):
        m = jnp.max(cur, axis=0, keepdims=True)
        out_ref[r:r + 1, :] = m
        hit = cur == m
        if want_rank:
            rank = jnp.where(hit, float(r), rank)
        if r + 1 < k:
            cur = jnp.where(hit, NEG_BIG, cur)
    return rank


def _peer_select_kernel(q_ref, keys_ref, cnt_ref, f1_ref, r2_ref, e2_ref, a_ref, b_ref, c_ref, *, tt):
    k = PEER_TOPK

    def head(h, carry):
        qh = q_ref[:, pl.ds(pl.multiple_of(h * PEER_QDIM, PEER_QDIM), PEER_QDIM)]
        s1 = _nt_dot(keys_ref[h, 0], qh[:, :PEER_QHALF], precision=HIGHEST)
        s2 = _nt_dot(keys_ref[h, 1], qh[:, PEER_QHALF:], precision=HIGHEST)
        _top_values(s1, k, a_ref)
        rank2 = _top_values(s2, k, b_ref, want_rank=True)
        av = a_ref[...]
        bv = b_ref[...]
        half = k // 2
        cand = jnp.concatenate([av[0:1] + bv] + [av[r:r + 1] + bv[:half] for r in range(1, half)]
                               + [av[half:] + bv[0:1]], axis=0)
        _top_values(cand, k, c_ref)
        cv = c_ref[...]
        z = jnp.sum(jnp.exp(cv - cv[0:1, :]), axis=0, keepdims=True)
        tau = cv[k - 1:k, :]
        cnt = jnp.zeros(s1.shape, F32)
        for c in range(k):
            ok = av + bv[c:c + 1] >= tau
            thr = jnp.min(jnp.where(ok, av, -NEG_BIG), axis=0, keepdims=True)
            cnt = cnt + jnp.where(s1 >= thr, 1.0, 0.0)
        cnt_ref[:, pl.ds(h, 1), :] = cnt[:, None, :]
        f1_ref[:, pl.ds(h, 1), :] = (jnp.exp(s1 - av[0:1]) / z)[:, None, :]
        rank2 = rank2.astype(r2_ref.dtype)
        e2 = jnp.exp(s2 - bv[0:1]).astype(e2_ref.dtype)
        rb = 2 * SUBLANES
        for g in range(PEER_NKEYS // rb):
            for tl in range(tt // LANES):
                r2_ref[h * (PEER_NKEYS // rb) + g, tl] = rank2[g * rb:(g + 1) * rb, tl * LANES:(tl + 1) * LANES]
                e2_ref[h * (PEER_NKEYS // rb) + g, tl] = e2[g * rb:(g + 1) * rb, tl * LANES:(tl + 1) * LANES]
        return carry

    lax.fori_loop(0, PEER_HEADS, head, 0)


def _peer_select(q, keys, tt=256):
    t = q.shape[0]
    tt = _pick(t, tt)
    shape = (PEER_NKEYS, PEER_HEADS, t)
    bspec = pl.BlockSpec((PEER_NKEYS, PEER_HEADS, tt), lambda i: (0, 0, i))
    n_slab = PEER_HEADS * PEER_NKEYS // (2 * SUBLANES)
    flat = (n_slab, t // LANES, 2 * SUBLANES, LANES)
    flat_spec = pl.BlockSpec((n_slab, tt // LANES, 2 * SUBLANES, LANES), lambda i: (0, i, 0, 0))
    return pl.pallas_call(
        functools.partial(_peer_select_kernel, tt=tt),
        grid=(t // tt,),
        in_specs=[pl.BlockSpec((tt, PEER_HEADS * PEER_QDIM), lambda i: (i, 0)),
                  pl.BlockSpec((PEER_HEADS, 2, PEER_NKEYS, PEER_QHALF), lambda i: (0, 0, 0, 0))],
        out_specs=[bspec, bspec, flat_spec, flat_spec],
        out_shape=[jax.ShapeDtypeStruct(shape, F32), jax.ShapeDtypeStruct(shape, F32),
                   jax.ShapeDtypeStruct(flat, BF16), jax.ShapeDtypeStruct(flat, BF16)],
        scratch_shapes=[pltpu.VMEM((PEER_TOPK, tt), F32), pltpu.VMEM((PEER_TOPK, tt), F32),
                        pltpu.VMEM((PEER_TOPK, tt), F32)],
        compiler_params=_cparams(("parallel",)),
        name="peer_select",
    )(q, keys.astype(F32))


def _peer_dense_kernel(hn_ref, u_ref, vt_ref, cnt_ref, f1_ref, r2_ref, e2_ref, x_ref, o_ref,
                       acc_ref, ht0_ref, ht1_ref, g0_ref, g1_ref, *, tt, ec, nc, d):
    s = pl.program_id(0)
    c_out = lax.rem(jnp.maximum(s - 2, 0), nc)

    @pl.when(s == 0)
    def _():
        for ref in (ht0_ref, ht1_ref, g0_ref, g1_ref):
            ref[...] = jnp.zeros_like(ref)

    @pl.when(c_out == 0)
    def _():
        acc_ref[...] = jnp.zeros_like(acc_ref)

    nk = PEER_NKEYS

    def stages(ht_w, ht_r, g_w, g_r):
        halves = [slice(0, tt // 2), slice(tt // 2, tt)]

        def stage_a(hs):
            ht_w[:, hs] = _nt_dot(u_ref[...], hn_ref[hs, :]).astype(ht_w.dtype)

        def stage_c(hs):
            acc_ref[:, hs] += jnp.dot(vt_ref[...], g_r[:, hs], preferred_element_type=F32)

        mxu_pieces = [functools.partial(stage_a, halves[0]), functools.partial(stage_a, halves[1]),
                      functools.partial(stage_c, halves[0]), functools.partial(stage_c, halves[1])]
        rb = 2 * SUBLANES
        tiles = [(ii, tg) for ii in range(ec // nk) for tg in range(tt // LANES)]
        per_piece = len(tiles) // len(mxu_pieces)
        for t_idx, (ii, tg) in enumerate(tiles):
            if t_idx % per_piece == 0:
                mxu_pieces[t_idx // per_piece]()
            if True:
                lanes = slice(tg * LANES, (tg + 1) * LANES)
                bcast = lambda ref, h: jnp.broadcast_to(ref[ii, h:h + 1, lanes], (rb, LANES)).astype(BF16)
                cnt = [bcast(cnt_ref, h) for h in range(PEER_HEADS)]
                f1 = [bcast(f1_ref, h) for h in range(PEER_HEADS)]
                for j0 in range(0, nk, rb):
                    wsel = jnp.zeros((rb, LANES), BF16)
                    for h in range(PEER_HEADS):
                        slab = (h * nk + j0) // rb
                        picked = jnp.clip(cnt[h] - r2_ref[slab, tg], 0, 1)
                        wsel = wsel + picked * (f1[h] * e2_ref[slab, tg])
                    rows = slice(ii * nk + j0, ii * nk + j0 + rb)
                    g_w[rows, lanes] = wsel * jax.nn.gelu(ht_r[rows, lanes])

    parity = lax.rem(s, 2)
    pl.when(parity == 0)(functools.partial(stages, ht0_ref, ht1_ref, g1_ref, g0_ref))
    pl.when(parity == 1)(functools.partial(stages, ht1_ref, ht0_ref, g0_ref, g1_ref))

    @pl.when((c_out == nc - 1) & (s >= 2))
    def _():
        step = 512
        for d0 in range(0, d, step):
            o_ref[:, d0:d0 + step] = x_ref[:, d0:d0 + step] + acc_ref[d0:d0 + step, :].T


def _peer_dense(hn, u_tab, vt_tab, layer, cntr, f1r, r2, e2, x, tt=512, ec=1024):
    t, d = hn.shape
    e = u_tab.shape[1]
    tt, ec = _pick(t, tt), _pick(e, ec)
    ni = ec // PEER_NKEYS
    nc = e // ec
    n_steps = (t // tt) * nc
    n_slab = r2.shape[0]
    tile = lambda s, lag: jnp.clip(s - lag, 0, n_steps - 1) // nc
    chunk = lambda s, lag: jnp.clip(s - lag, 0, n_steps - 1) % nc
    once = dict(pipeline_mode=pl.Buffered(1))
    return pl.pallas_call(
        functools.partial(_peer_dense_kernel, tt=tt, ec=ec, nc=nc, d=d),
        grid=(n_steps + 2,),
        in_specs=[pl.BlockSpec((tt, d), lambda s: (tile(s, 0), 0), **once),
                  pl.BlockSpec((None, ec, d), lambda s: (layer, chunk(s, 0), 0)),
                  pl.BlockSpec((None, d, ec), lambda s: (layer, 0, chunk(s, 2))),
                  pl.BlockSpec((ni, PEER_HEADS, tt), lambda s: (chunk(s, 1), 0, tile(s, 1))),
                  pl.BlockSpec((ni, PEER_HEADS, tt), lambda s: (chunk(s, 1), 0, tile(s, 1))),
                  pl.BlockSpec((n_slab, tt // LANES, 2 * SUBLANES, LANES), lambda s: (0, tile(s, 1), 0, 0)),
                  pl.BlockSpec((n_slab, tt // LANES, 2 * SUBLANES, LANES), lambda s: (0, tile(s, 1), 0, 0)),
                  pl.BlockSpec((tt, d), lambda s: (tile(s, 2), 0), **once)],
        out_specs=pl.BlockSpec((tt, d), lambda s: (tile(s, 2), 0)),
        out_shape=jax.ShapeDtypeStruct((t, d), F32),
        scratch_shapes=[pltpu.VMEM((d, tt), F32)] + [pltpu.VMEM((ec, tt), BF16)] * 4,
        compiler_params=_cparams(("arbitrary",)),
        name="peer_dense",
    )(hn, u_tab, vt_tab, cntr, f1r, r2, e2, x)


def _layer(x, batch, seq, layer, w_in, conv_w, a_log, dt_bias, a_out_gain, gmlp_norm, w_spatial, b_spatial,
           c_q_gain, c_k_gain, rel_bias, p_a, p_b, p_c, w_out, norm_mix, norm_ffn,
           peer_wq, peer_keys, peer_u, peer_vt):
    o_qkv_a = 0
    o_z = o_qkv_a + 3 * A_WIDTH
    o_beta = o_z + A_WIDTH
    o_uv = o_beta + 2 * A_HEADS
    o_qkv_c = o_uv + 2 * GMLP_WIDTH
    o_gate = o_qkv_c + 3 * C_WIDTH
    d_model = x.shape[1]

    h = _rmsnorm(x, norm_mix)
    proj = lambda col0, n, dtype, name: _matmul(h, w_in, layer, col0, n, dtype, w_is_nk=True, name=name)
    qkv_a = proj(o_qkv_a, 3 * A_WIDTH, BF16, "proj_qkv_a")
    z_a = proj(o_z, A_WIDTH, BF16, "proj_z_a")
    ba = proj(o_beta, LANES, F32, "proj_beta_alpha")
    uv_b = proj(o_uv, 2 * GMLP_WIDTH, BF16, "proj_uv_b")
    qkv_c = proj(o_qkv_c, 3 * C_WIDTH, BF16, "proj_qkv_c")
    gate = proj(o_gate, N_BRANCH * d_model, BF16, "proj_gate")

    qkv_prep = _conv_prep(qkv_a, conv_w.astype(F32), batch, seq)
    gcum, beta, dec = _gates(ba, a_log, dt_bias)
    u, wq, kd, at, egl = _delta_solve(qkv_prep, gcum, beta, dec)
    y_a = _delta_rec(u, wq, kd, at, egl, z_a, a_out_gain, batch, seq)
    y_b = _gmlp(uv_b, gmlp_norm, w_spatial, b_spatial)
    y_c = _band_attention(qkv_c, c_q_gain, c_k_gain, rel_bias, batch, seq)

    merged = _merge(y_a, y_b, y_c, p_a, p_b, p_c, layer, gate)
    x = _matmul(merged, w_out, layer, 0, d_model, F32, residual=x, name="out_proj")

    hn = _rmsnorm(x, norm_ffn)
    q = _matmul(hn, peer_wq, layer, 0, peer_wq.shape[2], F32, name="peer_query")
    cnt, f1, r2, e2 = _peer_select(q, peer_keys)
    return _peer_dense(hn, peer_u, peer_vt, layer, cnt, f1, r2, e2, x)


def kernel(x, w_in, conv_w, a_log, dt_bias, a_out_gain, gmlp_norm, w_spatial, b_spatial, c_q_gain, c_k_gain,
           rel_bias, p_a, p_b, p_c, w_out, norm_mix, norm_ffn, peer_wq, peer_keys, peer_u, peer_v):
    batch, seq, d_model = x.shape
    xt = x.reshape(batch * seq, d_model)
    w_in_t = jnp.swapaxes(w_in, 1, 2)
    peer_u16 = peer_u.astype(BF16)
    peer_vt16 = jnp.swapaxes(peer_v.astype(BF16), 1, 2)
    for l in range(w_in.shape[0]):
        xt = _layer(xt, batch, seq, l, w_in_t, conv_w[l], a_log[l], dt_bias[l], a_out_gain[l], gmlp_norm[l],
                    w_spatial[l], b_spatial[l], c_q_gain[l], c_k_gain[l], rel_bias[l], p_a, p_b, p_c,
                    w_out, norm_mix[l], norm_ffn[l], peer_wq, peer_keys[l], peer_u16, peer_vt16)
    return xt.reshape(batch, seq, d_model)
```

```python
import functools

import jax
import jax.numpy as jnp
import numpy as np
from jax import lax
from jax.experimental import pallas as pl
from jax.experimental.pallas import tpu as pltpu

F32 = jnp.float32
BF16 = jnp.bfloat16
HIGHEST = lax.Precision.HIGHEST

CHUNK = 64
EPS = 1e-6
A_HEADS = 8
A_DK = 128
A_DV = 128
A_WIDTH = A_HEADS * A_DV
CONV_W = 4
GMLP_CHUNK = 128
GMLP_GROUPS = 8
GMLP_GDIM = 128
GMLP_WIDTH = GMLP_GROUPS * GMLP_GDIM
C_HEADS = 8
C_DH = 128
C_WIDTH = C_HEADS * C_DH
LEFT_CHUNKS = 8
BAND = (LEFT_CHUNKS + 1) * CHUNK
MAX_REL = 128
N_BRANCH = 3
PEER_HEADS = 8
PEER_NKEYS = 128
PEER_TOPK = 16
PEER_QDIM = 256
PEER_QHALF = PEER_QDIM // 2

LANES = 128
SUBLANES = 8
NEG_BIG = -1e30
VMEM_LIMIT = 56 * 1024 * 1024
CHUNK_SHIFT = CHUNK.bit_length() - 1


def _cparams(sem, vmem_limit=VMEM_LIMIT, flags=None):
    return pltpu.CompilerParams(dimension_semantics=sem, vmem_limit_bytes=vmem_limit, flags=flags)


def _nt_dot(a, b, precision=None):
    return lax.dot_general(a, b, (((1,), (1,)), ((), ())), precision=precision,
                           preferred_element_type=F32)


def _tn_dot(a, b, precision=None):
    return lax.dot_general(a, b, (((0,), (0,)), ((), ())), precision=precision,
                           preferred_element_type=F32)


def _bdot(a, b):
    return jnp.dot(a.astype(BF16), b.astype(BF16), preferred_element_type=F32)


def _pick(n, pref):
    t = min(pref, n)
    while n % t:
        t -= LANES if t > LANES else 8
    return t


def _rmsnorm_kernel(x_ref, g_ref, o_ref):
    x = x_ref[...]
    ms = jnp.mean(x * x, axis=-1, keepdims=True)
    o_ref[...] = (x * lax.rsqrt(ms + EPS) * g_ref[...]).astype(o_ref.dtype)


def _rmsnorm(x, gain, out_dtype=BF16, tm=512):
    m, d = x.shape
    tm = _pick(m, tm)
    return pl.pallas_call(
        _rmsnorm_kernel,
        grid=(m // tm,),
        in_specs=[pl.BlockSpec((tm, d), lambda i: (i, 0)), pl.BlockSpec((1, d), lambda i: (0, 0))],
        out_specs=pl.BlockSpec((tm, d), lambda i: (i, 0)),
        out_shape=jax.ShapeDtypeStruct((m, d), out_dtype),
        compiler_params=_cparams(("parallel",)),
        name="rmsnorm",
    )(x, gain.reshape(1, d))


def _mm_w32_kernel(*refs, shift, has_res, w_is_nk):
    refs = list(refs)
    a_ref, w_ref = refs[:2]
    tail_ref = refs[2] if shift else None
    rest = refs[3:] if shift else refs[2:]
    r_ref = rest[0] if has_res else None
    o_ref, wb_ref = rest[-2:]

    @pl.when(pl.program_id(1) == 0)
    def _():
        w = w_ref[...]
        if shift:
            w = jnp.concatenate([w[shift:], tail_ref[:shift]], axis=0)
        wb_ref[...] = w.astype(wb_ref.dtype)

    if w_is_nk:
        acc = _nt_dot(a_ref[...], wb_ref[...])
    else:
        acc = jnp.dot(a_ref[...], wb_ref[...], preferred_element_type=F32)
    if has_res:
        acc = r_ref[...] + acc
    o_ref[...] = acc.astype(o_ref.dtype)


def _matmul(a, w, layer, col0, n, out_dtype, residual=None, w_is_nk=False, tm=1024, tn=1024, name="matmul"):
    m, k = a.shape
    tm, tn = _pick(m, tm), _pick(n, tn)
    shift = col0 % LANES
    base = col0 - shift
    assert base % tn == 0 and shift % SUBLANES == 0 and (w_is_nk or not shift), (col0, tn)
    in_specs = [pl.BlockSpec((tm, k), lambda j, i: (i, 0))]
    if w_is_nk:
        in_specs.append(pl.BlockSpec((None, tn, k), lambda j, i: (layer, base // tn + j, 0)))
        wb_shape = (tn, k)
    else:
        in_specs.append(pl.BlockSpec((None, k, tn), lambda j, i: (layer, 0, base // tn + j)))
        wb_shape = (k, tn)
    args = [a, w]
    if shift:
        in_specs.append(pl.BlockSpec((None, LANES, k), lambda j, i: (layer, (base + (j + 1) * tn) // LANES, 0)))
        args.append(w)
    if residual is not None:
        in_specs.append(pl.BlockSpec((tm, tn), lambda j, i: (i, j)))
        args.append(residual)
    return pl.pallas_call(
        functools.partial(_mm_w32_kernel, shift=shift, has_res=residual is not None, w_is_nk=w_is_nk),
        grid=(n // tn, m // tm),
        in_specs=in_specs,
        out_specs=pl.BlockSpec((tm, tn), lambda j, i: (i, j)),
        out_shape=jax.ShapeDtypeStruct((m, n), out_dtype),
        scratch_shapes=[pltpu.VMEM(wb_shape, BF16)],
        compiler_params=_cparams(("parallel", "arbitrary")),
        name=name,
    )(*args)


def _conv_prep_kernel(cur_ref, prev_ref, w_ref, o_ref, ext_ref, *, ts, tc, prev_rows, n_qk_tiles, n_q_tiles):
    i = pl.program_id(1)
    c = pl.program_id(2)
    prev = prev_ref[...].astype(F32)
    prev = jnp.where(i == 0, 0.0, prev)
    cur = cur_ref[...].astype(F32)
    ext_ref[0:prev_rows, :] = prev
    ext_ref[prev_rows:prev_rows + ts, :] = cur
    w = w_ref[...]
    acc = cur * w[CONV_W - 1:CONV_W, :]
    for d in range(1, CONV_W):
        acc = acc + ext_ref[prev_rows - d:prev_rows - d + ts, :] * w[CONV_W - 1 - d:CONV_W - d, :]
    y = acc * jax.nn.sigmoid(acc)
    parts = []
    for h in range(tc // A_DK):
        yh = y[:, h * A_DK:(h + 1) * A_DK]
        parts.append(yh * lax.rsqrt(jnp.sum(yh * yh, axis=-1, keepdims=True) + EPS))
    yn = jnp.concatenate(parts, axis=-1)
    scale = jnp.where(c < n_q_tiles, A_DK ** -0.5, 1.0).astype(F32)
    o_ref[...] = jnp.where(c < n_qk_tiles, yn * scale, y).astype(o_ref.dtype)


def _conv_prep(qkv, conv_w, batch, seq, ts=512, tc=512):
    t, ch = qkv.shape
    ts = _pick(seq, ts)
    prev_rows = 16
    n_s = seq // ts
    kern = functools.partial(_conv_prep_kernel, ts=ts, tc=tc, prev_rows=prev_rows,
                             n_qk_tiles=2 * A_WIDTH // tc, n_q_tiles=A_WIDTH // tc)
    rpb = ts // prev_rows
    return pl.pallas_call(
        kern,
        grid=(batch, n_s, ch // tc),
        in_specs=[
            pl.BlockSpec((ts, tc), lambda b, i, c: (b * n_s + i, c)),
            pl.BlockSpec((prev_rows, tc), lambda b, i, c: (jnp.maximum((b * n_s + i) * rpb - 1, 0), c)),
            pl.BlockSpec((CONV_W, tc), lambda b, i, c: (0, c)),
        ],
        out_specs=pl.BlockSpec((ts, tc), lambda b, i, c: (b * n_s + i, c)),
        out_shape=jax.ShapeDtypeStruct((t, ch), BF16),
        scratch_shapes=[pltpu.VMEM((prev_rows + ts, tc), F32)],
        compiler_params=_cparams(("parallel", "parallel", "parallel")),
        name="conv_prep",
    )(qkv, qkv, conv_w)


def _split3(x):
    hi = x.astype(BF16)
    r = x - hi.astype(F32)
    mid = r.astype(BF16)
    return hi, mid, (r - mid.astype(F32)).astype(BF16)


def _gates_kernel(ba_ref, alog_ref, dtb_ref, g_ref, b_ref, dec_ref, *, ts):
    def hdot(a, b):
        if a.dtype == BF16:
            return sum(jnp.dot(a, p, preferred_element_type=F32) for p in _split3(b))
        return sum(jnp.dot(p, b, preferred_element_type=F32) for p in _split3(a))

    ba = ba_ref[...]
    lane = lax.broadcasted_iota(jnp.int32, (ts, LANES), 1)
    head_lane = lane < A_HEADS
    beta = jnp.where(head_lane, jax.nn.sigmoid(ba), 0.0)
    alpha = pltpu.roll(ba, LANES - A_HEADS, axis=1)
    g = -jnp.exp(alog_ref[...]) * jax.nn.softplus(alpha + dtb_ref[...])
    g = jnp.where(head_lane, g, 0.0)
    r = lax.broadcasted_iota(jnp.int32, (ts, ts), 0)
    c = lax.broadcasted_iota(jnp.int32, (ts, ts), 1)
    same_chunk = (r >> CHUNK_SHIFT) == (c >> CHUNK_SHIFT)
    tri = jnp.where((c <= r) & same_chunk, 1.0, 0.0).astype(BF16)
    gcum = hdot(tri, g)
    er = lax.broadcasted_iota(jnp.int32, (LANES, A_WIDTH), 0)
    ec = lax.broadcasted_iota(jnp.int32, (LANES, A_WIDTH), 1)
    spread = jnp.where(er == (ec >> (A_DV.bit_length() - 1)), 1.0, 0.0).astype(BF16)
    g_ref[...] = hdot(gcum, spread)
    b_ref[...] = hdot(beta, spread)
    wd = A_HEADS * CHUNK
    er = lax.broadcasted_iota(jnp.int32, (LANES, wd), 0)
    ec = lax.broadcasted_iota(jnp.int32, (LANES, wd), 1)
    gi = hdot(gcum, jnp.where(er == (ec >> CHUNK_SHIFT), 1.0, 0.0).astype(BF16))
    ipos = lax.broadcasted_iota(jnp.int32, (ts, wd), 0) & (CHUNK - 1)
    jpos = lax.broadcasted_iota(jnp.int32, (ts, wd), 1) & (CHUNK - 1)
    blk = jnp.where(same_chunk, 1.0, 0.0).astype(BF16)
    gj = hdot(blk, jnp.where(ipos == jpos, gi, 0.0))
    dec_ref[...] = jnp.exp(jnp.where(ipos >= jpos, gi - gj, NEG_BIG))


def _gates(ba, a_log, dt_bias, ts=256):
    t = ba.shape[0]
    ts = _pick(t, ts)
    pad = lambda v: jnp.pad(v.astype(F32), (0, LANES - A_HEADS)).reshape(1, LANES)
    out = jax.ShapeDtypeStruct((t, A_WIDTH), F32)
    wd = A_HEADS * CHUNK
    return pl.pallas_call(
        functools.partial(_gates_kernel, ts=ts),
        grid=(t // ts,),
        in_specs=[pl.BlockSpec((ts, LANES), lambda i: (i, 0)),
                  pl.BlockSpec((1, LANES), lambda i: (0, 0)), pl.BlockSpec((1, LANES), lambda i: (0, 0))],
        out_specs=[pl.BlockSpec((ts, A_WIDTH), lambda i: (i, 0)), pl.BlockSpec((ts, A_WIDTH), lambda i: (i, 0)),
                   pl.BlockSpec((ts, wd), lambda i: (i, 0))],
        out_shape=[out, out, jax.ShapeDtypeStruct((t, wd), F32)],
        compiler_params=_cparams(("parallel",)),
        name="gates",
    )(ba, pad(a_log), pad(dt_bias))


def _delta_solve_kernel(q_ref, k_ref, v_ref, g_ref, b_ref, dec_ref, u_ref, wq_ref, kd_ref, at_ref, egl_ref,
                        *, ts, hp, group=8):
    ri = lax.broadcasted_iota(jnp.int32, (CHUNK, CHUNK), 0)
    ci = lax.broadcasted_iota(jnp.int32, (CHUNK, CHUNK), 1)
    strict = ri > ci
    ident = jnp.where(ri == ci, 1.0, 0.0).astype(F32)

    def load(n, h):
        rows = slice(n * CHUNK, (n + 1) * CHUNK)
        cols = slice(h * A_DK, (h + 1) * A_DK)
        c = dict(n=n, rows=rows, cols=cols, dcols=slice(h * CHUNK, (h + 1) * CHUNK))
        c["q"] = q_ref[rows, cols].astype(F32)
        c["k"] = k_ref[rows, cols].astype(F32)
        c["gb"] = g_ref[rows, cols]
        c["bt"] = b_ref[rows, cols]
        c["kb"] = c["k"] * c["bt"]
        c["eg"] = jnp.exp(c["gb"])
        return c

    chains = [(n, h) for n in range(ts // CHUNK) for h in range(hp)]
    for g0 in range(0, len(chains), group):
        cs = [load(n, h) for n, h in chains[g0:g0 + group]]
        for c in cs:
            qk_kk = _nt_dot(jnp.concatenate([c["q"], c["kb"]], axis=0).astype(BF16), c["k"].astype(BF16))
            dec = dec_ref[c["rows"], c["dcols"]]
            c["attn"] = qk_kk[:CHUNK] * dec
            c["a"] = jnp.where(strict, qk_kk[CHUNK:] * dec, 0.0)
        for c in cs:
            c["inv"] = ident - c["a"]
            c["p"] = _bdot(c["a"], c["a"])
        for _ in range(CHUNK_SHIFT - 2):
            for c in cs:
                y = _bdot(jnp.concatenate([c["inv"], c["p"]], axis=0), c["p"])
                c["inv"] = c["inv"] + y[:CHUNK]
                c["p"] = y[CHUNK:]
        for c in cs:
            c["inv"] = c["inv"] + _bdot(c["inv"], c["p"])
        for c in cs:
            v = v_ref[c["rows"], c["cols"]].astype(F32)
            c["uw"] = _bdot(c["inv"], jnp.concatenate([v * c["bt"], c["kb"] * c["eg"]], axis=1))
        for c in cs:
            n, rows, cols = c["n"], c["rows"], c["cols"]
            g_last = c["gb"][CHUNK - 1:CHUNK, :]
            u_ref[rows, cols] = c["uw"][:, :A_DV].astype(u_ref.dtype)
            wq_ref[2 * n * CHUNK:(2 * n + 1) * CHUNK, cols] = c["uw"][:, A_DV:].astype(wq_ref.dtype)
            wq_ref[(2 * n + 1) * CHUNK:(2 * n + 2) * CHUNK, cols] = (c["q"] * c["eg"]).astype(wq_ref.dtype)
            kd_ref[rows, cols] = (c["k"] * jnp.exp(g_last - c["gb"])).astype(kd_ref.dtype)
            at_ref[rows, c["dcols"]] = c["attn"].astype(at_ref.dtype)
            egl_ref[n * SUBLANES:(n + 1) * SUBLANES, cols] = jnp.broadcast_to(jnp.exp(g_last), (SUBLANES, A_DV))


def _delta_solve(qkv, gcum, beta, dec, ts=512, hp=4):
    t = qkv.shape[0]
    ts = _pick(t, ts)
    hw = hp * A_DK
    n_hb = A_WIDTH // hw
    spec = lambda off: pl.BlockSpec((ts, hw), lambda i, hb: (i, off * n_hb + hb))
    wide = jax.ShapeDtypeStruct((t, A_WIDTH), BF16)
    return pl.pallas_call(
        functools.partial(_delta_solve_kernel, ts=ts, hp=hp),
        grid=(t // ts, n_hb),
        in_specs=[spec(0), spec(1), spec(2), spec(0), spec(0),
                  pl.BlockSpec((ts, hp * CHUNK), lambda i, hb: (i, hb))],
        out_specs=[spec(0),
                   pl.BlockSpec((2 * ts, hw), lambda i, hb: (i, hb)),
                   spec(0),
                   pl.BlockSpec((ts, hp * CHUNK), lambda i, hb: (i, hb)),
                   pl.BlockSpec((ts // CHUNK * SUBLANES, hw), lambda i, hb: (i, hb))],
        out_shape=[wide, jax.ShapeDtypeStruct((2 * t, A_WIDTH), BF16), wide,
                   jax.ShapeDtypeStruct((t, A_HEADS * CHUNK), BF16),
                   jax.ShapeDtypeStruct((t // CHUNK * SUBLANES, A_WIDTH), F32)],
        compiler_params=_cparams(("parallel", "parallel")),
        name="delta_solve",
    )(qkv, qkv, qkv, gcum, beta, dec)


def _delta_rec_kernel(u_ref, wq_ref, kd_ref, at_ref, egl_ref, z_ref, gain_ref, o_ref, s_ref, *, ts):
    @pl.when(pl.program_id(1) == 0)
    def _():
        s_ref[...] = jnp.zeros_like(s_ref)

    gain = gain_ref[...]

    heads = range(A_HEADS)
    col = lambda h: slice(h * A_DK, (h + 1) * A_DK)
    for n in range(ts // CHUNK):
        rows = slice(n * CHUNK, (n + 1) * CHUNK)
        rows2 = slice(2 * n * CHUNK, (2 * n + 2) * CHUNK)
        s = [s_ref[h] for h in heads]
        ws = [jnp.dot(wq_ref[rows2, col(h)], s[h].astype(BF16), preferred_element_type=F32)
              for h in heads]
        v_new = [(u_ref[rows, col(h)].astype(F32) - ws[h][:CHUNK]).astype(BF16) for h in heads]
        o = [ws[h][CHUNK:] + jnp.dot(at_ref[rows, h * CHUNK:(h + 1) * CHUNK], v_new[h], preferred_element_type=F32)
             for h in heads]
        for h in heads:
            eg_last = egl_ref[n * SUBLANES:n * SUBLANES + 1, col(h)]
            s_ref[h] = s[h] * eg_last + _tn_dot(kd_ref[rows, col(h)], v_new[h])
        for h in heads:
            on = o[h] * lax.rsqrt(jnp.mean(o[h] * o[h], axis=-1, keepdims=True) + EPS) * gain
            z = z_ref[rows, col(h)].astype(F32)
            o_ref[rows, col(h)] = (on * (z * jax.nn.sigmoid(z))).astype(o_ref.dtype)


def _delta_rec(u, wq, kd, at, egl, z, out_gain, batch, seq, ts=512):
    t = u.shape[0]
    ts = _pick(seq, ts)
    n_s = seq // ts
    row = lambda b, i: (b * n_s + i, 0)
    return pl.pallas_call(
        functools.partial(_delta_rec_kernel, ts=ts),
        grid=(batch, n_s),
        in_specs=[pl.BlockSpec((ts, A_WIDTH), row), pl.BlockSpec((2 * ts, A_WIDTH), row),
                  pl.BlockSpec((ts, A_WIDTH), row), pl.BlockSpec((ts, A_HEADS * CHUNK), row),
                  pl.BlockSpec((ts // CHUNK * SUBLANES, A_WIDTH), row), pl.BlockSpec((ts, A_WIDTH), row),
                  pl.BlockSpec((1, A_DV), lambda b, i: (0, 0))],
        out_specs=pl.BlockSpec((ts, A_WIDTH), row),
        out_shape=jax.ShapeDtypeStruct((t, A_WIDTH), BF16),
        scratch_shapes=[pltpu.VMEM((A_HEADS, A_DK, A_DV), F32)],
        compiler_params=_cparams(("parallel", "arbitrary")),
        name="delta_rec",
    )(u, wq, kd, at, egl, z, out_gain.reshape(1, A_DV).astype(F32))


def _gmlp_kernel(u_ref, v_ref, gain_ref, w_ref, bias_ref, o_ref, *, nb):
    ri = lax.broadcasted_iota(jnp.int32, (GMLP_CHUNK, GMLP_CHUNK), 0)
    ci = lax.broadcasted_iota(jnp.int32, (GMLP_CHUNK, GMLP_CHUNK), 1)
    tril = ri >= ci
    gain = gain_ref[...]
    for blk in range(nb):
        rows = slice(blk * GMLP_CHUNK, (blk + 1) * GMLP_CHUNK)
        u = jax.nn.gelu(u_ref[rows, :].astype(F32))
        v = jax.nn.gelu(v_ref[rows, :].astype(F32))
        vn = (v * lax.rsqrt(jnp.mean(v * v, axis=-1, keepdims=True) + EPS) * gain).astype(BF16)
        for g in range(GMLP_GROUPS):
            cols = slice(g * GMLP_GDIM, (g + 1) * GMLP_GDIM)
            w = jnp.where(tril, w_ref[g], 0.0).astype(BF16)
            mixed = jnp.dot(w, vn[:, cols], preferred_element_type=F32) + bias_ref[:, cols]
            o_ref[rows, cols] = (u[:, cols] * mixed).astype(o_ref.dtype)


def _gmlp(uv, norm_gain, w_spatial, b_spatial, nb=4):
    t = uv.shape[0]
    tm = nb * GMLP_CHUNK
    bias = jnp.repeat(b_spatial.T.astype(F32), GMLP_GDIM, axis=1)
    return pl.pallas_call(
        functools.partial(_gmlp_kernel, nb=nb),
        grid=(t // tm,),
        in_specs=[pl.BlockSpec((tm, GMLP_WIDTH), lambda i: (i, 0)),
                  pl.BlockSpec((tm, GMLP_WIDTH), lambda i: (i, 1)),
                  pl.BlockSpec((1, GMLP_WIDTH), lambda i: (0, 0)),
                  pl.BlockSpec((GMLP_GROUPS, GMLP_CHUNK, GMLP_CHUNK), lambda i: (0, 0, 0)),
                  pl.BlockSpec((GMLP_CHUNK, GMLP_WIDTH), lambda i: (0, 0))],
        out_specs=pl.BlockSpec((tm, GMLP_WIDTH), lambda i: (i, 0)),
        out_shape=jax.ShapeDtypeStruct((t, GMLP_WIDTH), BF16),
        compiler_params=_cparams(("parallel",)),
        name="gmlp",
    )(uv, uv, norm_gain.reshape(1, GMLP_WIDTH).astype(F32), w_spatial.astype(F32), bias)


def _band_kernel(q_ref, kp_ref, kc_ref, vp_ref, vc_ref, qg_ref, kg_ref, bias_ref, o_ref, *, tq):
    i = pl.program_id(2)

    def norm(x, gain):
        x = x.astype(F32)
        return x * lax.rsqrt(jnp.mean(x * x, axis=-1, keepdims=True) + EPS) * gain

    qn = norm(q_ref[...], qg_ref[...]).astype(BF16)
    kcat = jnp.concatenate([norm(kp_ref[...], kg_ref[...]), norm(kc_ref[...], kg_ref[...])], axis=0).astype(BF16)
    vcat = jnp.concatenate([vp_ref[...], vc_ref[...]], axis=0).astype(BF16)
    bias = bias_ref[0]
    pad = LEFT_CHUNKS * CHUNK
    kpos = lax.broadcasted_iota(jnp.int32, (CHUNK, BAND), 1)
    chunks = range(tq // CHUNK)
    lo = [c * CHUNK + (tq - pad) for c in chunks]
    s = [_nt_dot(qn[c * CHUNK:(c + 1) * CHUNK], kcat[lo[c]:lo[c] + BAND]) for c in chunks]
    p = []
    for c in chunks:
        sc = s[c] * (C_DH ** -0.5) + bias
        valid = (i > 0) | (kpos + c * CHUNK >= pad)
        sc = jnp.where(valid, sc, NEG_BIG)
        e = jnp.exp(sc - jnp.max(sc, axis=-1, keepdims=True))
        p.append((e / jnp.sum(e, axis=-1, keepdims=True)).astype(BF16))
    o = [jnp.dot(p[c], vcat[lo[c]:lo[c] + BAND], preferred_element_type=F32) for c in chunks]
    for c in chunks:
        o_ref[c * CHUNK:(c + 1) * CHUNK, :] = o[c].astype(o_ref.dtype)


def _band_bias(rel_bias):
    diag = np.arange(-(CHUNK - 1), BAND)
    idx = np.clip(LEFT_CHUNKS * CHUNK - diag, -MAX_REL, MAX_REL) + MAX_REL
    vec = rel_bias.astype(F32)[:, idx]
    return jnp.stack([vec[:, CHUNK - 1 - i:CHUNK - 1 - i + BAND] for i in range(CHUNK)], axis=1)


def _band_attention(qkv, q_gain, k_gain, rel_bias, batch, seq):
    t = qkv.shape[0]
    tq = LEFT_CHUNKS * CHUNK
    n_s = seq // tq
    bias = _band_bias(rel_bias)
    cur = lambda off: pl.BlockSpec((tq, C_DH), lambda b, h, i: (b * n_s + i, off * C_HEADS + h))
    prv = lambda off: pl.BlockSpec((tq, C_DH), lambda b, h, i: (b * n_s + jnp.maximum(i - 1, 0), off * C_HEADS + h))
    vec = pl.BlockSpec((1, C_DH), lambda b, h, i: (0, 0))
    return pl.pallas_call(
        functools.partial(_band_kernel, tq=tq),
        grid=(batch, C_HEADS, n_s),
        in_specs=[cur(0), prv(1), cur(1), prv(2), cur(2), vec, vec,
                  pl.BlockSpec((1, CHUNK, BAND), lambda b, h, i: (h, 0, 0))],
        out_specs=pl.BlockSpec((tq, C_DH), lambda b, h, i: (b * n_s + i, h)),
        out_shape=jax.ShapeDtypeStruct((t, C_WIDTH), BF16),
        compiler_params=_cparams(("parallel", "parallel", "parallel")),
        name="band_attention",
    )(qkv, qkv, qkv, qkv, qkv, q_gain.reshape(1, C_DH).astype(F32), k_gain.reshape(1, C_DH).astype(F32), bias)


def _merge_kernel(ya_ref, yb_ref, yc_ref, pa_ref, pb_ref, pc_ref, ga_ref, gb_ref, gc_ref, o_ref,
                  wa_ref, wb_ref, wc_ref):
    @pl.when(pl.program_id(1) == 0)
    def _():
        for p_ref, w_ref in ((pa_ref, wa_ref), (pb_ref, wb_ref), (pc_ref, wc_ref)):
            w_ref[...] = p_ref[...].astype(w_ref.dtype)

    def branch(y_ref, w_ref, g_ref):
        gate = jax.nn.sigmoid(g_ref[...].astype(F32))
        return gate * jnp.dot(y_ref[...], w_ref[...], preferred_element_type=F32)

    merged = branch(ya_ref, wa_ref, ga_ref) + branch(yb_ref, wb_ref, gb_ref) + branch(yc_ref, wc_ref, gc_ref)
    o_ref[...] = merged.astype(o_ref.dtype)


def _merge(ya, yb, yc, pa, pb, pc, layer, gate, tm=1024, tn=512):
    t, k = ya.shape
    d = pa.shape[2]
    tm, tn = _pick(t, tm), _pick(d, tn)
    nd = d // tn
    ysp = pl.BlockSpec((tm, k), lambda j, i: (i, 0))
    psp = pl.BlockSpec((None, k, tn), lambda j, i: (layer, 0, j))
    gsp = lambda br: pl.BlockSpec((tm, tn), lambda j, i: (i, br * nd + j))
    return pl.pallas_call(
        _merge_kernel,
        grid=(nd, t // tm),
        in_specs=[ysp, ysp, ysp, psp, psp, psp, gsp(0), gsp(1), gsp(2)],
        out_specs=pl.BlockSpec((tm, tn), lambda j, i: (i, j)),
        out_shape=jax.ShapeDtypeStruct((t, d), BF16),
        scratch_shapes=[pltpu.VMEM((k, tn), BF16)] * 3,
        compiler_params=_cparams(("parallel", "arbitrary")),
        name="merge",
    )(ya, yb, yc, pa, pb, pc, gate, gate, gate)


def _top_values(x, k, out_ref, want_rank=False):
    cur = x
    rank = jnp.full(x.shape, float(k), F32) if want_rank else None
    for r in range(k):
        m = jnp.max(cur, axis=0, keepdims=True)
        out_ref[r:r + 1, :] = m
        hit = cur == m
        if want_rank:
            rank = jnp.where(hit, float(r), rank)
        if r + 1 < k:
            cur = jnp.where(hit, NEG_BIG, cur)
    return rank


def _peer_select_kernel(q_ref, keys_ref, cnt_ref, f1_ref, r2_ref, e2_ref, top_ref, *, tt):
    k = PEER_TOPK

    def head(h, sub):
        a_ref, b_ref, c_ref = top_ref.at[sub, 0], top_ref.at[sub, 1], top_ref.at[sub, 2]
        qh = q_ref[:, pl.ds(pl.multiple_of(h * PEER_QDIM, PEER_QDIM), PEER_QDIM)]
        s1 = _nt_dot(keys_ref[h, 0], qh[:, :PEER_QHALF], precision=HIGHEST)
        s2 = _nt_dot(keys_ref[h, 1], qh[:, PEER_QHALF:], precision=HIGHEST)
        _top_values(s1, k, a_ref)
        rank2 = _top_values(s2, k, b_ref, want_rank=True)
        av = a_ref[...]
        bv = b_ref[...]
        half = k // 2
        cand = jnp.concatenate([av[0:1] + bv] + [av[r:r + 1] + bv[:half] for r in range(1, half)]
                               + [av[half:] + bv[0:1]], axis=0)
        _top_values(cand, k, c_ref)
        cv = c_ref[...]
        z = jnp.sum(jnp.exp(cv - cv[0:1, :]), axis=0, keepdims=True)
        tau = cv[k - 1:k, :]
        cnt = jnp.zeros(s1.shape, F32)
        for c in range(k):
            ok = av + bv[c:c + 1] >= tau
            thr = jnp.min(jnp.where(ok, av, -NEG_BIG), axis=0, keepdims=True)
            cnt = cnt + jnp.where(s1 >= thr, 1.0, 0.0)
        cnt_ref[:, pl.ds(h, 1), :] = cnt[:, None, :]
        f1_ref[:, pl.ds(h, 1), :] = (jnp.exp(s1 - av[0:1]) / z)[:, None, :]
        rank2 = rank2.astype(r2_ref.dtype)
        e2 = jnp.exp(s2 - bv[0:1]).astype(e2_ref.dtype)
        rb = 2 * SUBLANES
        for g in range(PEER_NKEYS // rb):
            for tl in range(tt // LANES):
                r2_ref[h * (PEER_NKEYS // rb) + g, tl] = rank2[g * rb:(g + 1) * rb, tl * LANES:(tl + 1) * LANES]
                e2_ref[h * (PEER_NKEYS // rb) + g, tl] = e2[g * rb:(g + 1) * rb, tl * LANES:(tl + 1) * LANES]

    def pair(p, carry):
        head(2 * p, 0)
        head(2 * p + 1, 1)
        return carry

    lax.fori_loop(0, PEER_HEADS // 2, pair, 0)


def _peer_select(q, keys, tt=256):
    t = q.shape[0]
    tt = _pick(t, tt)
    shape = (PEER_NKEYS, PEER_HEADS, t)
    bspec = pl.BlockSpec((PEER_NKEYS, PEER_HEADS, tt), lambda i: (0, 0, i))
    n_slab = PEER_HEADS * PEER_NKEYS // (2 * SUBLANES)
    flat = (n_slab, t // LANES, 2 * SUBLANES, LANES)
    flat_spec = pl.BlockSpec((n_slab, tt // LANES, 2 * SUBLANES, LANES), lambda i: (0, i, 0, 0))
    return pl.pallas_call(
        functools.partial(_peer_select_kernel, tt=tt),
        grid=(t // tt,),
        in_specs=[pl.BlockSpec((tt, PEER_HEADS * PEER_QDIM), lambda i: (i, 0)),
                  pl.BlockSpec((PEER_HEADS, 2, PEER_NKEYS, PEER_QHALF), lambda i: (0, 0, 0, 0))],
        out_specs=[bspec, bspec, flat_spec, flat_spec],
        out_shape=[jax.ShapeDtypeStruct(shape, F32), jax.ShapeDtypeStruct(shape, F32),
                   jax.ShapeDtypeStruct(flat, BF16), jax.ShapeDtypeStruct(flat, BF16)],
        scratch_shapes=[pltpu.VMEM((2, 3, PEER_TOPK, tt), F32)],
        compiler_params=_cparams(("parallel",)),
        name="peer_select",
    )(q, keys.astype(F32))


def _peer_dense_kernel(hn_ref, u_ref, vt_ref, cnt_ref, f1_ref, r2_ref, e2_ref, x_ref, o_ref,
                       acc_ref, ht0_ref, ht1_ref, g0_ref, g1_ref, *, tt, ec, nc, d):
    s = pl.program_id(0)
    c_out = lax.rem(jnp.maximum(s - 2, 0), nc)

    @pl.when(s == 0)
    def _():
        for ref in (ht0_ref, ht1_ref, g0_ref, g1_ref):
            ref[...] = jnp.zeros_like(ref)

    @pl.when(c_out == 0)
    def _():
        acc_ref[...] = jnp.zeros_like(acc_ref)

    nk = PEER_NKEYS

    def stages(ht_w, ht_r, g_w, g_r):
        pa, pc = 128, 128

        def stage_a(r0):
            ht_w[r0:r0 + pa, :] = jnp.dot(u_ref[r0:r0 + pa, :], hn_ref[...],
                                          preferred_element_type=F32).astype(ht_w.dtype)

        def stage_c(r0):
            acc_ref[r0:r0 + pc, :] += jnp.dot(vt_ref[r0:r0 + pc, :], g_r[...], preferred_element_type=F32)

        mxu_pieces = ([(pa * d, functools.partial(stage_a, r0)) for r0 in range(0, ec, pa)]
                      + [(pc * ec, functools.partial(stage_c, r0)) for r0 in range(0, d, pc)])
        rb = 2 * SUBLANES
        tiles = [(ii, tg) for ii in range(ec // nk) for tg in range(tt // LANES)]
        total_work = sum(w for w, _ in mxu_pieces)
        emit_at, done = {}, 0
        for w, piece in mxu_pieces:
            emit_at.setdefault((done * len(tiles)) // total_work, []).append(piece)
            done += w
        for t_idx, (ii, tg) in enumerate(tiles):
            for piece in emit_at.get(t_idx, ()):
                piece()
            lanes = slice(tg * LANES, (tg + 1) * LANES)
            bcast = lambda ref, h: jnp.broadcast_to(ref[ii, h:h + 1, lanes], (rb, LANES)).astype(BF16)
            cnt = [bcast(cnt_ref, h) for h in range(PEER_HEADS)]
            f1 = [bcast(f1_ref, h) for h in range(PEER_HEADS)]
            for j0 in range(0, nk, rb):
                wsel = jnp.zeros((rb, LANES), BF16)
                for h in range(PEER_HEADS):
                    slab = (h * nk + j0) // rb
                    picked = jnp.clip(cnt[h] - r2_ref[slab, tg], 0, 1)
                    wsel = wsel + picked * (f1[h] * e2_ref[slab, tg])
                rows = slice(ii * nk + j0, ii * nk + j0 + rb)
                g_w[rows, lanes] = wsel * jax.nn.gelu(ht_r[rows, lanes])

    parity = lax.rem(s, 2)
    pl.when(parity == 0)(functools.partial(stages, ht0_ref, ht1_ref, g1_ref, g0_ref))
    pl.when(parity == 1)(functools.partial(stages, ht1_ref, ht0_ref, g0_ref, g1_ref))

    @pl.when((c_out == nc - 1) & (s >= 2))
    def _():
        step = 512
        for d0 in range(0, d, step):
            o_ref[:, d0:d0 + step] = x_ref[:, d0:d0 + step] + acc_ref[d0:d0 + step, :].T


def _peer_dense(hnt, u_tab, vt_tab, layer, cntr, f1r, r2, e2, x, tt=512, ec=1024):
    d, t = hnt.shape
    e = u_tab.shape[1]
    tt, ec = _pick(t, tt), _pick(e, ec)
    ni = ec // PEER_NKEYS
    nc = e // ec
    n_steps = (t // tt) * nc
    n_slab = r2.shape[0]
    tile = lambda s, lag: jnp.clip(s - lag, 0, n_steps - 1) // nc
    chunk = lambda s, lag: jnp.clip(s - lag, 0, n_steps - 1) % nc
    once = dict(pipeline_mode=pl.Buffered(1))
    return pl.pallas_call(
        functools.partial(_peer_dense_kernel, tt=tt, ec=ec, nc=nc, d=d),
        grid=(n_steps + 2,),
        in_specs=[pl.BlockSpec((d, tt), lambda s: (0, tile(s, 0)), **once),
                  pl.BlockSpec((None, ec, d), lambda s: (layer, chunk(s, 0), 0)),
                  pl.BlockSpec((None, d, ec), lambda s: (layer, 0, chunk(s, 2))),
                  pl.BlockSpec((ni, PEER_HEADS, tt), lambda s: (chunk(s, 1), 0, tile(s, 1))),
                  pl.BlockSpec((ni, PEER_HEADS, tt), lambda s: (chunk(s, 1), 0, tile(s, 1))),
                  pl.BlockSpec((n_slab, tt // LANES, 2 * SUBLANES, LANES), lambda s: (0, tile(s, 1), 0, 0)),
                  pl.BlockSpec((n_slab, tt // LANES, 2 * SUBLANES, LANES), lambda s: (0, tile(s, 1), 0, 0)),
                  pl.BlockSpec((tt, d), lambda s: (tile(s, 2), 0), **once)],
        out_specs=pl.BlockSpec((tt, d), lambda s: (tile(s, 2), 0)),
        out_shape=jax.ShapeDtypeStruct((t, d), F32),
        scratch_shapes=[pltpu.VMEM((d, tt), F32)] + [pltpu.VMEM((ec, tt), BF16)] * 4,
        compiler_params=_cparams(("arbitrary",)),
        name="peer_dense",
    )(hnt, u_tab, vt_tab, cntr, f1r, r2, e2, x)


def _layer(x, batch, seq, layer, w_in, conv_w, a_log, dt_bias, a_out_gain, gmlp_norm, w_spatial, b_spatial,
           c_q_gain, c_k_gain, rel_bias, p_a, p_b, p_c, w_out, norm_mix, norm_ffn,
           peer_wq, peer_keys, peer_u, peer_vt):
    o_qkv_a = 0
    o_z = o_qkv_a + 3 * A_WIDTH
    o_beta = o_z + A_WIDTH
    o_uv = o_beta + 2 * A_HEADS
    o_qkv_c = o_uv + 2 * GMLP_WIDTH
    o_gate = o_qkv_c + 3 * C_WIDTH
    d_model = x.shape[1]

    h = _rmsnorm(x, norm_mix)
    proj = lambda col0, n, dtype, name: _matmul(h, w_in, layer, col0, n, dtype, w_is_nk=True, name=name)
    qkv_a = proj(o_qkv_a, 3 * A_WIDTH, BF16, "proj_qkv_a")
    z_a = proj(o_z, A_WIDTH, BF16, "proj_z_a")
    ba = proj(o_beta, LANES, F32, "proj_beta_alpha")
    uv_b = proj(o_uv, 2 * GMLP_WIDTH, BF16, "proj_uv_b")
    qkv_c = proj(o_qkv_c, 3 * C_WIDTH, BF16, "proj_qkv_c")
    gate = proj(o_gate, N_BRANCH * d_model, BF16, "proj_gate")

    qkv_prep = _conv_prep(qkv_a, conv_w.astype(F32), batch, seq)
    gcum, beta, dec = _gates(ba, a_log, dt_bias)
    u, wq, kd, at, egl = _delta_solve(qkv_prep, gcum, beta, dec)
    y_a = _delta_rec(u, wq, kd, at, egl, z_a, a_out_gain, batch, seq)
    y_b = _gmlp(uv_b, gmlp_norm, w_spatial, b_spatial)
    y_c = _band_attention(qkv_c, c_q_gain, c_k_gain, rel_bias, batch, seq)

    merged = _merge(y_a, y_b, y_c, p_a, p_b, p_c, layer, gate)
    x = _matmul(merged, w_out, layer, 0, d_model, F32, residual=x, name="out_proj")

    hn = _rmsnorm(x, norm_ffn)
    q = _matmul(hn, peer_wq, layer, 0, peer_wq.shape[2], F32, name="peer_query")
    cnt, f1, r2, e2 = _peer_select(q, peer_keys)
    return _peer_dense(hn.T, peer_u, peer_vt, layer, cnt, f1, r2, e2, x)


def kernel(x, w_in, conv_w, a_log, dt_bias, a_out_gain, gmlp_norm, w_spatial, b_spatial, c_q_gain, c_k_gain,
           rel_bias, p_a, p_b, p_c, w_out, norm_mix, norm_ffn, peer_wq, peer_keys, peer_u, peer_v):
    batch, seq, d_model = x.shape
    xt = x.reshape(batch * seq, d_model)
    w_in_t = jnp.swapaxes(w_in, 1, 2)
    peer_u16 = peer_u.astype(BF16)
    peer_vt16 = jnp.swapaxes(peer_v.astype(BF16), 1, 2)
    for l in range(w_in.shape[0]):
        xt = _layer(xt, batch, seq, l, w_in_t, conv_w[l], a_log[l], dt_bias[l], a_out_gain[l], gmlp_norm[l],
                    w_spatial[l], b_spatial[l], c_q_gain[l], c_k_gain[l], rel_bias[l], p_a, p_b, p_c,
                    w_out, norm_mix[l], norm_ffn[l], peer_wq, peer_keys[l], peer_u16, peer_vt16)
    return xt.reshape(batch, seq, d_model)
```

```python
import functools

import jax
import jax.numpy as jnp
import numpy as np
from jax import lax
from jax.experimental import pallas as pl
from jax.experimental.pallas import tpu as pltpu

F32 = jnp.float32
BF16 = jnp.bfloat16
HIGHEST = lax.Precision.HIGHEST

CHUNK = 64
EPS = 1e-6
A_HEADS = 8
A_DK = 128
A_DV = 128
A_WIDTH = A_HEADS * A_DV
CONV_W = 4
GMLP_CHUNK = 128
GMLP_GROUPS = 8
GMLP_GDIM = 128
GMLP_WIDTH = GMLP_GROUPS * GMLP_GDIM
C_HEADS = 8
C_DH = 128
C_WIDTH = C_HEADS * C_DH
LEFT_CHUNKS = 8
BAND = (LEFT_CHUNKS + 1) * CHUNK
MAX_REL = 128
N_BRANCH = 3
PEER_HEADS = 8
PEER_NKEYS = 128
PEER_TOPK = 16
PEER_QDIM = 256
PEER_QHALF = PEER_QDIM // 2

LANES = 128
SUBLANES = 8
NEG_BIG = -1e30
VMEM_LIMIT = 56 * 1024 * 1024
CHUNK_SHIFT = CHUNK.bit_length() - 1


def _cparams(sem, vmem_limit=VMEM_LIMIT, flags=None):
    return pltpu.CompilerParams(dimension_semantics=sem, vmem_limit_bytes=vmem_limit, flags=flags)


def _nt_dot(a, b, precision=None):
    return lax.dot_general(a, b, (((1,), (1,)), ((), ())), precision=precision,
                           preferred_element_type=F32)


def _tn_dot(a, b, precision=None):
    return lax.dot_general(a, b, (((0,), (0,)), ((), ())), precision=precision,
                           preferred_element_type=F32)


def _bdot(a, b):
    return jnp.dot(a.astype(BF16), b.astype(BF16), preferred_element_type=F32)


def _pick(n, pref):
    t = min(pref, n)
    while n % t:
        t -= LANES if t > LANES else 8
    return t


def _rmsnorm_kernel(x_ref, g_ref, o_ref):
    x = x_ref[...]
    ms = jnp.mean(x * x, axis=-1, keepdims=True)
    o_ref[...] = (x * lax.rsqrt(ms + EPS) * g_ref[...]).astype(o_ref.dtype)


def _rmsnorm(x, gain, out_dtype=BF16, tm=512):
    m, d = x.shape
    tm = _pick(m, tm)
    return pl.pallas_call(
        _rmsnorm_kernel,
        grid=(m // tm,),
        in_specs=[pl.BlockSpec((tm, d), lambda i: (i, 0)), pl.BlockSpec((1, d), lambda i: (0, 0))],
        out_specs=pl.BlockSpec((tm, d), lambda i: (i, 0)),
        out_shape=jax.ShapeDtypeStruct((m, d), out_dtype),
        compiler_params=_cparams(("parallel",)),
        name="rmsnorm",
    )(x, gain.reshape(1, d))


def _mm_w32_kernel(*refs, shift, has_res, w_is_nk):
    refs = list(refs)
    a_ref, w_ref = refs[:2]
    tail_ref = refs[2] if shift else None
    rest = refs[3:] if shift else refs[2:]
    r_ref = rest[0] if has_res else None
    o_ref, wb_ref = rest[-2:]

    @pl.when(pl.program_id(1) == 0)
    def _():
        w = w_ref[...]
        if shift:
            w = jnp.concatenate([w[shift:], tail_ref[:shift]], axis=0)
        wb_ref[...] = w.astype(wb_ref.dtype)

    if w_is_nk:
        acc = _nt_dot(a_ref[...], wb_ref[...])
    else:
        acc = jnp.dot(a_ref[...], wb_ref[...], preferred_element_type=F32)
    if has_res:
        acc = r_ref[...] + acc
    o_ref[...] = acc.astype(o_ref.dtype)


def _matmul(a, w, layer, col0, n, out_dtype, residual=None, w_is_nk=False, tm=1024, tn=1024, name="matmul"):
    m, k = a.shape
    tm, tn = _pick(m, tm), _pick(n, tn)
    shift = col0 % LANES
    base = col0 - shift
    assert base % tn == 0 and shift % SUBLANES == 0 and (w_is_nk or not shift), (col0, tn)
    in_specs = [pl.BlockSpec((tm, k), lambda j, i: (i, 0))]
    if w_is_nk:
        in_specs.append(pl.BlockSpec((None, tn, k), lambda j, i: (layer, base // tn + j, 0)))
        wb_shape = (tn, k)
    else:
        in_specs.append(pl.BlockSpec((None, k, tn), lambda j, i: (layer, 0, base // tn + j)))
        wb_shape = (k, tn)
    args = [a, w]
    if shift:
        in_specs.append(pl.BlockSpec((None, LANES, k), lambda j, i: (layer, (base + (j + 1) * tn) // LANES, 0)))
        args.append(w)
    if residual is not None:
        in_specs.append(pl.BlockSpec((tm, tn), lambda j, i: (i, j)))
        args.append(residual)
    return pl.pallas_call(
        functools.partial(_mm_w32_kernel, shift=shift, has_res=residual is not None, w_is_nk=w_is_nk),
        grid=(n // tn, m // tm),
        in_specs=in_specs,
        out_specs=pl.BlockSpec((tm, tn), lambda j, i: (i, j)),
        out_shape=jax.ShapeDtypeStruct((m, n), out_dtype),
        scratch_shapes=[pltpu.VMEM(wb_shape, BF16)],
        compiler_params=_cparams(("parallel", "arbitrary")),
        name=name,
    )(*args)


def _conv_prep_kernel(cur_ref, prev_ref, w_ref, o_ref, ext_ref, *, ts, tc, prev_rows, n_qk_tiles, n_q_tiles):
    i = pl.program_id(1)
    c = pl.program_id(2)
    prev = prev_ref[...].astype(F32)
    prev = jnp.where(i == 0, 0.0, prev)
    cur = cur_ref[...].astype(F32)
    ext_ref[0:prev_rows, :] = prev
    ext_ref[prev_rows:prev_rows + ts, :] = cur
    w = w_ref[...]
    acc = cur * w[CONV_W - 1:CONV_W, :]
    for d in range(1, CONV_W):
        acc = acc + ext_ref[prev_rows - d:prev_rows - d + ts, :] * w[CONV_W - 1 - d:CONV_W - d, :]
    y = acc * jax.nn.sigmoid(acc)
    parts = []
    for h in range(tc // A_DK):
        yh = y[:, h * A_DK:(h + 1) * A_DK]
        parts.append(yh * lax.rsqrt(jnp.sum(yh * yh, axis=-1, keepdims=True) + EPS))
    yn = jnp.concatenate(parts, axis=-1)
    scale = jnp.where(c < n_q_tiles, A_DK ** -0.5, 1.0).astype(F32)
    o_ref[...] = jnp.where(c < n_qk_tiles, yn * scale, y).astype(o_ref.dtype)


def _conv_prep(qkv, conv_w, batch, seq, ts=1024, tc=512):
    t, ch = qkv.shape
    ts = _pick(seq, ts)
    prev_rows = 16
    n_s = seq // ts
    kern = functools.partial(_conv_prep_kernel, ts=ts, tc=tc, prev_rows=prev_rows,
                             n_qk_tiles=2 * A_WIDTH // tc, n_q_tiles=A_WIDTH // tc)
    rpb = ts // prev_rows
    return pl.pallas_call(
        kern,
        grid=(batch, n_s, ch // tc),
        in_specs=[
            pl.BlockSpec((ts, tc), lambda b, i, c: (b * n_s + i, c)),
            pl.BlockSpec((prev_rows, tc), lambda b, i, c: (jnp.maximum((b * n_s + i) * rpb - 1, 0), c)),
            pl.BlockSpec((CONV_W, tc), lambda b, i, c: (0, c)),
        ],
        out_specs=pl.BlockSpec((ts, tc), lambda b, i, c: (b * n_s + i, c)),
        out_shape=jax.ShapeDtypeStruct((t, ch), BF16),
        scratch_shapes=[pltpu.VMEM((prev_rows + ts, tc), F32)],
        compiler_params=_cparams(("parallel", "parallel", "parallel")),
        name="conv_prep",
    )(qkv, qkv, conv_w)


def _split3(x):
    hi = x.astype(BF16)
    r = x - hi.astype(F32)
    mid = r.astype(BF16)
    return hi, mid, (r - mid.astype(F32)).astype(BF16)


def _gates_kernel(ba_ref, alog_ref, dtb_ref, g_ref, b_ref, dec_ref, *, ts):
    def hdot(a, b):
        if a.dtype == BF16:
            return sum(jnp.dot(a, p, preferred_element_type=F32) for p in _split3(b))
        return sum(jnp.dot(p, b, preferred_element_type=F32) for p in _split3(a))

    ba = ba_ref[...]
    lane = lax.broadcasted_iota(jnp.int32, (ts, LANES), 1)
    head_lane = lane < A_HEADS
    beta = jnp.where(head_lane, jax.nn.sigmoid(ba), 0.0)
    alpha = pltpu.roll(ba, LANES - A_HEADS, axis=1)
    g = -jnp.exp(alog_ref[...]) * jax.nn.softplus(alpha + dtb_ref[...])
    g = jnp.where(head_lane, g, 0.0)
    r = lax.broadcasted_iota(jnp.int32, (ts, ts), 0)
    c = lax.broadcasted_iota(jnp.int32, (ts, ts), 1)
    same_chunk = (r >> CHUNK_SHIFT) == (c >> CHUNK_SHIFT)
    tri = jnp.where((c <= r) & same_chunk, 1.0, 0.0).astype(BF16)
    gcum = hdot(tri, g)
    er = lax.broadcasted_iota(jnp.int32, (LANES, A_WIDTH), 0)
    ec = lax.broadcasted_iota(jnp.int32, (LANES, A_WIDTH), 1)
    spread = jnp.where(er == (ec >> (A_DV.bit_length() - 1)), 1.0, 0.0).astype(BF16)
    g_ref[...] = hdot(gcum, spread)
    b_ref[...] = hdot(beta, spread)
    wd = A_HEADS * CHUNK
    er = lax.broadcasted_iota(jnp.int32, (LANES, wd), 0)
    ec = lax.broadcasted_iota(jnp.int32, (LANES, wd), 1)
    gi = hdot(gcum, jnp.where(er == (ec >> CHUNK_SHIFT), 1.0, 0.0).astype(BF16))
    ipos = lax.broadcasted_iota(jnp.int32, (ts, wd), 0) & (CHUNK - 1)
    jpos = lax.broadcasted_iota(jnp.int32, (ts, wd), 1) & (CHUNK - 1)
    blk = jnp.where(same_chunk, 1.0, 0.0).astype(BF16)
    gj = hdot(blk, jnp.where(ipos == jpos, gi, 0.0))
    dec_ref[...] = jnp.exp(jnp.where(ipos >= jpos, gi - gj, NEG_BIG))


def _gates(ba, a_log, dt_bias, ts=256):
    t = ba.shape[0]
    ts = _pick(t, ts)
    pad = lambda v: jnp.pad(v.astype(F32), (0, LANES - A_HEADS)).reshape(1, LANES)
    out = jax.ShapeDtypeStruct((t, A_WIDTH), F32)
    wd = A_HEADS * CHUNK
    return pl.pallas_call(
        functools.partial(_gates_kernel, ts=ts),
        grid=(t // ts,),
        in_specs=[pl.BlockSpec((ts, LANES), lambda i: (i, 0)),
                  pl.BlockSpec((1, LANES), lambda i: (0, 0)), pl.BlockSpec((1, LANES), lambda i: (0, 0))],
        out_specs=[pl.BlockSpec((ts, A_WIDTH), lambda i: (i, 0)), pl.BlockSpec((ts, A_WIDTH), lambda i: (i, 0)),
                   pl.BlockSpec((ts, wd), lambda i: (i, 0))],
        out_shape=[out, out, jax.ShapeDtypeStruct((t, wd), F32)],
        compiler_params=_cparams(("parallel",)),
        name="gates",
    )(ba, pad(a_log), pad(dt_bias))


def _delta_solve_kernel(q_ref, k_ref, v_ref, g_ref, b_ref, dec_ref, u_ref, wq_ref, kd_ref, at_ref, egl_ref,
                        *, ts, hp, group=8):
    ri = lax.broadcasted_iota(jnp.int32, (CHUNK, CHUNK), 0)
    ci = lax.broadcasted_iota(jnp.int32, (CHUNK, CHUNK), 1)
    strict = ri > ci
    ident = jnp.where(ri == ci, 1.0, 0.0).astype(F32)

    def load(n, h):
        rows = slice(n * CHUNK, (n + 1) * CHUNK)
        cols = slice(h * A_DK, (h + 1) * A_DK)
        c = dict(n=n, rows=rows, cols=cols, dcols=slice(h * CHUNK, (h + 1) * CHUNK))
        c["q"] = q_ref[rows, cols].astype(F32)
        c["k"] = k_ref[rows, cols].astype(F32)
        c["gb"] = g_ref[rows, cols]
        c["bt"] = b_ref[rows, cols]
        c["kb"] = c["k"] * c["bt"]
        c["eg"] = jnp.exp(c["gb"])
        return c

    chains = [(n, h) for n in range(ts // CHUNK) for h in range(hp)]
    for g0 in range(0, len(chains), group):
        cs = [load(n, h) for n, h in chains[g0:g0 + group]]
        for c in cs:
            qk_kk = _nt_dot(jnp.concatenate([c["q"], c["kb"]], axis=0).astype(BF16), c["k"].astype(BF16))
            dec = dec_ref[c["rows"], c["dcols"]]
            c["attn"] = qk_kk[:CHUNK] * dec
            c["a"] = jnp.where(strict, qk_kk[CHUNK:] * dec, 0.0)
        for c in cs:
            c["inv"] = ident - c["a"]
            c["p"] = _bdot(c["a"], c["a"])
        for _ in range(CHUNK_SHIFT - 2):
            for c in cs:
                y = _bdot(jnp.concatenate([c["inv"], c["p"]], axis=0), c["p"])
                c["inv"] = c["inv"] + y[:CHUNK]
                c["p"] = y[CHUNK:]
        for c in cs:
            c["inv"] = c["inv"] + _bdot(c["inv"], c["p"])
        for c in cs:
            v = v_ref[c["rows"], c["cols"]].astype(F32)
            c["uw"] = _bdot(c["inv"], jnp.concatenate([v * c["bt"], c["kb"] * c["eg"]], axis=1))
        for c in cs:
            n, rows, cols = c["n"], c["rows"], c["cols"]
            g_last = c["gb"][CHUNK - 1:CHUNK, :]
            u_ref[rows, cols] = c["uw"][:, :A_DV].astype(u_ref.dtype)
            wq_ref[2 * n * CHUNK:(2 * n + 1) * CHUNK, cols] = c["uw"][:, A_DV:].astype(wq_ref.dtype)
            wq_ref[(2 * n + 1) * CHUNK:(2 * n + 2) * CHUNK, cols] = (c["q"] * c["eg"]).astype(wq_ref.dtype)
            kd_ref[rows, cols] = (c["k"] * jnp.exp(g_last - c["gb"])).astype(kd_ref.dtype)
            at_ref[rows, c["dcols"]] = c["attn"].astype(at_ref.dtype)
            egl_ref[n * SUBLANES:(n + 1) * SUBLANES, cols] = jnp.broadcast_to(jnp.exp(g_last), (SUBLANES, A_DV))


def _delta_solve(qkv, gcum, beta, dec, ts=512, hp=4):
    t = qkv.shape[0]
    ts = _pick(t, ts)
    hw = hp * A_DK
    n_hb = A_WIDTH // hw
    spec = lambda off: pl.BlockSpec((ts, hw), lambda i, hb: (i, off * n_hb + hb))
    wide = jax.ShapeDtypeStruct((t, A_WIDTH), BF16)
    return pl.pallas_call(
        functools.partial(_delta_solve_kernel, ts=ts, hp=hp),
        grid=(t // ts, n_hb),
        in_specs=[spec(0), spec(1), spec(2), spec(0), spec(0),
                  pl.BlockSpec((ts, hp * CHUNK), lambda i, hb: (i, hb))],
        out_specs=[spec(0),
                   pl.BlockSpec((2 * ts, hw), lambda i, hb: (i, hb)),
                   spec(0),
                   pl.BlockSpec((ts, hp * CHUNK), lambda i, hb: (i, hb)),
                   pl.BlockSpec((ts // CHUNK * SUBLANES, hw), lambda i, hb: (i, hb))],
        out_shape=[wide, jax.ShapeDtypeStruct((2 * t, A_WIDTH), BF16), wide,
                   jax.ShapeDtypeStruct((t, A_HEADS * CHUNK), BF16),
                   jax.ShapeDtypeStruct((t // CHUNK * SUBLANES, A_WIDTH), F32)],
        compiler_params=_cparams(("parallel", "parallel")),
        name="delta_solve",
    )(qkv, qkv, qkv, gcum, beta, dec)


def _delta_rec_kernel(u_ref, wq_ref, kd_ref, at_ref, egl_ref, z_ref, gain_ref, o_ref, s_ref, *, ts):
    @pl.when(pl.program_id(1) == 0)
    def _():
        s_ref[...] = jnp.zeros_like(s_ref)

    gain = gain_ref[...]

    heads = range(A_HEADS)
    col = lambda h: slice(h * A_DK, (h + 1) * A_DK)
    for n in range(ts // CHUNK):
        rows = slice(n * CHUNK, (n + 1) * CHUNK)
        rows2 = slice(2 * n * CHUNK, (2 * n + 2) * CHUNK)
        s = [s_ref[h] for h in heads]
        ws = [jnp.dot(wq_ref[rows2, col(h)], s[h].astype(BF16), preferred_element_type=F32)
              for h in heads]
        v_new = [(u_ref[rows, col(h)].astype(F32) - ws[h][:CHUNK]).astype(BF16) for h in heads]
        o = [ws[h][CHUNK:] + jnp.dot(at_ref[rows, h * CHUNK:(h + 1) * CHUNK], v_new[h], preferred_element_type=F32)
             for h in heads]
        for h in heads:
            eg_last = egl_ref[n * SUBLANES:n * SUBLANES + 1, col(h)]
            s_ref[h] = s[h] * eg_last + _tn_dot(kd_ref[rows, col(h)], v_new[h])
        for h in heads:
            on = o[h] * lax.rsqrt(jnp.mean(o[h] * o[h], axis=-1, keepdims=True) + EPS) * gain
            z = z_ref[rows, col(h)].astype(F32)
            o_ref[rows, col(h)] = (on * (z * jax.nn.sigmoid(z))).astype(o_ref.dtype)


def _delta_rec(u, wq, kd, at, egl, z, out_gain, batch, seq, ts=512):
    t = u.shape[0]
    ts = _pick(seq, ts)
    n_s = seq // ts
    row = lambda b, i: (b * n_s + i, 0)
    return pl.pallas_call(
        functools.partial(_delta_rec_kernel, ts=ts),
        grid=(batch, n_s),
        in_specs=[pl.BlockSpec((ts, A_WIDTH), row), pl.BlockSpec((2 * ts, A_WIDTH), row),
                  pl.BlockSpec((ts, A_WIDTH), row), pl.BlockSpec((ts, A_HEADS * CHUNK), row),
                  pl.BlockSpec((ts // CHUNK * SUBLANES, A_WIDTH), row), pl.BlockSpec((ts, A_WIDTH), row),
                  pl.BlockSpec((1, A_DV), lambda b, i: (0, 0))],
        out_specs=pl.BlockSpec((ts, A_WIDTH), row),
        out_shape=jax.ShapeDtypeStruct((t, A_WIDTH), BF16),
        scratch_shapes=[pltpu.VMEM((A_HEADS, A_DK, A_DV), F32)],
        compiler_params=_cparams(("parallel", "arbitrary")),
        name="delta_rec",
    )(u, wq, kd, at, egl, z, out_gain.reshape(1, A_DV).astype(F32))


def _gmlp_kernel(u_ref, v_ref, gain_ref, w_ref, bias_ref, o_ref, *, nb):
    ri = lax.broadcasted_iota(jnp.int32, (GMLP_CHUNK, GMLP_CHUNK), 0)
    ci = lax.broadcasted_iota(jnp.int32, (GMLP_CHUNK, GMLP_CHUNK), 1)
    tril = ri >= ci
    gain = gain_ref[...]
    for blk in range(nb):
        rows = slice(blk * GMLP_CHUNK, (blk + 1) * GMLP_CHUNK)
        u = jax.nn.gelu(u_ref[rows, :].astype(F32))
        v = jax.nn.gelu(v_ref[rows, :].astype(F32))
        vn = (v * lax.rsqrt(jnp.mean(v * v, axis=-1, keepdims=True) + EPS) * gain).astype(BF16)
        for g in range(GMLP_GROUPS):
            cols = slice(g * GMLP_GDIM, (g + 1) * GMLP_GDIM)
            w = jnp.where(tril, w_ref[g], 0.0).astype(BF16)
            mixed = jnp.dot(w, vn[:, cols], preferred_element_type=F32) + bias_ref[:, cols]
            o_ref[rows, cols] = (u[:, cols] * mixed).astype(o_ref.dtype)


def _gmlp(uv, norm_gain, w_spatial, b_spatial, nb=8):
    t = uv.shape[0]
    tm = nb * GMLP_CHUNK
    bias = jnp.repeat(b_spatial.T.astype(F32), GMLP_GDIM, axis=1)
    return pl.pallas_call(
        functools.partial(_gmlp_kernel, nb=nb),
        grid=(t // tm,),
        in_specs=[pl.BlockSpec((tm, GMLP_WIDTH), lambda i: (i, 0)),
                  pl.BlockSpec((tm, GMLP_WIDTH), lambda i: (i, 1)),
                  pl.BlockSpec((1, GMLP_WIDTH), lambda i: (0, 0)),
                  pl.BlockSpec((GMLP_GROUPS, GMLP_CHUNK, GMLP_CHUNK), lambda i: (0, 0, 0)),
                  pl.BlockSpec((GMLP_CHUNK, GMLP_WIDTH), lambda i: (0, 0))],
        out_specs=pl.BlockSpec((tm, GMLP_WIDTH), lambda i: (i, 0)),
        out_shape=jax.ShapeDtypeStruct((t, GMLP_WIDTH), BF16),
        compiler_params=_cparams(("parallel",)),
        name="gmlp",
    )(uv, uv, norm_gain.reshape(1, GMLP_WIDTH).astype(F32), w_spatial.astype(F32), bias)


def _band_kernel(q_ref, kp_ref, kc_ref, vp_ref, vc_ref, qg_ref, kg_ref, bias_ref, o_ref, *, tq):
    i = pl.program_id(2)

    def norm(x, gain):
        x = x.astype(F32)
        return x * lax.rsqrt(jnp.mean(x * x, axis=-1, keepdims=True) + EPS) * gain

    qn = norm(q_ref[...], qg_ref[...]).astype(BF16)
    kcat = jnp.concatenate([norm(kp_ref[...], kg_ref[...]), norm(kc_ref[...], kg_ref[...])], axis=0).astype(BF16)
    vcat = jnp.concatenate([vp_ref[...], vc_ref[...]], axis=0).astype(BF16)
    bias = bias_ref[0]
    pad = LEFT_CHUNKS * CHUNK
    kpos = lax.broadcasted_iota(jnp.int32, (CHUNK, BAND), 1)
    chunks = range(tq // CHUNK)
    lo = [c * CHUNK + (tq - pad) for c in chunks]
    s = [_nt_dot(qn[c * CHUNK:(c + 1) * CHUNK], kcat[lo[c]:lo[c] + BAND]) for c in chunks]
    p = []
    for c in chunks:
        sc = s[c] * (C_DH ** -0.5) + bias
        valid = (i > 0) | (kpos + c * CHUNK >= pad)
        sc = jnp.where(valid, sc, NEG_BIG)
        e = jnp.exp(sc - jnp.max(sc, axis=-1, keepdims=True))
        p.append((e / jnp.sum(e, axis=-1, keepdims=True)).astype(BF16))
    o = [jnp.dot(p[c], vcat[lo[c]:lo[c] + BAND], preferred_element_type=F32) for c in chunks]
    for c in chunks:
        o_ref[c * CHUNK:(c + 1) * CHUNK, :] = o[c].astype(o_ref.dtype)


def _band_bias(rel_bias):
    diag = np.arange(-(CHUNK - 1), BAND)
    idx = np.clip(LEFT_CHUNKS * CHUNK - diag, -MAX_REL, MAX_REL) + MAX_REL
    vec = rel_bias.astype(F32)[:, idx]
    return jnp.stack([vec[:, CHUNK - 1 - i:CHUNK - 1 - i + BAND] for i in range(CHUNK)], axis=1)


def _band_attention(qkv, q_gain, k_gain, rel_bias, batch, seq):
    t = qkv.shape[0]
    tq = LEFT_CHUNKS * CHUNK
    n_s = seq // tq
    bias = _band_bias(rel_bias)
    cur = lambda off: pl.BlockSpec((tq, C_DH), lambda b, h, i: (b * n_s + i, off * C_HEADS + h))
    prv = lambda off: pl.BlockSpec((tq, C_DH), lambda b, h, i: (b * n_s + jnp.maximum(i - 1, 0), off * C_HEADS + h))
    vec = pl.BlockSpec((1, C_DH), lambda b, h, i: (0, 0))
    return pl.pallas_call(
        functools.partial(_band_kernel, tq=tq),
        grid=(batch, C_HEADS, n_s),
        in_specs=[cur(0), prv(1), cur(1), prv(2), cur(2), vec, vec,
                  pl.BlockSpec((1, CHUNK, BAND), lambda b, h, i: (h, 0, 0))],
        out_specs=pl.BlockSpec((tq, C_DH), lambda b, h, i: (b * n_s + i, h)),
        out_shape=jax.ShapeDtypeStruct((t, C_WIDTH), BF16),
        compiler_params=_cparams(("parallel", "parallel", "parallel")),
        name="band_attention",
    )(qkv, qkv, qkv, qkv, qkv, q_gain.reshape(1, C_DH).astype(F32), k_gain.reshape(1, C_DH).astype(F32), bias)


def _merge_kernel(ya_ref, yb_ref, yc_ref, pa_ref, pb_ref, pc_ref, ga_ref, gb_ref, gc_ref, o_ref,
                  wa_ref, wb_ref, wc_ref):
    @pl.when(pl.program_id(1) == 0)
    def _():
        for p_ref, w_ref in ((pa_ref, wa_ref), (pb_ref, wb_ref), (pc_ref, wc_ref)):
            w_ref[...] = p_ref[...].astype(w_ref.dtype)

    def branch(y_ref, w_ref, g_ref):
        gate = jax.nn.sigmoid(g_ref[...].astype(F32))
        return gate * jnp.dot(y_ref[...], w_ref[...], preferred_element_type=F32)

    merged = branch(ya_ref, wa_ref, ga_ref) + branch(yb_ref, wb_ref, gb_ref) + branch(yc_ref, wc_ref, gc_ref)
    o_ref[...] = merged.astype(o_ref.dtype)


def _merge(ya, yb, yc, pa, pb, pc, layer, gate, tm=1024, tn=512):
    t, k = ya.shape
    d = pa.shape[2]
    tm, tn = _pick(t, tm), _pick(d, tn)
    nd = d // tn
    ysp = pl.BlockSpec((tm, k), lambda j, i: (i, 0))
    psp = pl.BlockSpec((None, k, tn), lambda j, i: (layer, 0, j))
    gsp = lambda br: pl.BlockSpec((tm, tn), lambda j, i: (i, br * nd + j))
    return pl.pallas_call(
        _merge_kernel,
        grid=(nd, t // tm),
        in_specs=[ysp, ysp, ysp, psp, psp, psp, gsp(0), gsp(1), gsp(2)],
        out_specs=pl.BlockSpec((tm, tn), lambda j, i: (i, j)),
        out_shape=jax.ShapeDtypeStruct((t, d), BF16),
        scratch_shapes=[pltpu.VMEM((k, tn), BF16)] * 3,
        compiler_params=_cparams(("parallel", "arbitrary")),
        name="merge",
    )(ya, yb, yc, pa, pb, pc, gate, gate, gate)


def _top_values(x, k, out_ref, want_rank=False):
    cur = x
    rank = jnp.full(x.shape, float(k), F32) if want_rank else None
    for r in range(k):
        m = jnp.max(cur, axis=0, keepdims=True)
        out_ref[r:r + 1, :] = m
        hit = cur == m
        if want_rank:
            rank = jnp.where(hit, float(r), rank)
        if r + 1 < k:
            cur = jnp.where(hit, NEG_BIG, cur)
    return rank


def _peer_select_kernel(q_ref, keys_ref, cnt_ref, f1_ref, r2_ref, e2_ref, top_ref, *, tt):
    k = PEER_TOPK

    def head(h, sub):
        a_ref, b_ref, c_ref = top_ref.at[sub, 0], top_ref.at[sub, 1], top_ref.at[sub, 2]
        qh = q_ref[:, pl.ds(pl.multiple_of(h * PEER_QDIM, PEER_QDIM), PEER_QDIM)]
        s1 = _nt_dot(keys_ref[h, 0], qh[:, :PEER_QHALF], precision=HIGHEST)
        s2 = _nt_dot(keys_ref[h, 1], qh[:, PEER_QHALF:], precision=HIGHEST)
        _top_values(s1, k, a_ref)
        rank2 = _top_values(s2, k, b_ref, want_rank=True)
        av = a_ref[...]
        bv = b_ref[...]
        half = k // 2
        cand = jnp.concatenate([av[0:1] + bv] + [av[r:r + 1] + bv[:half] for r in range(1, half)]
                               + [av[half:] + bv[0:1]], axis=0)
        _top_values(cand, k, c_ref)
        cv = c_ref[...]
        z = jnp.sum(jnp.exp(cv - cv[0:1, :]), axis=0, keepdims=True)
        tau = cv[k - 1:k, :]
        cnt = jnp.zeros(s1.shape, F32)
        for c in range(k):
            ok = av + bv[c:c + 1] >= tau
            thr = jnp.min(jnp.where(ok, av, -NEG_BIG), axis=0, keepdims=True)
            cnt = cnt + jnp.where(s1 >= thr, 1.0, 0.0)
        cnt_ref[:, pl.ds(h, 1), :] = cnt[:, None, :]
        f1_ref[:, pl.ds(h, 1), :] = (jnp.exp(s1 - av[0:1]) / z)[:, None, :]
        rank2 = rank2.astype(r2_ref.dtype)
        e2 = jnp.exp(s2 - bv[0:1]).astype(e2_ref.dtype)
        rb = 2 * SUBLANES
        for g in range(PEER_NKEYS // rb):
            for tl in range(tt // LANES):
                r2_ref[h * (PEER_NKEYS // rb) + g, tl] = rank2[g * rb:(g + 1) * rb, tl * LANES:(tl + 1) * LANES]
                e2_ref[h * (PEER_NKEYS // rb) + g, tl] = e2[g * rb:(g + 1) * rb, tl * LANES:(tl + 1) * LANES]

    def pair(p, carry):
        head(2 * p, 0)
        head(2 * p + 1, 1)
        return carry

    lax.fori_loop(0, PEER_HEADS // 2, pair, 0)


def _peer_select(q, keys, tt=256):
    t = q.shape[0]
    tt = _pick(t, tt)
    shape = (PEER_NKEYS, PEER_HEADS, t)
    bspec = pl.BlockSpec((PEER_NKEYS, PEER_HEADS, tt), lambda i: (0, 0, i))
    n_slab = PEER_HEADS * PEER_NKEYS // (2 * SUBLANES)
    flat = (n_slab, t // LANES, 2 * SUBLANES, LANES)
    flat_spec = pl.BlockSpec((n_slab, tt // LANES, 2 * SUBLANES, LANES), lambda i: (0, i, 0, 0))
    return pl.pallas_call(
        functools.partial(_peer_select_kernel, tt=tt),
        grid=(t // tt,),
        in_specs=[pl.BlockSpec((tt, PEER_HEADS * PEER_QDIM), lambda i: (i, 0)),
                  pl.BlockSpec((PEER_HEADS, 2, PEER_NKEYS, PEER_QHALF), lambda i: (0, 0, 0, 0))],
        out_specs=[bspec, bspec, flat_spec, flat_spec],
        out_shape=[jax.ShapeDtypeStruct(shape, F32), jax.ShapeDtypeStruct(shape, F32),
                   jax.ShapeDtypeStruct(flat, BF16), jax.ShapeDtypeStruct(flat, BF16)],
        scratch_shapes=[pltpu.VMEM((2, 3, PEER_TOPK, tt), F32)],
        compiler_params=_cparams(("parallel",)),
        name="peer_select",
    )(q, keys.astype(F32))


def _peer_dense_kernel(hn_ref, u_ref, vt_ref, cnt_ref, f1_ref, r2_ref, e2_ref, x_ref, o_ref,
                       acc_ref, ht0_ref, ht1_ref, g0_ref, g1_ref, *, tt, ec, nc, d):
    s = pl.program_id(0)
    c_out = lax.rem(jnp.maximum(s - 2, 0), nc)

    @pl.when(s == 0)
    def _():
        for ref in (ht0_ref, ht1_ref, g0_ref, g1_ref):
            ref[...] = jnp.zeros_like(ref)

    @pl.when(c_out == 0)
    def _():
        acc_ref[...] = jnp.zeros_like(acc_ref)

    nk = PEER_NKEYS

    def stages(ht_w, ht_r, g_w, g_r):
        halves = [slice(0, tt // 2), slice(tt // 2, tt)]

        def stage_a(hs):
            ht_w[:, hs] = _nt_dot(u_ref[...], hn_ref[hs, :]).astype(ht_w.dtype)

        def stage_c(hs):
            acc_ref[:, hs] += jnp.dot(vt_ref[...], g_r[:, hs], preferred_element_type=F32)

        mxu_pieces = [functools.partial(stage_a, halves[0]), functools.partial(stage_a, halves[1]),
                      functools.partial(stage_c, halves[0]), functools.partial(stage_c, halves[1])]
        rb = 2 * SUBLANES
        tiles = [(ii, tg) for ii in range(ec // nk) for tg in range(tt // LANES)]
        per_piece = len(tiles) // len(mxu_pieces)
        for t_idx, (ii, tg) in enumerate(tiles):
            if t_idx % per_piece == 0:
                mxu_pieces[t_idx // per_piece]()
            lanes = slice(tg * LANES, (tg + 1) * LANES)
            bcast = lambda ref, h: jnp.broadcast_to(ref[ii, h:h + 1, lanes], (rb, LANES)).astype(BF16)
            cnt = [bcast(cnt_ref, h) for h in range(PEER_HEADS)]
            f1 = [bcast(f1_ref, h) for h in range(PEER_HEADS)]
            for j0 in range(0, nk, rb):
                wsel = jnp.zeros((rb, LANES), BF16)
                for h in range(PEER_HEADS):
                    slab = (h * nk + j0) // rb
                    picked = jnp.clip(cnt[h] - r2_ref[slab, tg], 0, 1)
                    wsel = wsel + picked * (f1[h] * e2_ref[slab, tg])
                rows = slice(ii * nk + j0, ii * nk + j0 + rb)
                g_w[rows, lanes] = wsel * jax.nn.gelu(ht_r[rows, lanes])

    parity = lax.rem(s, 2)
    pl.when(parity == 0)(functools.partial(stages, ht0_ref, ht1_ref, g1_ref, g0_ref))
    pl.when(parity == 1)(functools.partial(stages, ht1_ref, ht0_ref, g0_ref, g1_ref))

    @pl.when((c_out == nc - 1) & (s >= 2))
    def _():
        step = 512
        for d0 in range(0, d, step):
            o_ref[:, d0:d0 + step] = x_ref[:, d0:d0 + step] + acc_ref[d0:d0 + step, :].T


def _peer_dense(hn, u_tab, vt_tab, layer, cntr, f1r, r2, e2, x, tt=512, ec=1024):
    t, d = hn.shape
    e = u_tab.shape[1]
    tt, ec = _pick(t, tt), _pick(e, ec)
    ni = ec // PEER_NKEYS
    nc = e // ec
    n_steps = (t // tt) * nc
    n_slab = r2.shape[0]
    tile = lambda s, lag: jnp.clip(s - lag, 0, n_steps - 1) // nc
    chunk = lambda s, lag: jnp.clip(s - lag, 0, n_steps - 1) % nc
    once = dict(pipeline_mode=pl.Buffered(1))
    return pl.pallas_call(
        functools.partial(_peer_dense_kernel, tt=tt, ec=ec, nc=nc, d=d),
        grid=(n_steps + 2,),
        in_specs=[pl.BlockSpec((tt, d), lambda s: (tile(s, 0), 0), **once),
                  pl.BlockSpec((None, ec, d), lambda s: (layer, chunk(s, 0), 0)),
                  pl.BlockSpec((None, d, ec), lambda s: (layer, 0, chunk(s, 2))),
                  pl.BlockSpec((ni, PEER_HEADS, tt), lambda s: (chunk(s, 1), 0, tile(s, 1))),
                  pl.BlockSpec((ni, PEER_HEADS, tt), lambda s: (chunk(s, 1), 0, tile(s, 1))),
                  pl.BlockSpec((n_slab, tt // LANES, 2 * SUBLANES, LANES), lambda s: (0, tile(s, 1), 0, 0)),
                  pl.BlockSpec((n_slab, tt // LANES, 2 * SUBLANES, LANES), lambda s: (0, tile(s, 1), 0, 0)),
                  pl.BlockSpec((tt, d), lambda s: (tile(s, 2), 0), **once)],
        out_specs=pl.BlockSpec((tt, d), lambda s: (tile(s, 2), 0)),
        out_shape=jax.ShapeDtypeStruct((t, d), F32),
        scratch_shapes=[pltpu.VMEM((d, tt), F32)] + [pltpu.VMEM((ec, tt), BF16)] * 4,
        compiler_params=_cparams(("arbitrary",)),
        name="peer_dense",
    )(hn, u_tab, vt_tab, cntr, f1r, r2, e2, x)


def _layer(x, batch, seq, layer, w_in, conv_w, a_log, dt_bias, a_out_gain, gmlp_norm, w_spatial, b_spatial,
           c_q_gain, c_k_gain, rel_bias, p_a, p_b, p_c, w_out, norm_mix, norm_ffn,
           peer_wq, peer_keys, peer_u, peer_vt):
    o_qkv_a = 0
    o_z = o_qkv_a + 3 * A_WIDTH
    o_beta = o_z + A_WIDTH
    o_uv = o_beta + 2 * A_HEADS
    o_qkv_c = o_uv + 2 * GMLP_WIDTH
    o_gate = o_qkv_c + 3 * C_WIDTH
    d_model = x.shape[1]

    h = _rmsnorm(x, norm_mix)
    proj = lambda col0, n, dtype, name: _matmul(h, w_in, layer, col0, n, dtype, w_is_nk=True, name=name)
    qkv_a = proj(o_qkv_a, 3 * A_WIDTH, BF16, "proj_qkv_a")
    z_a = proj(o_z, A_WIDTH, BF16, "proj_z_a")
    ba = proj(o_beta, LANES, F32, "proj_beta_alpha")
    uv_b = proj(o_uv, 2 * GMLP_WIDTH, BF16, "proj_uv_b")
    qkv_c = proj(o_qkv_c, 3 * C_WIDTH, BF16, "proj_qkv_c")
    gate = proj(o_gate, N_BRANCH * d_model, BF16, "proj_gate")

    qkv_prep = _conv_prep(qkv_a, conv_w.astype(F32), batch, seq)
    gcum, beta, dec = _gates(ba, a_log, dt_bias)
    u, wq, kd, at, egl = _delta_solve(qkv_prep, gcum, beta, dec)
    y_a = _delta_rec(u, wq, kd, at, egl, z_a, a_out_gain, batch, seq)
    y_b = _gmlp(uv_b, gmlp_norm, w_spatial, b_spatial)
    y_c = _band_attention(qkv_c, c_q_gain, c_k_gain, rel_bias, batch, seq)

    merged = _merge(y_a, y_b, y_c, p_a, p_b, p_c, layer, gate)
    x = _matmul(merged, w_out, layer, 0, d_model, F32, residual=x, name="out_proj")

    hn = _rmsnorm(x, norm_ffn)
    q = _matmul(hn, peer_wq, layer, 0, peer_wq.shape[2], F32, name="peer_query")
    cnt, f1, r2, e2 = _peer_select(q, peer_keys)
    return _peer_dense(hn, peer_u, peer_vt, layer, cnt, f1, r2, e2, x)


def kernel(x, w_in, conv_w, a_log, dt_bias, a_out_gain, gmlp_norm, w_spatial, b_spatial, c_q_gain, c_k_gain,
           rel_bias, p_a, p_b, p_c, w_out, norm_mix, norm_ffn, peer_wq, peer_keys, peer_u, peer_v):
    batch, seq, d_model = x.shape
    xt = x.reshape(batch * seq, d_model)
    w_in_t = jnp.swapaxes(w_in, 1, 2)
    peer_u16 = peer_u.astype(BF16)
    peer_vt16 = jnp.swapaxes(peer_v.astype(BF16), 1, 2)
    for l in range(w_in.shape[0]):
        xt = _layer(xt, batch, seq, l, w_in_t, conv_w[l], a_log[l], dt_bias[l], a_out_gain[l], gmlp_norm[l],
                    w_spatial[l], b_spatial[l], c_q_gain[l], c_k_gain[l], rel_bias[l], p_a, p_b, p_c,
                    w_out, norm_mix[l], norm_ffn[l], peer_wq, peer_keys[l], peer_u16, peer_vt16)
    return xt.reshape(batch, seq, d_model)
```

```python
import functools

import jax
import jax.numpy as jnp
import numpy as np
from jax import lax
from jax.experimental import pallas as pl
from jax.experimental.pallas import tpu as pltpu

F32 = jnp.float32
BF16 = jnp.bfloat16
HIGHEST = lax.Precision.HIGHEST

CHUNK = 64
EPS = 1e-6
A_HEADS = 8
A_DK = 128
A_DV = 128
A_WIDTH = A_HEADS * A_DV
CONV_W = 4
GMLP_CHUNK = 128
GMLP_GROUPS = 8
GMLP_GDIM = 128
GMLP_WIDTH = GMLP_GROUPS * GMLP_GDIM
C_HEADS = 8
C_DH = 128
C_WIDTH = C_HEADS * C_DH
LEFT_CHUNKS = 8
BAND = (LEFT_CHUNKS + 1) * CHUNK
MAX_REL = 128
N_BRANCH = 3
PEER_HEADS = 8
PEER_NKEYS = 128
PEER_TOPK = 16
PEER_QDIM = 256
PEER_QHALF = PEER_QDIM // 2

LANES = 128
SUBLANES = 8
NEG_BIG = -1e30
VMEM_LIMIT = 56 * 1024 * 1024
CHUNK_SHIFT = CHUNK.bit_length() - 1


def _cparams(sem, vmem_limit=VMEM_LIMIT, flags=None):
    return pltpu.CompilerParams(dimension_semantics=sem, vmem_limit_bytes=vmem_limit, flags=flags)


def _nt_dot(a, b, precision=None):
    return lax.dot_general(a, b, (((1,), (1,)), ((), ())), precision=precision,
                           preferred_element_type=F32)


def _tn_dot(a, b, precision=None):
    return lax.dot_general(a, b, (((0,), (0,)), ((), ())), precision=precision,
                           preferred_element_type=F32)


def _bdot(a, b):
    return jnp.dot(a.astype(BF16), b.astype(BF16), preferred_element_type=F32)


def _sigmoid(x):
    return 0.5 * jnp.tanh(0.5 * x) + 0.5


def _pick(n, pref):
    t = min(pref, n)
    while n % t:
        t -= LANES if t > LANES else 8
    return t


def _rmsnorm_kernel(x_ref, g_ref, o_ref):
    x = x_ref[...]
    ms = jnp.mean(x * x, axis=-1, keepdims=True)
    o_ref[...] = (x * lax.rsqrt(ms + EPS) * g_ref[...]).astype(o_ref.dtype)


def _rmsnorm(x, gain, out_dtype=BF16, tm=512):
    m, d = x.shape
    tm = _pick(m, tm)
    return pl.pallas_call(
        _rmsnorm_kernel,
        grid=(m // tm,),
        in_specs=[pl.BlockSpec((tm, d), lambda i: (i, 0)), pl.BlockSpec((1, d), lambda i: (0, 0))],
        out_specs=pl.BlockSpec((tm, d), lambda i: (i, 0)),
        out_shape=jax.ShapeDtypeStruct((m, d), out_dtype),
        compiler_params=_cparams(("parallel",)),
        name="rmsnorm",
    )(x, gain.reshape(1, d))


def _mm_w32_kernel(*refs, shift, has_res, w_is_nk):
    refs = list(refs)
    a_ref, w_ref = refs[:2]
    tail_ref = refs[2] if shift else None
    rest = refs[3:] if shift else refs[2:]
    r_ref = rest[0] if has_res else None
    o_ref, wb_ref = rest[-2:]

    @pl.when(pl.program_id(1) == 0)
    def _():
        w = w_ref[...]
        if shift:
            w = jnp.concatenate([w[shift:], tail_ref[:shift]], axis=0)
        wb_ref[...] = w.astype(wb_ref.dtype)

    if w_is_nk:
        acc = _nt_dot(a_ref[...], wb_ref[...])
    else:
        acc = jnp.dot(a_ref[...], wb_ref[...], preferred_element_type=F32)
    if has_res:
        acc = r_ref[...] + acc
    o_ref[...] = acc.astype(o_ref.dtype)


def _matmul(a, w, layer, col0, n, out_dtype, residual=None, w_is_nk=False, tm=1024, tn=1024, name="matmul"):
    m, k = a.shape
    tm, tn = _pick(m, tm), _pick(n, tn)
    shift = col0 % LANES
    base = col0 - shift
    assert base % tn == 0 and shift % SUBLANES == 0 and (w_is_nk or not shift), (col0, tn)
    in_specs = [pl.BlockSpec((tm, k), lambda j, i: (i, 0))]
    if w_is_nk:
        in_specs.append(pl.BlockSpec((None, tn, k), lambda j, i: (layer, base // tn + j, 0)))
        wb_shape = (tn, k)
    else:
        in_specs.append(pl.BlockSpec((None, k, tn), lambda j, i: (layer, 0, base // tn + j)))
        wb_shape = (k, tn)
    args = [a, w]
    if shift:
        in_specs.append(pl.BlockSpec((None, LANES, k), lambda j, i: (layer, (base + (j + 1) * tn) // LANES, 0)))
        args.append(w)
    if residual is not None:
        in_specs.append(pl.BlockSpec((tm, tn), lambda j, i: (i, j)))
        args.append(residual)
    return pl.pallas_call(
        functools.partial(_mm_w32_kernel, shift=shift, has_res=residual is not None, w_is_nk=w_is_nk),
        grid=(n // tn, m // tm),
        in_specs=in_specs,
        out_specs=pl.BlockSpec((tm, tn), lambda j, i: (i, j)),
        out_shape=jax.ShapeDtypeStruct((m, n), out_dtype),
        scratch_shapes=[pltpu.VMEM(wb_shape, BF16)],
        compiler_params=_cparams(("parallel", "arbitrary")),
        name=name,
    )(*args)


def _conv_prep_kernel(cur_ref, prev_ref, w_ref, o_ref, ext_ref, *, ts, tc, prev_rows, n_qk_tiles, n_q_tiles):
    i = pl.program_id(1)
    c = pl.program_id(2)
    prev = prev_ref[...].astype(F32)
    prev = jnp.where(i == 0, 0.0, prev)
    cur = cur_ref[...].astype(F32)
    ext_ref[0:prev_rows, :] = prev
    ext_ref[prev_rows:prev_rows + ts, :] = cur
    w = w_ref[...]
    acc = cur * w[CONV_W - 1:CONV_W, :]
    for d in range(1, CONV_W):
        acc = acc + ext_ref[prev_rows - d:prev_rows - d + ts, :] * w[CONV_W - 1 - d:CONV_W - d, :]
    y = acc * _sigmoid(acc)
    parts = []
    for h in range(tc // A_DK):
        yh = y[:, h * A_DK:(h + 1) * A_DK]
        parts.append(yh * lax.rsqrt(jnp.sum(yh * yh, axis=-1, keepdims=True) + EPS))
    yn = jnp.concatenate(parts, axis=-1)
    scale = jnp.where(c < n_q_tiles, A_DK ** -0.5, 1.0).astype(F32)
    o_ref[...] = jnp.where(c < n_qk_tiles, yn * scale, y).astype(o_ref.dtype)


def _conv_prep(qkv, conv_w, batch, seq, ts=1024, tc=512):
    t, ch = qkv.shape
    ts = _pick(seq, ts)
    prev_rows = 16
    n_s = seq // ts
    kern = functools.partial(_conv_prep_kernel, ts=ts, tc=tc, prev_rows=prev_rows,
                             n_qk_tiles=2 * A_WIDTH // tc, n_q_tiles=A_WIDTH // tc)
    rpb = ts // prev_rows
    return pl.pallas_call(
        kern,
        grid=(batch, n_s, ch // tc),
        in_specs=[
            pl.BlockSpec((ts, tc), lambda b, i, c: (b * n_s + i, c)),
            pl.BlockSpec((prev_rows, tc), lambda b, i, c: (jnp.maximum((b * n_s + i) * rpb - 1, 0), c)),
            pl.BlockSpec((CONV_W, tc), lambda b, i, c: (0, c)),
        ],
        out_specs=pl.BlockSpec((ts, tc), lambda b, i, c: (b * n_s + i, c)),
        out_shape=jax.ShapeDtypeStruct((t, ch), BF16),
        scratch_shapes=[pltpu.VMEM((prev_rows + ts, tc), F32)],
        compiler_params=_cparams(("parallel", "parallel", "parallel")),
        name="conv_prep",
    )(qkv, qkv, conv_w)


def _split3(x):
    hi = x.astype(BF16)
    r = x - hi.astype(F32)
    mid = r.astype(BF16)
    return hi, mid, (r - mid.astype(F32)).astype(BF16)


def _gates_kernel(ba_ref, alog_ref, dtb_ref, g_ref, b_ref, dec_ref, *, ts):
    def hdot(a, b):
        if a.dtype == BF16:
            return sum(jnp.dot(a, p, preferred_element_type=F32) for p in _split3(b))
        return sum(jnp.dot(p, b, preferred_element_type=F32) for p in _split3(a))

    ba = ba_ref[...]
    lane = lax.broadcasted_iota(jnp.int32, (ts, LANES), 1)
    head_lane = lane < A_HEADS
    beta = jnp.where(head_lane, jax.nn.sigmoid(ba), 0.0)
    alpha = pltpu.roll(ba, LANES - A_HEADS, axis=1)
    g = -jnp.exp(alog_ref[...]) * jax.nn.softplus(alpha + dtb_ref[...])
    g = jnp.where(head_lane, g, 0.0)
    r = lax.broadcasted_iota(jnp.int32, (ts, ts), 0)
    c = lax.broadcasted_iota(jnp.int32, (ts, ts), 1)
    same_chunk = (r >> CHUNK_SHIFT) == (c >> CHUNK_SHIFT)
    tri = jnp.where((c <= r) & same_chunk, 1.0, 0.0).astype(BF16)
    gcum = hdot(tri, g)
    er = lax.broadcasted_iota(jnp.int32, (LANES, A_WIDTH), 0)
    ec = lax.broadcasted_iota(jnp.int32, (LANES, A_WIDTH), 1)
    spread = jnp.where(er == (ec >> (A_DV.bit_length() - 1)), 1.0, 0.0).astype(BF16)
    g_ref[...] = hdot(gcum, spread)
    b_ref[...] = hdot(beta, spread)
    wd = A_HEADS * CHUNK
    er = lax.broadcasted_iota(jnp.int32, (LANES, wd), 0)
    ec = lax.broadcasted_iota(jnp.int32, (LANES, wd), 1)
    gi = hdot(gcum, jnp.where(er == (ec >> CHUNK_SHIFT), 1.0, 0.0).astype(BF16))
    ipos = lax.broadcasted_iota(jnp.int32, (ts, wd), 0) & (CHUNK - 1)
    jpos = lax.broadcasted_iota(jnp.int32, (ts, wd), 1) & (CHUNK - 1)
    blk = jnp.where(same_chunk, 1.0, 0.0).astype(BF16)
    gj = hdot(blk, jnp.where(ipos == jpos, gi, 0.0))
    dec_ref[...] = jnp.exp(jnp.where(ipos >= jpos, gi - gj, NEG_BIG))


def _gates(ba, a_log, dt_bias, ts=256):
    t = ba.shape[0]
    ts = _pick(t, ts)
    pad = lambda v: jnp.pad(v.astype(F32), (0, LANES - A_HEADS)).reshape(1, LANES)
    out = jax.ShapeDtypeStruct((t, A_WIDTH), F32)
    wd = A_HEADS * CHUNK
    return pl.pallas_call(
        functools.partial(_gates_kernel, ts=ts),
        grid=(t // ts,),
        in_specs=[pl.BlockSpec((ts, LANES), lambda i: (i, 0)),
                  pl.BlockSpec((1, LANES), lambda i: (0, 0)), pl.BlockSpec((1, LANES), lambda i: (0, 0))],
        out_specs=[pl.BlockSpec((ts, A_WIDTH), lambda i: (i, 0)), pl.BlockSpec((ts, A_WIDTH), lambda i: (i, 0)),
                   pl.BlockSpec((ts, wd), lambda i: (i, 0))],
        out_shape=[out, out, jax.ShapeDtypeStruct((t, wd), F32)],
        compiler_params=_cparams(("parallel",)),
        name="gates",
    )(ba, pad(a_log), pad(dt_bias))


def _delta_solve_kernel(q_ref, k_ref, v_ref, g_ref, b_ref, dec_ref, u_ref, wq_ref, kd_ref, at_ref, egl_ref,
                        *, ts, hp, group=8):
    ri = lax.broadcasted_iota(jnp.int32, (CHUNK, CHUNK), 0)
    ci = lax.broadcasted_iota(jnp.int32, (CHUNK, CHUNK), 1)
    strict = ri > ci
    ident = jnp.where(ri == ci, 1.0, 0.0).astype(F32)

    def load(n, h):
        rows = slice(n * CHUNK, (n + 1) * CHUNK)
        cols = slice(h * A_DK, (h + 1) * A_DK)
        c = dict(n=n, rows=rows, cols=cols, dcols=slice(h * CHUNK, (h + 1) * CHUNK))
        c["q"] = q_ref[rows, cols].astype(F32)
        c["k"] = k_ref[rows, cols].astype(F32)
        c["gb"] = g_ref[rows, cols]
        c["bt"] = b_ref[rows, cols]
        c["kb"] = c["k"] * c["bt"]
        c["eg"] = jnp.exp(c["gb"])
        return c

    chains = [(n, h) for n in range(ts // CHUNK) for h in range(hp)]
    for g0 in range(0, len(chains), group):
        cs = [load(n, h) for n, h in chains[g0:g0 + group]]
        for c in cs:
            qk_kk = _nt_dot(jnp.concatenate([c["q"], c["kb"]], axis=0).astype(BF16), c["k"].astype(BF16))
            dec = dec_ref[c["rows"], c["dcols"]]
            c["attn"] = qk_kk[:CHUNK] * dec
            c["a"] = jnp.where(strict, qk_kk[CHUNK:] * dec, 0.0)
        for c in cs:
            c["inv"] = ident - c["a"]
            c["p"] = _bdot(c["a"], c["a"])
        for _ in range(CHUNK_SHIFT - 2):
            for c in cs:
                y = _bdot(jnp.concatenate([c["inv"], c["p"]], axis=0), c["p"])
                c["inv"] = c["inv"] + y[:CHUNK]
                c["p"] = y[CHUNK:]
        for c in cs:
            c["inv"] = c["inv"] + _bdot(c["inv"], c["p"])
        for c in cs:
            v = v_ref[c["rows"], c["cols"]].astype(F32)
            c["uw"] = _bdot(c["inv"], jnp.concatenate([v * c["bt"], c["kb"] * c["eg"]], axis=1))
        for c in cs:
            n, rows, cols = c["n"], c["rows"], c["cols"]
            g_last = c["gb"][CHUNK - 1:CHUNK, :]
            u_ref[rows, cols] = c["uw"][:, :A_DV].astype(u_ref.dtype)
            wq_ref[2 * n * CHUNK:(2 * n + 1) * CHUNK, cols] = c["uw"][:, A_DV:].astype(wq_ref.dtype)
            wq_ref[(2 * n + 1) * CHUNK:(2 * n + 2) * CHUNK, cols] = (c["q"] * c["eg"]).astype(wq_ref.dtype)
            kd_ref[rows, cols] = (c["k"] * jnp.exp(g_last - c["gb"])).astype(kd_ref.dtype)
            at_ref[rows, c["dcols"]] = c["attn"].astype(at_ref.dtype)
            egl_ref[n * SUBLANES:(n + 1) * SUBLANES, cols] = jnp.broadcast_to(jnp.exp(g_last), (SUBLANES, A_DV))


def _delta_solve(qkv, gcum, beta, dec, ts=512, hp=4):
    t = qkv.shape[0]
    ts = _pick(t, ts)
    hw = hp * A_DK
    n_hb = A_WIDTH // hw
    spec = lambda off: pl.BlockSpec((ts, hw), lambda i, hb: (i, off * n_hb + hb))
    wide = jax.ShapeDtypeStruct((t, A_WIDTH), BF16)
    return pl.pallas_call(
        functools.partial(_delta_solve_kernel, ts=ts, hp=hp),
        grid=(t // ts, n_hb),
        in_specs=[spec(0), spec(1), spec(2), spec(0), spec(0),
                  pl.BlockSpec((ts, hp * CHUNK), lambda i, hb: (i, hb))],
        out_specs=[spec(0),
                   pl.BlockSpec((2 * ts, hw), lambda i, hb: (i, hb)),
                   spec(0),
                   pl.BlockSpec((ts, hp * CHUNK), lambda i, hb: (i, hb)),
                   pl.BlockSpec((ts // CHUNK * SUBLANES, hw), lambda i, hb: (i, hb))],
        out_shape=[wide, jax.ShapeDtypeStruct((2 * t, A_WIDTH), BF16), wide,
                   jax.ShapeDtypeStruct((t, A_HEADS * CHUNK), BF16),
                   jax.ShapeDtypeStruct((t // CHUNK * SUBLANES, A_WIDTH), F32)],
        compiler_params=_cparams(("parallel", "parallel")),
        name="delta_solve",
    )(qkv, qkv, qkv, gcum, beta, dec)


def _delta_rec_kernel(u_ref, wq_ref, kd_ref, at_ref, egl_ref, z_ref, gain_ref, o_ref, s_ref, *, ts):
    @pl.when(pl.program_id(1) == 0)
    def _():
        s_ref[...] = jnp.zeros_like(s_ref)

    gain = gain_ref[...]

    heads = range(A_HEADS)
    col = lambda h: slice(h * A_DK, (h + 1) * A_DK)
    for n in range(ts // CHUNK):
        rows = slice(n * CHUNK, (n + 1) * CHUNK)
        rows2 = slice(2 * n * CHUNK, (2 * n + 2) * CHUNK)
        s = [s_ref[h] for h in heads]
        ws = [jnp.dot(wq_ref[rows2, col(h)], s[h].astype(BF16), preferred_element_type=F32)
              for h in heads]
        v_new = [(u_ref[rows, col(h)].astype(F32) - ws[h][:CHUNK]).astype(BF16) for h in heads]
        o = [ws[h][CHUNK:] + jnp.dot(at_ref[rows, h * CHUNK:(h + 1) * CHUNK], v_new[h], preferred_element_type=F32)
             for h in heads]
        for h in heads:
            eg_last = egl_ref[n * SUBLANES:n * SUBLANES + 1, col(h)]
            s_ref[h] = s[h] * eg_last + _tn_dot(kd_ref[rows, col(h)], v_new[h])
        for h in heads:
            on = o[h] * lax.rsqrt(jnp.mean(o[h] * o[h], axis=-1, keepdims=True) + EPS) * gain
            z = z_ref[rows, col(h)].astype(F32)
            o_ref[rows, col(h)] = (on * (z * _sigmoid(z))).astype(o_ref.dtype)


def _delta_rec(u, wq, kd, at, egl, z, out_gain, batch, seq, ts=512):
    t = u.shape[0]
    ts = _pick(seq, ts)
    n_s = seq // ts
    row = lambda b, i: (b * n_s + i, 0)
    return pl.pallas_call(
        functools.partial(_delta_rec_kernel, ts=ts),
        grid=(batch, n_s),
        in_specs=[pl.BlockSpec((ts, A_WIDTH), row), pl.BlockSpec((2 * ts, A_WIDTH), row),
                  pl.BlockSpec((ts, A_WIDTH), row), pl.BlockSpec((ts, A_HEADS * CHUNK), row),
                  pl.BlockSpec((ts // CHUNK * SUBLANES, A_WIDTH), row), pl.BlockSpec((ts, A_WIDTH), row),
                  pl.BlockSpec((1, A_DV), lambda b, i: (0, 0))],
        out_specs=pl.BlockSpec((ts, A_WIDTH), row),
        out_shape=jax.ShapeDtypeStruct((t, A_WIDTH), BF16),
        scratch_shapes=[pltpu.VMEM((A_HEADS, A_DK, A_DV), F32)],
        compiler_params=_cparams(("parallel", "arbitrary")),
        name="delta_rec",
    )(u, wq, kd, at, egl, z, out_gain.reshape(1, A_DV).astype(F32))


def _gmlp_kernel(u_ref, v_ref, gain_ref, w_ref, bias_ref, o_ref, *, nb):
    ri = lax.broadcasted_iota(jnp.int32, (GMLP_CHUNK, GMLP_CHUNK), 0)
    ci = lax.broadcasted_iota(jnp.int32, (GMLP_CHUNK, GMLP_CHUNK), 1)
    tril = ri >= ci
    gain = gain_ref[...]
    for blk in range(nb):
        rows = slice(blk * GMLP_CHUNK, (blk + 1) * GMLP_CHUNK)
        u = jax.nn.gelu(u_ref[rows, :].astype(F32))
        v = jax.nn.gelu(v_ref[rows, :].astype(F32))
        vn = (v * lax.rsqrt(jnp.mean(v * v, axis=-1, keepdims=True) + EPS) * gain).astype(BF16)
        for g in range(GMLP_GROUPS):
            cols = slice(g * GMLP_GDIM, (g + 1) * GMLP_GDIM)
            w = jnp.where(tril, w_ref[g], 0.0).astype(BF16)
            mixed = jnp.dot(w, vn[:, cols], preferred_element_type=F32) + bias_ref[:, cols]
            o_ref[rows, cols] = (u[:, cols] * mixed).astype(o_ref.dtype)


def _gmlp(uv, norm_gain, w_spatial, b_spatial, nb=8):
    t = uv.shape[0]
    tm = nb * GMLP_CHUNK
    bias = jnp.repeat(b_spatial.T.astype(F32), GMLP_GDIM, axis=1)
    return pl.pallas_call(
        functools.partial(_gmlp_kernel, nb=nb),
        grid=(t // tm,),
        in_specs=[pl.BlockSpec((tm, GMLP_WIDTH), lambda i: (i, 0)),
                  pl.BlockSpec((tm, GMLP_WIDTH), lambda i: (i, 1)),
                  pl.BlockSpec((1, GMLP_WIDTH), lambda i: (0, 0)),
                  pl.BlockSpec((GMLP_GROUPS, GMLP_CHUNK, GMLP_CHUNK), lambda i: (0, 0, 0)),
                  pl.BlockSpec((GMLP_CHUNK, GMLP_WIDTH), lambda i: (0, 0))],
        out_specs=pl.BlockSpec((tm, GMLP_WIDTH), lambda i: (i, 0)),
        out_shape=jax.ShapeDtypeStruct((t, GMLP_WIDTH), BF16),
        compiler_params=_cparams(("parallel",)),
        name="gmlp",
    )(uv, uv, norm_gain.reshape(1, GMLP_WIDTH).astype(F32), w_spatial.astype(F32), bias)


def _band_kernel(q_ref, kp_ref, kc_ref, vp_ref, vc_ref, qg_ref, kg_ref, bias_ref, o_ref, *, tq):
    i = pl.program_id(2)

    def norm(x, gain):
        x = x.astype(F32)
        return x * lax.rsqrt(jnp.mean(x * x, axis=-1, keepdims=True) + EPS) * gain

    qn = norm(q_ref[...], qg_ref[...]).astype(BF16)
    kcat = jnp.concatenate([norm(kp_ref[...], kg_ref[...]), norm(kc_ref[...], kg_ref[...])], axis=0).astype(BF16)
    vcat = jnp.concatenate([vp_ref[...], vc_ref[...]], axis=0).astype(BF16)
    bias = bias_ref[0]
    pad = LEFT_CHUNKS * CHUNK
    kpos = lax.broadcasted_iota(jnp.int32, (CHUNK, BAND), 1)
    chunks = range(tq // CHUNK)
    lo = [c * CHUNK + (tq - pad) for c in chunks]
    s = [_nt_dot(qn[c * CHUNK:(c + 1) * CHUNK], kcat[lo[c]:lo[c] + BAND]) for c in chunks]
    p = []
    for c in chunks:
        sc = s[c] * (C_DH ** -0.5) + bias
        valid = (i > 0) | (kpos + c * CHUNK >= pad)
        sc = jnp.where(valid, sc, NEG_BIG)
        e = jnp.exp(sc - jnp.max(sc, axis=-1, keepdims=True))
        p.append((e / jnp.sum(e, axis=-1, keepdims=True)).astype(BF16))
    o = [jnp.dot(p[c], vcat[lo[c]:lo[c] + BAND], preferred_element_type=F32) for c in chunks]
    for c in chunks:
        o_ref[c * CHUNK:(c + 1) * CHUNK, :] = o[c].astype(o_ref.dtype)


def _band_bias(rel_bias):
    diag = np.arange(-(CHUNK - 1), BAND)
    idx = np.clip(LEFT_CHUNKS * CHUNK - diag, -MAX_REL, MAX_REL) + MAX_REL
    vec = rel_bias.astype(F32)[:, idx]
    return jnp.stack([vec[:, CHUNK - 1 - i:CHUNK - 1 - i + BAND] for i in range(CHUNK)], axis=1)


def _band_attention(qkv, q_gain, k_gain, rel_bias, batch, seq):
    t = qkv.shape[0]
    tq = LEFT_CHUNKS * CHUNK
    n_s = seq // tq
    bias = _band_bias(rel_bias)
    cur = lambda off: pl.BlockSpec((tq, C_DH), lambda b, h, i: (b * n_s + i, off * C_HEADS + h))
    prv = lambda off: pl.BlockSpec((tq, C_DH), lambda b, h, i: (b * n_s + jnp.maximum(i - 1, 0), off * C_HEADS + h))
    vec = pl.BlockSpec((1, C_DH), lambda b, h, i: (0, 0))
    return pl.pallas_call(
        functools.partial(_band_kernel, tq=tq),
        grid=(batch, C_HEADS, n_s),
        in_specs=[cur(0), prv(1), cur(1), prv(2), cur(2), vec, vec,
                  pl.BlockSpec((1, CHUNK, BAND), lambda b, h, i: (h, 0, 0))],
        out_specs=pl.BlockSpec((tq, C_DH), lambda b, h, i: (b * n_s + i, h)),
        out_shape=jax.ShapeDtypeStruct((t, C_WIDTH), BF16),
        compiler_params=_cparams(("parallel", "parallel", "parallel")),
        name="band_attention",
    )(qkv, qkv, qkv, qkv, qkv, q_gain.reshape(1, C_DH).astype(F32), k_gain.reshape(1, C_DH).astype(F32), bias)


def _merge_kernel(ya_ref, yb_ref, yc_ref, pa_ref, pb_ref, pc_ref, ga_ref, gb_ref, gc_ref, o_ref,
                  wa_ref, wb_ref, wc_ref):
    @pl.when(pl.program_id(1) == 0)
    def _():
        for p_ref, w_ref in ((pa_ref, wa_ref), (pb_ref, wb_ref), (pc_ref, wc_ref)):
            w_ref[...] = p_ref[...].astype(w_ref.dtype)

    def branch(y_ref, w_ref, g_ref):
        gate = _sigmoid(g_ref[...].astype(F32))
        return gate * jnp.dot(y_ref[...], w_ref[...], preferred_element_type=F32)

    merged = branch(ya_ref, wa_ref, ga_ref) + branch(yb_ref, wb_ref, gb_ref) + branch(yc_ref, wc_ref, gc_ref)
    o_ref[...] = merged.astype(o_ref.dtype)


def _merge(ya, yb, yc, pa, pb, pc, layer, gate, tm=1024, tn=512):
    t, k = ya.shape
    d = pa.shape[2]
    tm, tn = _pick(t, tm), _pick(d, tn)
    nd = d // tn
    ysp = pl.BlockSpec((tm, k), lambda j, i: (i, 0))
    psp = pl.BlockSpec((None, k, tn), lambda j, i: (layer, 0, j))
    gsp = lambda br: pl.BlockSpec((tm, tn), lambda j, i: (i, br * nd + j))
    return pl.pallas_call(
        _merge_kernel,
        grid=(nd, t // tm),
        in_specs=[ysp, ysp, ysp, psp, psp, psp, gsp(0), gsp(1), gsp(2)],
        out_specs=pl.BlockSpec((tm, tn), lambda j, i: (i, j)),
        out_shape=jax.ShapeDtypeStruct((t, d), BF16),
        scratch_shapes=[pltpu.VMEM((k, tn), BF16)] * 3,
        compiler_params=_cparams(("parallel", "arbitrary")),
        name="merge",
    )(ya, yb, yc, pa, pb, pc, gate, gate, gate)


def _top_values(x, k, out_ref, want_rank=False):
    cur = x
    rank = jnp.full(x.shape, float(k), F32) if want_rank else None
    for r in range(k):
        m = jnp.max(cur, axis=0, keepdims=True)
        out_ref[r:r + 1, :] = m
        hit = cur == m
        if want_rank:
            rank = jnp.where(hit, float(r), rank)
        if r + 1 < k:
            cur = jnp.where(hit, NEG_BIG, cur)
    return rank


def _peer_select_kernel(q_ref, keys_ref, cnt_ref, f1_ref, r2_ref, e2_ref, top_ref, *, tt):
    k = PEER_TOPK

    def head(h, sub):
        a_ref, b_ref, c_ref = top_ref.at[sub, 0], top_ref.at[sub, 1], top_ref.at[sub, 2]
        qh = q_ref[:, pl.ds(pl.multiple_of(h * PEER_QDIM, PEER_QDIM), PEER_QDIM)]
        s1 = _nt_dot(keys_ref[h, 0], qh[:, :PEER_QHALF], precision=HIGHEST)
        s2 = _nt_dot(keys_ref[h, 1], qh[:, PEER_QHALF:], precision=HIGHEST)
        _top_values(s1, k, a_ref)
        rank2 = _top_values(s2, k, b_ref, want_rank=True)
        av = a_ref[...]
        bv = b_ref[...]
        half = k // 2
        cand = jnp.concatenate([av[0:1] + bv] + [av[r:r + 1] + bv[:half] for r in range(1, half)]
                               + [av[half:] + bv[0:1]], axis=0)
        _top_values(cand, k, c_ref)
        cv = c_ref[...]
        z = jnp.sum(jnp.exp(cv - cv[0:1, :]), axis=0, keepdims=True)
        tau = cv[k - 1:k, :]
        cnt = jnp.zeros(s1.shape, F32)
        for c in range(half):
            ok = av + bv[c:c + 1] >= tau
            thr = jnp.min(jnp.where(ok, av, -NEG_BIG), axis=0, keepdims=True)
            cnt = cnt + jnp.where(s1 >= thr, 1.0, 0.0)
        n_hi = jnp.sum(jnp.where(av[0:1] + bv[half:] >= tau, 1.0, 0.0), axis=0, keepdims=True)
        cnt = cnt + jnp.where(s1 >= av[0:1], n_hi, 0.0)
        cnt_ref[:, pl.ds(h, 1), :] = cnt[:, None, :]
        f1_ref[:, pl.ds(h, 1), :] = (jnp.exp(s1 - av[0:1]) / z)[:, None, :]
        rank2 = rank2.astype(r2_ref.dtype)
        e2 = jnp.exp(s2 - bv[0:1]).astype(e2_ref.dtype)
        rb = 2 * SUBLANES
        for g in range(PEER_NKEYS // rb):
            for tl in range(tt // LANES):
                r2_ref[h * (PEER_NKEYS // rb) + g, tl] = rank2[g * rb:(g + 1) * rb, tl * LANES:(tl + 1) * LANES]
                e2_ref[h * (PEER_NKEYS // rb) + g, tl] = e2[g * rb:(g + 1) * rb, tl * LANES:(tl + 1) * LANES]

    def pair(p, carry):
        head(2 * p, 0)
        head(2 * p + 1, 1)
        return carry

    lax.fori_loop(0, PEER_HEADS // 2, pair, 0)


def _peer_select(q, keys, tt=256):
    t = q.shape[0]
    tt = _pick(t, tt)
    shape = (PEER_NKEYS, PEER_HEADS, t)
    bspec = pl.BlockSpec((PEER_NKEYS, PEER_HEADS, tt), lambda i: (0, 0, i))
    n_slab = PEER_HEADS * PEER_NKEYS // (2 * SUBLANES)
    flat = (n_slab, t // LANES, 2 * SUBLANES, LANES)
    flat_spec = pl.BlockSpec((n_slab, tt // LANES, 2 * SUBLANES, LANES), lambda i: (0, i, 0, 0))
    return pl.pallas_call(
        functools.partial(_peer_select_kernel, tt=tt),
        grid=(t // tt,),
        in_specs=[pl.BlockSpec((tt, PEER_HEADS * PEER_QDIM), lambda i: (i, 0)),
                  pl.BlockSpec((PEER_HEADS, 2, PEER_NKEYS, PEER_QHALF), lambda i: (0, 0, 0, 0))],
        out_specs=[bspec, bspec, flat_spec, flat_spec],
        out_shape=[jax.ShapeDtypeStruct(shape, F32), jax.ShapeDtypeStruct(shape, F32),
                   jax.ShapeDtypeStruct(flat, BF16), jax.ShapeDtypeStruct(flat, BF16)],
        scratch_shapes=[pltpu.VMEM((2, 3, PEER_TOPK, tt), F32)],
        compiler_params=_cparams(("parallel",)),
        name="peer_select",
    )(q, keys.astype(F32))


def _peer_dense_kernel(hn_ref, u_ref, vt_ref, cnt_ref, f1_ref, r2_ref, e2_ref, x_ref, o_ref,
                       acc_ref, ht0_ref, ht1_ref, g0_ref, g1_ref, *, tt, ec, nc, d):
    s = pl.program_id(0)
    c_out = lax.rem(jnp.maximum(s - 2, 0), nc)

    @pl.when(s == 0)
    def _():
        for ref in (ht0_ref, ht1_ref, g0_ref, g1_ref):
            ref[...] = jnp.zeros_like(ref)

    @pl.when(c_out == 0)
    def _():
        acc_ref[...] = jnp.zeros_like(acc_ref)

    nk = PEER_NKEYS

    def stages(ht_w, ht_r, g_w, g_r):
        halves = [slice(0, tt // 2), slice(tt // 2, tt)]

        def stage_a(hs):
            ht_w[:, hs] = _nt_dot(u_ref[...], hn_ref[hs, :]).astype(ht_w.dtype)

        def stage_c(hs):
            acc_ref[:, hs] += jnp.dot(vt_ref[...], g_r[:, hs], preferred_element_type=F32)

        mxu_pieces = [functools.partial(stage_a, halves[0]), functools.partial(stage_a, halves[1]),
                      functools.partial(stage_c, halves[0]), functools.partial(stage_c, halves[1])]
        rb = 2 * SUBLANES
        tiles = [(ii, tg) for ii in range(ec // nk) for tg in range(tt // LANES)]
        per_piece = len(tiles) // len(mxu_pieces)
        for t_idx, (ii, tg) in enumerate(tiles):
            if t_idx % per_piece == 0:
                mxu_pieces[t_idx // per_piece]()
            lanes = slice(tg * LANES, (tg + 1) * LANES)
            bcast = lambda ref, h: jnp.broadcast_to(ref[ii, h:h + 1, lanes], (rb, LANES)).astype(BF16)
            cnt = [bcast(cnt_ref, h) for h in range(PEER_HEADS)]
            f1 = [bcast(f1_ref, h) for h in range(PEER_HEADS)]
            for j0 in range(0, nk, rb):
                wsel = jnp.zeros((rb, LANES), BF16)
                for h in range(PEER_HEADS):
                    slab = (h * nk + j0) // rb
                    picked = jnp.clip(cnt[h] - r2_ref[slab, tg], 0, 1)
                    wsel = wsel + picked * (f1[h] * e2_ref[slab, tg])
                rows = slice(ii * nk + j0, ii * nk + j0 + rb)
                g_w[rows, lanes] = wsel * jax.nn.gelu(ht_r[rows, lanes])

    parity = lax.rem(s, 2)
    pl.when(parity == 0)(functools.partial(stages, ht0_ref, ht1_ref, g1_ref, g0_ref))
    pl.when(parity == 1)(functools.partial(stages, ht1_ref, ht0_ref, g0_ref, g1_ref))

    @pl.when((c_out == nc - 1) & (s >= 2))
    def _():
        step = 512
        for d0 in range(0, d, step):
            o_ref[:, d0:d0 + step] = x_ref[:, d0:d0 + step] + acc_ref[d0:d0 + step, :].T


def _peer_dense(hn, u_tab, vt_tab, layer, cntr, f1r, r2, e2, x, tt=512, ec=1024):
    t, d = hn.shape
    e = u_tab.shape[1]
    tt, ec = _pick(t, tt), _pick(e, ec)
    ni = ec // PEER_NKEYS
    nc = e // ec
    n_steps = (t // tt) * nc
    n_slab = r2.shape[0]
    tile = lambda s, lag: jnp.clip(s - lag, 0, n_steps - 1) // nc
    chunk = lambda s, lag: jnp.clip(s - lag, 0, n_steps - 1) % nc
    once = dict(pipeline_mode=pl.Buffered(1))
    return pl.pallas_call(
        functools.partial(_peer_dense_kernel, tt=tt, ec=ec, nc=nc, d=d),
        grid=(n_steps + 2,),
        in_specs=[pl.BlockSpec((tt, d), lambda s: (tile(s, 0), 0), **once),
                  pl.BlockSpec((None, ec, d), lambda s: (layer, chunk(s, 0), 0)),
                  pl.BlockSpec((None, d, ec), lambda s: (layer, 0, chunk(s, 2))),
                  pl.BlockSpec((ni, PEER_HEADS, tt), lambda s: (chunk(s, 1), 0, tile(s, 1))),
                  pl.BlockSpec((ni, PEER_HEADS, tt), lambda s: (chunk(s, 1), 0, tile(s, 1))),
                  pl.BlockSpec((n_slab, tt // LANES, 2 * SUBLANES, LANES), lambda s: (0, tile(s, 1), 0, 0)),
                  pl.BlockSpec((n_slab, tt // LANES, 2 * SUBLANES, LANES), lambda s: (0, tile(s, 1), 0, 0)),
                  pl.BlockSpec((tt, d), lambda s: (tile(s, 2), 0), **once)],
        out_specs=pl.BlockSpec((tt, d), lambda s: (tile(s, 2), 0)),
        out_shape=jax.ShapeDtypeStruct((t, d), F32),
        scratch_shapes=[pltpu.VMEM((d, tt), F32)] + [pltpu.VMEM((ec, tt), BF16)] * 4,
        compiler_params=_cparams(("arbitrary",)),
        name="peer_dense",
    )(hn, u_tab, vt_tab, cntr, f1r, r2, e2, x)


def _layer(x, batch, seq, layer, w_in, conv_w, a_log, dt_bias, a_out_gain, gmlp_norm, w_spatial, b_spatial,
           c_q_gain, c_k_gain, rel_bias, p_a, p_b, p_c, w_out, norm_mix, norm_ffn,
           peer_wq, peer_keys, peer_u, peer_vt):
    o_qkv_a = 0
    o_z = o_qkv_a + 3 * A_WIDTH
    o_beta = o_z + A_WIDTH
    o_uv = o_beta + 2 * A_HEADS
    o_qkv_c = o_uv + 2 * GMLP_WIDTH
    o_gate = o_qkv_c + 3 * C_WIDTH
    d_model = x.shape[1]

    h = _rmsnorm(x, norm_mix)
    proj = lambda col0, n, dtype, name: _matmul(h, w_in, layer, col0, n, dtype, w_is_nk=True, name=name)
    qkv_a = proj(o_qkv_a, 3 * A_WIDTH, BF16, "proj_qkv_a")
    z_a = proj(o_z, A_WIDTH, BF16, "proj_z_a")
    ba = proj(o_beta, LANES, F32, "proj_beta_alpha")
    uv_b = proj(o_uv, 2 * GMLP_WIDTH, BF16, "proj_uv_b")
    qkv_c = proj(o_qkv_c, 3 * C_WIDTH, BF16, "proj_qkv_c")
    gate = proj(o_gate, N_BRANCH * d_model, BF16, "proj_gate")

    qkv_prep = _conv_prep(qkv_a, conv_w.astype(F32), batch, seq)
    gcum, beta, dec = _gates(ba, a_log, dt_bias)
    u, wq, kd, at, egl = _delta_solve(qkv_prep, gcum, beta, dec)
    y_a = _delta_rec(u, wq, kd, at, egl, z_a, a_out_gain, batch, seq)
    y_b = _gmlp(uv_b, gmlp_norm, w_spatial, b_spatial)
    y_c = _band_attention(qkv_c, c_q_gain, c_k_gain, rel_bias, batch, seq)

    merged = _merge(y_a, y_b, y_c, p_a, p_b, p_c, layer, gate)
    x = _matmul(merged, w_out, layer, 0, d_model, F32, residual=x, name="out_proj")

    hn = _rmsnorm(x, norm_ffn)
    q = _matmul(hn, peer_wq, layer, 0, peer_wq.shape[2], F32, name="peer_query")
    cnt, f1, r2, e2 = _peer_select(q, peer_keys)
    return _peer_dense(hn, peer_u, peer_vt, layer, cnt, f1, r2, e2, x)


def kernel(x, w_in, conv_w, a_log, dt_bias, a_out_gain, gmlp_norm, w_spatial, b_spatial, c_q_gain, c_k_gain,
           rel_bias, p_a, p_b, p_c, w_out, norm_mix, norm_ffn, peer_wq, peer_keys, peer_u, peer_v):
    batch, seq, d_model = x.shape
    xt = x.reshape(batch * seq, d_model)
    w_in_t = jnp.swapaxes(w_in, 1, 2)
    peer_u16 = peer_u.astype(BF16)
    peer_vt16 = jnp.swapaxes(peer_v.astype(BF16), 1, 2)
    for l in range(w_in.shape[0]):
        xt = _layer(xt, batch, seq, l, w_in_t, conv_w[l], a_log[l], dt_bias[l], a_out_gain[l], gmlp_norm[l],
                    w_spatial[l], b_spatial[l], c_q_gain[l], c_k_gain[l], rel_bias[l], p_a, p_b, p_c,
                    w_out, norm_mix[l], norm_ffn[l], peer_wq, peer_keys[l], peer_u16, peer_vt16)
    return xt.reshape(batch, seq, d_model)
```

```python
import functools

import jax
import jax.numpy as jnp
import numpy as np
from jax import lax
from jax.experimental import pallas as pl
from jax.experimental.pallas import tpu as pltpu

F32 = jnp.float32
BF16 = jnp.bfloat16
HIGHEST = lax.Precision.HIGHEST

CHUNK = 64
EPS = 1e-6
A_HEADS = 8
A_DK = 128
A_DV = 128
A_WIDTH = A_HEADS * A_DV
CONV_W = 4
GMLP_CHUNK = 128
GMLP_GROUPS = 8
GMLP_GDIM = 128
GMLP_WIDTH = GMLP_GROUPS * GMLP_GDIM
C_HEADS = 8
C_DH = 128
C_WIDTH = C_HEADS * C_DH
LEFT_CHUNKS = 8
BAND = (LEFT_CHUNKS + 1) * CHUNK
MAX_REL = 128
N_BRANCH = 3
PEER_HEADS = 8
PEER_NKEYS = 128
PEER_TOPK = 16
PEER_QDIM = 256
PEER_QHALF = PEER_QDIM // 2

LANES = 128
SUBLANES = 8
NEG_BIG = -1e30
VMEM_LIMIT = 56 * 1024 * 1024
CHUNK_SHIFT = CHUNK.bit_length() - 1


def _cparams(sem, vmem_limit=VMEM_LIMIT, flags=None):
    return pltpu.CompilerParams(dimension_semantics=sem, vmem_limit_bytes=vmem_limit, flags=flags)


def _nt_dot(a, b, precision=None):
    return lax.dot_general(a, b, (((1,), (1,)), ((), ())), precision=precision,
                           preferred_element_type=F32)


def _tn_dot(a, b, precision=None):
    return lax.dot_general(a, b, (((0,), (0,)), ((), ())), precision=precision,
                           preferred_element_type=F32)


def _bdot(a, b):
    return jnp.dot(a.astype(BF16), b.astype(BF16), preferred_element_type=F32)


def _sigmoid(x):
    return 0.5 * jnp.tanh(0.5 * x) + 0.5


def _pick(n, pref):
    t = min(pref, n)
    while n % t:
        t -= LANES if t > LANES else 8
    return t


def _rmsnorm_kernel(x_ref, g_ref, o_ref):
    x = x_ref[...]
    ms = jnp.mean(x * x, axis=-1, keepdims=True)
    o_ref[...] = (x * lax.rsqrt(ms + EPS) * g_ref[...]).astype(o_ref.dtype)


def _rmsnorm(x, gain, out_dtype=BF16, tm=512):
    m, d = x.shape
    tm = _pick(m, tm)
    return pl.pallas_call(
        _rmsnorm_kernel,
        grid=(m // tm,),
        in_specs=[pl.BlockSpec((tm, d), lambda i: (i, 0)), pl.BlockSpec((1, d), lambda i: (0, 0))],
        out_specs=pl.BlockSpec((tm, d), lambda i: (i, 0)),
        out_shape=jax.ShapeDtypeStruct((m, d), out_dtype),
        compiler_params=_cparams(("parallel",)),
        name="rmsnorm",
    )(x, gain.reshape(1, d))


def _mm_w32_kernel(*refs, shift, has_res, w_is_nk):
    refs = list(refs)
    a_ref, w_ref = refs[:2]
    tail_ref = refs[2] if shift else None
    rest = refs[3:] if shift else refs[2:]
    r_ref = rest[0] if has_res else None
    o_ref, wb_ref = rest[-2:]

    @pl.when(pl.program_id(1) == 0)
    def _():
        w = w_ref[...]
        if shift:
            w = jnp.concatenate([w[shift:], tail_ref[:shift]], axis=0)
        wb_ref[...] = w.astype(wb_ref.dtype)

    if w_is_nk:
        acc = _nt_dot(a_ref[...], wb_ref[...])
    else:
        acc = jnp.dot(a_ref[...], wb_ref[...], preferred_element_type=F32)
    if has_res:
        acc = r_ref[...] + acc
    o_ref[...] = acc.astype(o_ref.dtype)


def _matmul(a, w, layer, col0, n, out_dtype, residual=None, w_is_nk=False, tm=1024, tn=1024, name="matmul"):
    m, k = a.shape
    tm, tn = _pick(m, tm), _pick(n, tn)
    shift = col0 % LANES
    base = col0 - shift
    assert base % tn == 0 and shift % SUBLANES == 0 and (w_is_nk or not shift), (col0, tn)
    in_specs = [pl.BlockSpec((tm, k), lambda j, i: (i, 0))]
    if w_is_nk:
        in_specs.append(pl.BlockSpec((None, tn, k), lambda j, i: (layer, base // tn + j, 0)))
        wb_shape = (tn, k)
    else:
        in_specs.append(pl.BlockSpec((None, k, tn), lambda j, i: (layer, 0, base // tn + j)))
        wb_shape = (k, tn)
    args = [a, w]
    if shift:
        in_specs.append(pl.BlockSpec((None, LANES, k), lambda j, i: (layer, (base + (j + 1) * tn) // LANES, 0)))
        args.append(w)
    if residual is not None:
        in_specs.append(pl.BlockSpec((tm, tn), lambda j, i: (i, j)))
        args.append(residual)
    return pl.pallas_call(
        functools.partial(_mm_w32_kernel, shift=shift, has_res=residual is not None, w_is_nk=w_is_nk),
        grid=(n // tn, m // tm),
        in_specs=in_specs,
        out_specs=pl.BlockSpec((tm, tn), lambda j, i: (i, j)),
        out_shape=jax.ShapeDtypeStruct((m, n), out_dtype),
        scratch_shapes=[pltpu.VMEM(wb_shape, BF16)],
        compiler_params=_cparams(("parallel", "arbitrary")),
        name=name,
    )(*args)


def _conv_prep_kernel(cur_ref, prev_ref, w_ref, o_ref, ext_ref, *, ts, tc, prev_rows, n_qk_tiles, n_q_tiles):
    i = pl.program_id(1)
    c = pl.program_id(2)
    prev = prev_ref[...].astype(F32)
    prev = jnp.where(i == 0, 0.0, prev)
    cur = cur_ref[...].astype(F32)
    ext_ref[0:prev_rows, :] = prev
    ext_ref[prev_rows:prev_rows + ts, :] = cur
    w = w_ref[...]
    acc = cur * w[CONV_W - 1:CONV_W, :]
    for d in range(1, CONV_W):
        acc = acc + ext_ref[prev_rows - d:prev_rows - d + ts, :] * w[CONV_W - 1 - d:CONV_W - d, :]
    y = acc * _sigmoid(acc)
    parts = []
    for h in range(tc // A_DK):
        yh = y[:, h * A_DK:(h + 1) * A_DK]
        parts.append(yh * lax.rsqrt(jnp.sum(yh * yh, axis=-1, keepdims=True) + EPS))
    yn = jnp.concatenate(parts, axis=-1)
    scale = jnp.where(c < n_q_tiles, A_DK ** -0.5, 1.0).astype(F32)
    o_ref[...] = jnp.where(c < n_qk_tiles, yn * scale, y).astype(o_ref.dtype)


def _conv_prep(qkv, conv_w, batch, seq, ts=1024, tc=512):
    t, ch = qkv.shape
    ts = _pick(seq, ts)
    prev_rows = 16
    n_s = seq // ts
    kern = functools.partial(_conv_prep_kernel, ts=ts, tc=tc, prev_rows=prev_rows,
                             n_qk_tiles=2 * A_WIDTH // tc, n_q_tiles=A_WIDTH // tc)
    rpb = ts // prev_rows
    return pl.pallas_call(
        kern,
        grid=(batch, n_s, ch // tc),
        in_specs=[
            pl.BlockSpec((ts, tc), lambda b, i, c: (b * n_s + i, c)),
            pl.BlockSpec((prev_rows, tc), lambda b, i, c: (jnp.maximum((b * n_s + i) * rpb - 1, 0), c)),
            pl.BlockSpec((CONV_W, tc), lambda b, i, c: (0, c)),
        ],
        out_specs=pl.BlockSpec((ts, tc), lambda b, i, c: (b * n_s + i, c)),
        out_shape=jax.ShapeDtypeStruct((t, ch), BF16),
        scratch_shapes=[pltpu.VMEM((prev_rows + ts, tc), F32)],
        compiler_params=_cparams(("parallel", "parallel", "parallel")),
        name="conv_prep",
    )(qkv, qkv, conv_w)


def _split3(x):
    hi = x.astype(BF16)
    r = x - hi.astype(F32)
    mid = r.astype(BF16)
    return hi, mid, (r - mid.astype(F32)).astype(BF16)


def _gates_kernel(ba_ref, alog_ref, dtb_ref, g_ref, b_ref, dec_ref, *, ts):
    def hdot(a, b):
        if a.dtype == BF16:
            return sum(jnp.dot(a, p, preferred_element_type=F32) for p in _split3(b))
        return sum(jnp.dot(p, b, preferred_element_type=F32) for p in _split3(a))

    ba = ba_ref[...]
    lane = lax.broadcasted_iota(jnp.int32, (ts, LANES), 1)
    head_lane = lane < A_HEADS
    beta = jnp.where(head_lane, jax.nn.sigmoid(ba), 0.0)
    alpha = pltpu.roll(ba, LANES - A_HEADS, axis=1)
    g = -jnp.exp(alog_ref[...]) * jax.nn.softplus(alpha + dtb_ref[...])
    g = jnp.where(head_lane, g, 0.0)
    r = lax.broadcasted_iota(jnp.int32, (ts, ts), 0)
    c = lax.broadcasted_iota(jnp.int32, (ts, ts), 1)
    same_chunk = (r >> CHUNK_SHIFT) == (c >> CHUNK_SHIFT)
    tri = jnp.where((c <= r) & same_chunk, 1.0, 0.0).astype(BF16)
    gcum = hdot(tri, g)
    er = lax.broadcasted_iota(jnp.int32, (LANES, A_WIDTH), 0)
    ec = lax.broadcasted_iota(jnp.int32, (LANES, A_WIDTH), 1)
    spread = jnp.where(er == (ec >> (A_DV.bit_length() - 1)), 1.0, 0.0).astype(BF16)
    g_ref[...] = hdot(gcum, spread)
    b_ref[...] = hdot(beta, spread)
    wd = A_HEADS * CHUNK
    er = lax.broadcasted_iota(jnp.int32, (LANES, wd), 0)
    ec = lax.broadcasted_iota(jnp.int32, (LANES, wd), 1)
    gi = hdot(gcum, jnp.where(er == (ec >> CHUNK_SHIFT), 1.0, 0.0).astype(BF16))
    ipos = lax.broadcasted_iota(jnp.int32, (ts, wd), 0) & (CHUNK - 1)
    jpos = lax.broadcasted_iota(jnp.int32, (ts, wd), 1) & (CHUNK - 1)
    blk = jnp.where(same_chunk, 1.0, 0.0).astype(BF16)
    gj = hdot(blk, jnp.where(ipos == jpos, gi, 0.0))
    dec_ref[...] = jnp.exp(jnp.where(ipos >= jpos, gi - gj, NEG_BIG))


def _gates(ba, a_log, dt_bias, ts=256):
    t = ba.shape[0]
    ts = _pick(t, ts)
    pad = lambda v: jnp.pad(v.astype(F32), (0, LANES - A_HEADS)).reshape(1, LANES)
    out = jax.ShapeDtypeStruct((t, A_WIDTH), F32)
    wd = A_HEADS * CHUNK
    return pl.pallas_call(
        functools.partial(_gates_kernel, ts=ts),
        grid=(t // ts,),
        in_specs=[pl.BlockSpec((ts, LANES), lambda i: (i, 0)),
                  pl.BlockSpec((1, LANES), lambda i: (0, 0)), pl.BlockSpec((1, LANES), lambda i: (0, 0))],
        out_specs=[pl.BlockSpec((ts, A_WIDTH), lambda i: (i, 0)), pl.BlockSpec((ts, A_WIDTH), lambda i: (i, 0)),
                   pl.BlockSpec((ts, wd), lambda i: (i, 0))],
        out_shape=[out, out, jax.ShapeDtypeStruct((t, wd), F32)],
        compiler_params=_cparams(("parallel",)),
        name="gates",
    )(ba, pad(a_log), pad(dt_bias))


def _delta_solve_kernel(q_ref, k_ref, v_ref, g_ref, b_ref, dec_ref, u_ref, wq_ref, kd_ref, at_ref, egl_ref,
                        *, ts, hp, group=16):
    ri = lax.broadcasted_iota(jnp.int32, (CHUNK, CHUNK), 0)
    ci = lax.broadcasted_iota(jnp.int32, (CHUNK, CHUNK), 1)
    strict = ri > ci
    ident = jnp.where(ri == ci, 1.0, 0.0).astype(F32)

    def load(n, h):
        rows = slice(n * CHUNK, (n + 1) * CHUNK)
        cols = slice(h * A_DK, (h + 1) * A_DK)
        c = dict(n=n, rows=rows, cols=cols, dcols=slice(h * CHUNK, (h + 1) * CHUNK))
        c["q"] = q_ref[rows, cols].astype(F32)
        c["k"] = k_ref[rows, cols].astype(F32)
        c["gb"] = g_ref[rows, cols]
        c["bt"] = b_ref[rows, cols]
        c["kb"] = c["k"] * c["bt"]
        c["eg"] = jnp.exp(c["gb"])
        return c

    chains = [(n, h) for n in range(ts // CHUNK) for h in range(hp)]
    for g0 in range(0, len(chains), group):
        cs = [load(n, h) for n, h in chains[g0:g0 + group]]
        for c in cs:
            qk_kk = _nt_dot(jnp.concatenate([c["q"], c["kb"]], axis=0).astype(BF16), c["k"].astype(BF16))
            dec = dec_ref[c["rows"], c["dcols"]]
            c["attn"] = qk_kk[:CHUNK] * dec
            c["a"] = jnp.where(strict, qk_kk[CHUNK:] * dec, 0.0)
        for c in cs:
            c["inv"] = ident - c["a"]
            c["p"] = _bdot(c["a"], c["a"])
        for _ in range(CHUNK_SHIFT - 2):
            for c in cs:
                y = _bdot(jnp.concatenate([c["inv"], c["p"]], axis=0), c["p"])
                c["inv"] = c["inv"] + y[:CHUNK]
                c["p"] = y[CHUNK:]
        for c in cs:
            c["inv"] = c["inv"] + _bdot(c["inv"], c["p"])
        for c in cs:
            v = v_ref[c["rows"], c["cols"]].astype(F32)
            c["uw"] = _bdot(c["inv"], jnp.concatenate([v * c["bt"], c["kb"] * c["eg"]], axis=1))
        for c in cs:
            n, rows, cols = c["n"], c["rows"], c["cols"]
            g_last = c["gb"][CHUNK - 1:CHUNK, :]
            u_ref[rows, cols] = c["uw"][:, :A_DV].astype(u_ref.dtype)
            wq_ref[2 * n * CHUNK:(2 * n + 1) * CHUNK, cols] = c["uw"][:, A_DV:].astype(wq_ref.dtype)
            wq_ref[(2 * n + 1) * CHUNK:(2 * n + 2) * CHUNK, cols] = (c["q"] * c["eg"]).astype(wq_ref.dtype)
            kd_ref[rows, cols] = (c["k"] * jnp.exp(g_last - c["gb"])).astype(kd_ref.dtype)
            at_ref[rows, c["dcols"]] = c["attn"].astype(at_ref.dtype)
            egl_ref[n * SUBLANES:(n + 1) * SUBLANES, cols] = jnp.broadcast_to(jnp.exp(g_last), (SUBLANES, A_DV))


def _delta_solve(qkv, gcum, beta, dec, ts=512, hp=4):
    t = qkv.shape[0]
    ts = _pick(t, ts)
    hw = hp * A_DK
    n_hb = A_WIDTH // hw
    spec = lambda off: pl.BlockSpec((ts, hw), lambda i, hb: (i, off * n_hb + hb))
    wide = jax.ShapeDtypeStruct((t, A_WIDTH), BF16)
    return pl.pallas_call(
        functools.partial(_delta_solve_kernel, ts=ts, hp=hp),
        grid=(t // ts, n_hb),
        in_specs=[spec(0), spec(1), spec(2), spec(0), spec(0),
                  pl.BlockSpec((ts, hp * CHUNK), lambda i, hb: (i, hb))],
        out_specs=[spec(0),
                   pl.BlockSpec((2 * ts, hw), lambda i, hb: (i, hb)),
                   spec(0),
                   pl.BlockSpec((ts, hp * CHUNK), lambda i, hb: (i, hb)),
                   pl.BlockSpec((ts // CHUNK * SUBLANES, hw), lambda i, hb: (i, hb))],
        out_shape=[wide, jax.ShapeDtypeStruct((2 * t, A_WIDTH), BF16), wide,
                   jax.ShapeDtypeStruct((t, A_HEADS * CHUNK), BF16),
                   jax.ShapeDtypeStruct((t // CHUNK * SUBLANES, A_WIDTH), F32)],
        compiler_params=_cparams(("parallel", "parallel")),
        name="delta_solve",
    )(qkv, qkv, qkv, gcum, beta, dec)


def _delta_rec_kernel(u_ref, wq_ref, kd_ref, at_ref, egl_ref, z_ref, gain_ref, o_ref, s_ref, *, ts):
    @pl.when(pl.program_id(1) == 0)
    def _():
        s_ref[...] = jnp.zeros_like(s_ref)

    gain = gain_ref[...]

    heads = range(A_HEADS)
    col = lambda h: slice(h * A_DK, (h + 1) * A_DK)
    for n in range(ts // CHUNK):
        rows = slice(n * CHUNK, (n + 1) * CHUNK)
        rows2 = slice(2 * n * CHUNK, (2 * n + 2) * CHUNK)
        s = [s_ref[h] for h in heads]
        ws = [jnp.dot(wq_ref[rows2, col(h)], s[h].astype(BF16), preferred_element_type=F32)
              for h in heads]
        v_new = [(u_ref[rows, col(h)].astype(F32) - ws[h][:CHUNK]).astype(BF16) for h in heads]
        o = [ws[h][CHUNK:] + jnp.dot(at_ref[rows, h * CHUNK:(h + 1) * CHUNK], v_new[h], preferred_element_type=F32)
             for h in heads]
        for h in heads:
            eg_last = egl_ref[n * SUBLANES:n * SUBLANES + 1, col(h)]
            s_ref[h] = s[h] * eg_last + _tn_dot(kd_ref[rows, col(h)], v_new[h])
        for h in heads:
            on = o[h] * lax.rsqrt(jnp.mean(o[h] * o[h], axis=-1, keepdims=True) + EPS) * gain
            z = z_ref[rows, col(h)].astype(F32)
            o_ref[rows, col(h)] = (on * (z * _sigmoid(z))).astype(o_ref.dtype)


def _delta_rec(u, wq, kd, at, egl, z, out_gain, batch, seq, ts=512):
    t = u.shape[0]
    ts = _pick(seq, ts)
    n_s = seq // ts
    row = lambda b, i: (b * n_s + i, 0)
    return pl.pallas_call(
        functools.partial(_delta_rec_kernel, ts=ts),
        grid=(batch, n_s),
        in_specs=[pl.BlockSpec((ts, A_WIDTH), row), pl.BlockSpec((2 * ts, A_WIDTH), row),
                  pl.BlockSpec((ts, A_WIDTH), row), pl.BlockSpec((ts, A_HEADS * CHUNK), row),
                  pl.BlockSpec((ts // CHUNK * SUBLANES, A_WIDTH), row), pl.BlockSpec((ts, A_WIDTH), row),
                  pl.BlockSpec((1, A_DV), lambda b, i: (0, 0))],
        out_specs=pl.BlockSpec((ts, A_WIDTH), row),
        out_shape=jax.ShapeDtypeStruct((t, A_WIDTH), BF16),
        scratch_shapes=[pltpu.VMEM((A_HEADS, A_DK, A_DV), F32)],
        compiler_params=_cparams(("parallel", "arbitrary")),
        name="delta_rec",
    )(u, wq, kd, at, egl, z, out_gain.reshape(1, A_DV).astype(F32))


def _gmlp_kernel(u_ref, v_ref, gain_ref, w_ref, bias_ref, o_ref, *, nb):
    ri = lax.broadcasted_iota(jnp.int32, (GMLP_CHUNK, GMLP_CHUNK), 0)
    ci = lax.broadcasted_iota(jnp.int32, (GMLP_CHUNK, GMLP_CHUNK), 1)
    tril = ri >= ci
    gain = gain_ref[...]
    for blk in range(nb):
        rows = slice(blk * GMLP_CHUNK, (blk + 1) * GMLP_CHUNK)
        u = jax.nn.gelu(u_ref[rows, :].astype(F32))
        v = jax.nn.gelu(v_ref[rows, :].astype(F32))
        vn = (v * lax.rsqrt(jnp.mean(v * v, axis=-1, keepdims=True) + EPS) * gain).astype(BF16)
        for g in range(GMLP_GROUPS):
            cols = slice(g * GMLP_GDIM, (g + 1) * GMLP_GDIM)
            w = jnp.where(tril, w_ref[g], 0.0).astype(BF16)
            mixed = jnp.dot(w, vn[:, cols], preferred_element_type=F32) + bias_ref[:, cols]
            o_ref[rows, cols] = (u[:, cols] * mixed).astype(o_ref.dtype)


def _gmlp(uv, norm_gain, w_spatial, b_spatial, nb=8):
    t = uv.shape[0]
    tm = nb * GMLP_CHUNK
    bias = jnp.repeat(b_spatial.T.astype(F32), GMLP_GDIM, axis=1)
    return pl.pallas_call(
        functools.partial(_gmlp_kernel, nb=nb),
        grid=(t // tm,),
        in_specs=[pl.BlockSpec((tm, GMLP_WIDTH), lambda i: (i, 0)),
                  pl.BlockSpec((tm, GMLP_WIDTH), lambda i: (i, 1)),
                  pl.BlockSpec((1, GMLP_WIDTH), lambda i: (0, 0)),
                  pl.BlockSpec((GMLP_GROUPS, GMLP_CHUNK, GMLP_CHUNK), lambda i: (0, 0, 0)),
                  pl.BlockSpec((GMLP_CHUNK, GMLP_WIDTH), lambda i: (0, 0))],
        out_specs=pl.BlockSpec((tm, GMLP_WIDTH), lambda i: (i, 0)),
        out_shape=jax.ShapeDtypeStruct((t, GMLP_WIDTH), BF16),
        compiler_params=_cparams(("parallel",)),
        name="gmlp",
    )(uv, uv, norm_gain.reshape(1, GMLP_WIDTH).astype(F32), w_spatial.astype(F32), bias)


def _band_kernel(q_ref, kp_ref, kc_ref, vp_ref, vc_ref, qg_ref, kg_ref, bias_ref, o_ref, *, tq, hp):
    i = pl.program_id(2)

    def norm(x, gain):
        x = x.astype(F32)
        return x * lax.rsqrt(jnp.mean(x * x, axis=-1, keepdims=True) + EPS) * gain

    col = lambda h: slice(h * C_DH, (h + 1) * C_DH)
    heads = range(hp)
    qn = [norm(q_ref[:, col(h)], qg_ref[...]).astype(BF16) for h in heads]
    kcat = [jnp.concatenate([norm(kp_ref[:, col(h)], kg_ref[...]), norm(kc_ref[:, col(h)], kg_ref[...])],
                            axis=0).astype(BF16) for h in heads]
    vcat = [jnp.concatenate([vp_ref[:, col(h)], vc_ref[:, col(h)]], axis=0).astype(BF16) for h in heads]
    pad = LEFT_CHUNKS * CHUNK
    kpos = lax.broadcasted_iota(jnp.int32, (CHUNK, BAND), 1)
    lo = [c * CHUNK + (tq - pad) for c in range(tq // CHUNK)]
    pairs = [(h, c) for h in heads for c in range(tq // CHUNK)]
    s = [_nt_dot(qn[h][c * CHUNK:(c + 1) * CHUNK], kcat[h][lo[c]:lo[c] + BAND]) for h, c in pairs]
    p = []
    for (h, c), sc in zip(pairs, s):
        sc = sc * (C_DH ** -0.5) + bias_ref[h]
        valid = (i > 0) | (kpos + c * CHUNK >= pad)
        sc = jnp.where(valid, sc, NEG_BIG)
        e = jnp.exp(sc - jnp.max(sc, axis=-1, keepdims=True))
        p.append((e / jnp.sum(e, axis=-1, keepdims=True)).astype(BF16))
    o = [jnp.dot(pp, vcat[h][lo[c]:lo[c] + BAND], preferred_element_type=F32) for (h, c), pp in zip(pairs, p)]
    for (h, c), oo in zip(pairs, o):
        o_ref[c * CHUNK:(c + 1) * CHUNK, col(h)] = oo.astype(o_ref.dtype)


def _band_bias(rel_bias):
    diag = np.arange(-(CHUNK - 1), BAND)
    idx = np.clip(LEFT_CHUNKS * CHUNK - diag, -MAX_REL, MAX_REL) + MAX_REL
    vec = rel_bias.astype(F32)[:, idx]
    return jnp.stack([vec[:, CHUNK - 1 - i:CHUNK - 1 - i + BAND] for i in range(CHUNK)], axis=1)


def _band_attention(qkv, q_gain, k_gain, rel_bias, batch, seq, hp=2):
    t = qkv.shape[0]
    tq = LEFT_CHUNKS * CHUNK
    n_s = seq // tq
    n_hb = C_HEADS // hp
    hw = hp * C_DH
    bias = _band_bias(rel_bias)
    cur = lambda off: pl.BlockSpec((tq, hw), lambda b, h, i: (b * n_s + i, off * n_hb + h))
    prv = lambda off: pl.BlockSpec((tq, hw), lambda b, h, i: (b * n_s + jnp.maximum(i - 1, 0), off * n_hb + h))
    vec = pl.BlockSpec((1, C_DH), lambda b, h, i: (0, 0))
    return pl.pallas_call(
        functools.partial(_band_kernel, tq=tq, hp=hp),
        grid=(batch, n_hb, n_s),
        in_specs=[cur(0), prv(1), cur(1), prv(2), cur(2), vec, vec,
                  pl.BlockSpec((hp, CHUNK, BAND), lambda b, h, i: (h, 0, 0))],
        out_specs=pl.BlockSpec((tq, hw), lambda b, h, i: (b * n_s + i, h)),
        out_shape=jax.ShapeDtypeStruct((t, C_WIDTH), BF16),
        compiler_params=_cparams(("parallel", "parallel", "parallel")),
        name="band_attention",
    )(qkv, qkv, qkv, qkv, qkv, q_gain.reshape(1, C_DH).astype(F32), k_gain.reshape(1, C_DH).astype(F32), bias)


def _merge_kernel(ya_ref, yb_ref, yc_ref, pa_ref, pb_ref, pc_ref, ga_ref, gb_ref, gc_ref, o_ref,
                  wa_ref, wb_ref, wc_ref):
    @pl.when(pl.program_id(1) == 0)
    def _():
        for p_ref, w_ref in ((pa_ref, wa_ref), (pb_ref, wb_ref), (pc_ref, wc_ref)):
            w_ref[...] = p_ref[...].astype(w_ref.dtype)

    def branch(y_ref, w_ref, g_ref):
        gate = _sigmoid(g_ref[...].astype(F32))
        return gate * jnp.dot(y_ref[...], w_ref[...], preferred_element_type=F32)

    merged = branch(ya_ref, wa_ref, ga_ref) + branch(yb_ref, wb_ref, gb_ref) + branch(yc_ref, wc_ref, gc_ref)
    o_ref[...] = merged.astype(o_ref.dtype)


def _merge(ya, yb, yc, pa, pb, pc, layer, gate, tm=1024, tn=512):
    t, k = ya.shape
    d = pa.shape[2]
    tm, tn = _pick(t, tm), _pick(d, tn)
    nd = d // tn
    ysp = pl.BlockSpec((tm, k), lambda j, i: (i, 0))
    psp = pl.BlockSpec((None, k, tn), lambda j, i: (layer, 0, j))
    gsp = lambda br: pl.BlockSpec((tm, tn), lambda j, i: (i, br * nd + j))
    return pl.pallas_call(
        _merge_kernel,
        grid=(nd, t // tm),
        in_specs=[ysp, ysp, ysp, psp, psp, psp, gsp(0), gsp(1), gsp(2)],
        out_specs=pl.BlockSpec((tm, tn), lambda j, i: (i, j)),
        out_shape=jax.ShapeDtypeStruct((t, d), BF16),
        scratch_shapes=[pltpu.VMEM((k, tn), BF16)] * 3,
        compiler_params=_cparams(("parallel", "arbitrary")),
        name="merge",
    )(ya, yb, yc, pa, pb, pc, gate, gate, gate)


def _top_values(x, k, out_ref, want_rank=False):
    cur = x
    rank = jnp.full(x.shape, float(k), F32) if want_rank else None
    for r in range(k):
        m = jnp.max(cur, axis=0, keepdims=True)
        out_ref[r:r + 1, :] = m
        hit = cur == m
        if want_rank:
            rank = jnp.where(hit, float(r), rank)
        if r + 1 < k:
            cur = jnp.where(hit, NEG_BIG, cur)
    return rank


def _peer_select_kernel(q_ref, keys_ref, cnt_ref, f1_ref, r2_ref, e2_ref, top_ref, *, tt):
    k = PEER_TOPK

    def head(h, sub):
        a_ref, b_ref, c_ref = top_ref.at[sub, 0], top_ref.at[sub, 1], top_ref.at[sub, 2]
        qh = q_ref[:, pl.ds(pl.multiple_of(h * PEER_QDIM, PEER_QDIM), PEER_QDIM)]
        s1 = _nt_dot(keys_ref[h, 0], qh[:, :PEER_QHALF], precision=HIGHEST)
        s2 = _nt_dot(keys_ref[h, 1], qh[:, PEER_QHALF:], precision=HIGHEST)
        _top_values(s1, k, a_ref)
        rank2 = _top_values(s2, k, b_ref, want_rank=True)
        av = a_ref[...]
        bv = b_ref[...]
        half = k // 2
        cand = jnp.concatenate([av[0:1] + bv] + [av[r:r + 1] + bv[:half] for r in range(1, half)]
                               + [av[half:] + bv[0:1]], axis=0)
        _top_values(cand, k, c_ref)
        cv = c_ref[...]
        z = jnp.sum(jnp.exp(cv - cv[0:1, :]), axis=0, keepdims=True)
        tau = cv[k - 1:k, :]
        cnt = jnp.zeros(s1.shape, F32)
        for c in range(half):
            ok = av + bv[c:c + 1] >= tau
            thr = jnp.min(jnp.where(ok, av, -NEG_BIG), axis=0, keepdims=True)
            cnt = cnt + jnp.where(s1 >= thr, 1.0, 0.0)
        n_hi = jnp.sum(jnp.where(av[0:1] + bv[half:] >= tau, 1.0, 0.0), axis=0, keepdims=True)
        cnt = cnt + jnp.where(s1 >= av[0:1], n_hi, 0.0)
        cnt_ref[:, pl.ds(h, 1), :] = cnt[:, None, :]
        f1_ref[:, pl.ds(h, 1), :] = (jnp.exp(s1 - av[0:1]) / z)[:, None, :]
        rank2 = rank2.astype(r2_ref.dtype)
        e2 = jnp.exp(s2 - bv[0:1]).astype(e2_ref.dtype)
        rb = 2 * SUBLANES
        for g in range(PEER_NKEYS // rb):
            for tl in range(tt // LANES):
                r2_ref[h * (PEER_NKEYS // rb) + g, tl] = rank2[g * rb:(g + 1) * rb, tl * LANES:(tl + 1) * LANES]
                e2_ref[h * (PEER_NKEYS // rb) + g, tl] = e2[g * rb:(g + 1) * rb, tl * LANES:(tl + 1) * LANES]

    def pair(p, carry):
        head(2 * p, 0)
        head(2 * p + 1, 1)
        return carry

    lax.fori_loop(0, PEER_HEADS // 2, pair, 0)


def _peer_select(q, keys, tt=256):
    t = q.shape[0]
    tt = _pick(t, tt)
    shape = (PEER_NKEYS, PEER_HEADS, t)
    bspec = pl.BlockSpec((PEER_NKEYS, PEER_HEADS, tt), lambda i: (0, 0, i))
    n_slab = PEER_HEADS * PEER_NKEYS // (2 * SUBLANES)
    flat = (n_slab, t // LANES, 2 * SUBLANES, LANES)
    flat_spec = pl.BlockSpec((n_slab, tt // LANES, 2 * SUBLANES, LANES), lambda i: (0, i, 0, 0))
    return pl.pallas_call(
        functools.partial(_peer_select_kernel, tt=tt),
        grid=(t // tt,),
        in_specs=[pl.BlockSpec((tt, PEER_HEADS * PEER_QDIM), lambda i: (i, 0)),
                  pl.BlockSpec((PEER_HEADS, 2, PEER_NKEYS, PEER_QHALF), lambda i: (0, 0, 0, 0))],
        out_specs=[bspec, bspec, flat_spec, flat_spec],
        out_shape=[jax.ShapeDtypeStruct(shape, F32), jax.ShapeDtypeStruct(shape, F32),
                   jax.ShapeDtypeStruct(flat, BF16), jax.ShapeDtypeStruct(flat, BF16)],
        scratch_shapes=[pltpu.VMEM((2, 3, PEER_TOPK, tt), F32)],
        compiler_params=_cparams(("parallel",)),
        name="peer_select",
    )(q, keys.astype(F32))


def _peer_dense_kernel(hn_ref, u_ref, vt_ref, cnt_ref, f1_ref, r2_ref, e2_ref, x_ref, o_ref,
                       acc_ref, ht0_ref, ht1_ref, g0_ref, g1_ref, *, tt, ec, nc, d):
    s = pl.program_id(0)
    c_out = lax.rem(jnp.maximum(s - 2, 0), nc)

    @pl.when(s == 0)
    def _():
        for ref in (ht0_ref, ht1_ref, g0_ref, g1_ref):
            ref[...] = jnp.zeros_like(ref)

    @pl.when(c_out == 0)
    def _():
        acc_ref[...] = jnp.zeros_like(acc_ref)

    nk = PEER_NKEYS

    def stages(ht_w, ht_r, g_w, g_r):
        halves = [slice(0, tt // 2), slice(tt // 2, tt)]

        def stage_a(hs):
            ht_w[:, hs] = _nt_dot(u_ref[...], hn_ref[hs, :]).astype(ht_w.dtype)

        def stage_c(hs):
            acc_ref[:, hs] += jnp.dot(vt_ref[...], g_r[:, hs], preferred_element_type=F32)

        mxu_pieces = [functools.partial(stage_a, halves[0]), functools.partial(stage_a, halves[1]),
                      functools.partial(stage_c, halves[0]), functools.partial(stage_c, halves[1])]
        rb = 2 * SUBLANES
        tiles = [(ii, tg) for ii in range(ec // nk) for tg in range(tt // LANES)]
        per_piece = len(tiles) // len(mxu_pieces)
        for t_idx, (ii, tg) in enumerate(tiles):
            if t_idx % per_piece == 0:
                mxu_pieces[t_idx // per_piece]()
            lanes = slice(tg * LANES, (tg + 1) * LANES)
            bcast = lambda ref, h: jnp.broadcast_to(ref[ii, h:h + 1, lanes], (rb, LANES)).astype(BF16)
            cnt = [bcast(cnt_ref, h) for h in range(PEER_HEADS)]
            f1 = [bcast(f1_ref, h) for h in range(PEER_HEADS)]
            for j0 in range(0, nk, rb):
                wsel = jnp.zeros((rb, LANES), BF16)
                for h in range(PEER_HEADS):
                    slab = (h * nk + j0) // rb
                    picked = jnp.clip(cnt[h] - r2_ref[slab, tg], 0, 1)
                    wsel = wsel + picked * (f1[h] * e2_ref[slab, tg])
                rows = slice(ii * nk + j0, ii * nk + j0 + rb)
                g_w[rows, lanes] = wsel * jax.nn.gelu(ht_r[rows, lanes])

    parity = lax.rem(s, 2)
    pl.when(parity == 0)(functools.partial(stages, ht0_ref, ht1_ref, g1_ref, g0_ref))
    pl.when(parity == 1)(functools.partial(stages, ht1_ref, ht0_ref, g0_ref, g1_ref))

    @pl.when((c_out == nc - 1) & (s >= 2))
    def _():
        step = 512
        for d0 in range(0, d, step):
            o_ref[:, d0:d0 + step] = x_ref[:, d0:d0 + step] + acc_ref[d0:d0 + step, :].T


def _peer_dense(hn, u_tab, vt_tab, layer, cntr, f1r, r2, e2, x, tt=512, ec=1024):
    t, d = hn.shape
    e = u_tab.shape[1]
    tt, ec = _pick(t, tt), _pick(e, ec)
    ni = ec // PEER_NKEYS
    nc = e // ec
    n_steps = (t // tt) * nc
    n_slab = r2.shape[0]
    tile = lambda s, lag: jnp.clip(s - lag, 0, n_steps - 1) // nc
    chunk = lambda s, lag: jnp.clip(s - lag, 0, n_steps - 1) % nc
    once = dict(pipeline_mode=pl.Buffered(1))
    return pl.pallas_call(
        functools.partial(_peer_dense_kernel, tt=tt, ec=ec, nc=nc, d=d),
        grid=(n_steps + 2,),
        in_specs=[pl.BlockSpec((tt, d), lambda s: (tile(s, 0), 0), **once),
                  pl.BlockSpec((None, ec, d), lambda s: (layer, chunk(s, 0), 0)),
                  pl.BlockSpec((None, d, ec), lambda s: (layer, 0, chunk(s, 2))),
                  pl.BlockSpec((ni, PEER_HEADS, tt), lambda s: (chunk(s, 1), 0, tile(s, 1))),
                  pl.BlockSpec((ni, PEER_HEADS, tt), lambda s: (chunk(s, 1), 0, tile(s, 1))),
                  pl.BlockSpec((n_slab, tt // LANES, 2 * SUBLANES, LANES), lambda s: (0, tile(s, 1), 0, 0)),
                  pl.BlockSpec((n_slab, tt // LANES, 2 * SUBLANES, LANES), lambda s: (0, tile(s, 1), 0, 0)),
                  pl.BlockSpec((tt, d), lambda s: (tile(s, 2), 0), **once)],
        out_specs=pl.BlockSpec((tt, d), lambda s: (tile(s, 2), 0)),
        out_shape=jax.ShapeDtypeStruct((t, d), F32),
        scratch_shapes=[pltpu.VMEM((d, tt), F32)] + [pltpu.VMEM((ec, tt), BF16)] * 4,
        compiler_params=_cparams(("arbitrary",)),
        name="peer_dense",
    )(hn, u_tab, vt_tab, cntr, f1r, r2, e2, x)


def _layer(x, batch, seq, layer, w_in, conv_w, a_log, dt_bias, a_out_gain, gmlp_norm, w_spatial, b_spatial,
           c_q_gain, c_k_gain, rel_bias, p_a, p_b, p_c, w_out, norm_mix, norm_ffn,
           peer_wq, peer_keys, peer_u, peer_vt):
    o_qkv_a = 0
    o_z = o_qkv_a + 3 * A_WIDTH
    o_beta = o_z + A_WIDTH
    o_uv = o_beta + 2 * A_HEADS
    o_qkv_c = o_uv + 2 * GMLP_WIDTH
    o_gate = o_qkv_c + 3 * C_WIDTH
    d_model = x.shape[1]

    h = _rmsnorm(x, norm_mix)
    proj = lambda col0, n, dtype, name: _matmul(h, w_in, layer, col0, n, dtype, w_is_nk=True, name=name)
    qkv_a = proj(o_qkv_a, 3 * A_WIDTH, BF16, "proj_qkv_a")
    z_a = proj(o_z, A_WIDTH, BF16, "proj_z_a")
    ba = proj(o_beta, LANES, F32, "proj_beta_alpha")
    uv_b = proj(o_uv, 2 * GMLP_WIDTH, BF16, "proj_uv_b")
    qkv_c = proj(o_qkv_c, 3 * C_WIDTH, BF16, "proj_qkv_c")
    gate = proj(o_gate, N_BRANCH * d_model, BF16, "proj_gate")

    qkv_prep = _conv_prep(qkv_a, conv_w.astype(F32), batch, seq)
    gcum, beta, dec = _gates(ba, a_log, dt_bias)
    u, wq, kd, at, egl = _delta_solve(qkv_prep, gcum, beta, dec)
    y_a = _delta_rec(u, wq, kd, at, egl, z_a, a_out_gain, batch, seq)
    y_b = _gmlp(uv_b, gmlp_norm, w_spatial, b_spatial)
    y_c = _band_attention(qkv_c, c_q_gain, c_k_gain, rel_bias, batch, seq)

    merged = _merge(y_a, y_b, y_c, p_a, p_b, p_c, layer, gate)
    x = _matmul(merged, w_out, layer, 0, d_model, F32, residual=x, name="out_proj")

    hn = _rmsnorm(x, norm_ffn)
    q = _matmul(hn, peer_wq, layer, 0, peer_wq.shape[2], F32, name="peer_query")
    cnt, f1, r2, e2 = _peer_select(q, peer_keys)
    return _peer_dense(hn, peer_u, peer_vt, layer, cnt, f1, r2, e2, x)


def kernel(x, w_in, conv_w, a_log, dt_bias, a_out_gain, gmlp_norm, w_spatial, b_spatial, c_q_gain, c_k_gain,
           rel_bias, p_a, p_b, p_c, w_out, norm_mix, norm_ffn, peer_wq, peer_keys, peer_u, peer_v):
    batch, seq, d_model = x.shape
    xt = x.reshape(batch * seq, d_model)
    w_in_t = jnp.swapaxes(w_in, 1, 2)
    peer_u16 = peer_u.astype(BF16)
    peer_vt16 = jnp.swapaxes(peer_v.astype(BF16), 1, 2)
    for l in range(w_in.shape[0]):
        xt = _layer(xt, batch, seq, l, w_in_t, conv_w[l], a_log[l], dt_bias[l], a_out_gain[l], gmlp_norm[l],
                    w_spatial[l], b_spatial[l], c_q_gain[l], c_k_gain[l], rel_bias[l], p_a, p_b, p_c,
                    w_out, norm_mix[l], norm_ffn[l], peer_wq, peer_keys[l], peer_u16, peer_vt16)
    return xt.reshape(batch, seq, d_model)
```

```python
import functools

import jax
import jax.numpy as jnp
import numpy as np
from jax import lax
from jax.experimental import pallas as pl
from jax.experimental.pallas import tpu as pltpu

F32 = jnp.float32
BF16 = jnp.bfloat16
HIGHEST = lax.Precision.HIGHEST

CHUNK = 64
EPS = 1e-6
A_HEADS = 8
A_DK = 128
A_DV = 128
A_WIDTH = A_HEADS * A_DV
CONV_W = 4
GMLP_CHUNK = 128
GMLP_GROUPS = 8
GMLP_GDIM = 128
GMLP_WIDTH = GMLP_GROUPS * GMLP_GDIM
C_HEADS = 8
C_DH = 128
C_WIDTH = C_HEADS * C_DH
LEFT_CHUNKS = 8
BAND = (LEFT_CHUNKS + 1) * CHUNK
MAX_REL = 128
N_BRANCH = 3
PEER_HEADS = 8
PEER_NKEYS = 128
PEER_TOPK = 16
PEER_QDIM = 256
PEER_QHALF = PEER_QDIM // 2

LANES = 128
SUBLANES = 8
NEG_BIG = -1e30
VMEM_LIMIT = 56 * 1024 * 1024
CHUNK_SHIFT = CHUNK.bit_length() - 1


def _cparams(sem, vmem_limit=VMEM_LIMIT, flags=None):
    return pltpu.CompilerParams(dimension_semantics=sem, vmem_limit_bytes=vmem_limit, flags=flags)


def _nt_dot(a, b, precision=None):
    return lax.dot_general(a, b, (((1,), (1,)), ((), ())), precision=precision,
                           preferred_element_type=F32)


def _tn_dot(a, b, precision=None):
    return lax.dot_general(a, b, (((0,), (0,)), ((), ())), precision=precision,
                           preferred_element_type=F32)


def _bdot(a, b):
    return jnp.dot(a.astype(BF16), b.astype(BF16), preferred_element_type=F32)


def _sigmoid(x):
    return 0.5 * jnp.tanh(0.5 * x) + 0.5


def _pick(n, pref):
    t = min(pref, n)
    while n % t:
        t -= LANES if t > LANES else 8
    return t


def _rmsnorm_kernel(x_ref, g_ref, o_ref):
    x = x_ref[...]
    ms = jnp.mean(x * x, axis=-1, keepdims=True)
    o_ref[...] = (x * lax.rsqrt(ms + EPS) * g_ref[...]).astype(o_ref.dtype)


def _rmsnorm(x, gain, out_dtype=BF16, tm=512):
    m, d = x.shape
    tm = _pick(m, tm)
    return pl.pallas_call(
        _rmsnorm_kernel,
        grid=(m // tm,),
        in_specs=[pl.BlockSpec((tm, d), lambda i: (i, 0)), pl.BlockSpec((1, d), lambda i: (0, 0))],
        out_specs=pl.BlockSpec((tm, d), lambda i: (i, 0)),
        out_shape=jax.ShapeDtypeStruct((m, d), out_dtype),
        compiler_params=_cparams(("parallel",)),
        name="rmsnorm",
    )(x, gain.reshape(1, d))


def _mm_w32_kernel(*refs, shift, has_res, w_is_nk):
    refs = list(refs)
    a_ref, w_ref = refs[:2]
    tail_ref = refs[2] if shift else None
    rest = refs[3:] if shift else refs[2:]
    r_ref = rest[0] if has_res else None
    o_ref, wb_ref = rest[-2:]

    @pl.when(pl.program_id(1) == 0)
    def _():
        w = w_ref[...]
        if shift:
            w = jnp.concatenate([w[shift:], tail_ref[:shift]], axis=0)
        wb_ref[...] = w.astype(wb_ref.dtype)

    if w_is_nk:
        acc = _nt_dot(a_ref[...], wb_ref[...])
    else:
        acc = jnp.dot(a_ref[...], wb_ref[...], preferred_element_type=F32)
    if has_res:
        acc = r_ref[...] + acc
    o_ref[...] = acc.astype(o_ref.dtype)


def _matmul(a, w, layer, col0, n, out_dtype, residual=None, w_is_nk=False, tm=1024, tn=1024, name="matmul"):
    m, k = a.shape
    tm, tn = _pick(m, tm), _pick(n, tn)
    shift = col0 % LANES
    base = col0 - shift
    assert base % tn == 0 and shift % SUBLANES == 0 and (w_is_nk or not shift), (col0, tn)
    in_specs = [pl.BlockSpec((tm, k), lambda j, i: (i, 0))]
    if w_is_nk:
        in_specs.append(pl.BlockSpec((None, tn, k), lambda j, i: (layer, base // tn + j, 0)))
        wb_shape = (tn, k)
    else:
        in_specs.append(pl.BlockSpec((None, k, tn), lambda j, i: (layer, 0, base // tn + j)))
        wb_shape = (k, tn)
    args = [a, w]
    if shift:
        in_specs.append(pl.BlockSpec((None, LANES, k), lambda j, i: (layer, (base + (j + 1) * tn) // LANES, 0)))
        args.append(w)
    if residual is not None:
        in_specs.append(pl.BlockSpec((tm, tn), lambda j, i: (i, j)))
        args.append(residual)
    return pl.pallas_call(
        functools.partial(_mm_w32_kernel, shift=shift, has_res=residual is not None, w_is_nk=w_is_nk),
        grid=(n // tn, m // tm),
        in_specs=in_specs,
        out_specs=pl.BlockSpec((tm, tn), lambda j, i: (i, j)),
        out_shape=jax.ShapeDtypeStruct((m, n), out_dtype),
        scratch_shapes=[pltpu.VMEM(wb_shape, BF16)],
        compiler_params=_cparams(("parallel", "arbitrary")),
        name=name,
    )(*args)


def _conv_prep_kernel(cur_ref, prev_ref, w_ref, o_ref, ext_ref, *, ts, tc, prev_rows, n_qk_tiles, n_q_tiles):
    i = pl.program_id(1)
    c = pl.program_id(2)
    prev = prev_ref[...].astype(F32)
    prev = jnp.where(i == 0, 0.0, prev)
    cur = cur_ref[...].astype(F32)
    ext_ref[0:prev_rows, :] = prev
    ext_ref[prev_rows:prev_rows + ts, :] = cur
    w = w_ref[...]
    acc = cur * w[CONV_W - 1:CONV_W, :]
    for d in range(1, CONV_W):
        acc = acc + ext_ref[prev_rows - d:prev_rows - d + ts, :] * w[CONV_W - 1 - d:CONV_W - d, :]
    y = acc * _sigmoid(acc)
    parts = []
    for h in range(tc // A_DK):
        yh = y[:, h * A_DK:(h + 1) * A_DK]
        parts.append(yh * lax.rsqrt(jnp.sum(yh * yh, axis=-1, keepdims=True) + EPS))
    yn = jnp.concatenate(parts, axis=-1)
    scale = jnp.where(c < n_q_tiles, A_DK ** -0.5, 1.0).astype(F32)
    o_ref[...] = jnp.where(c < n_qk_tiles, yn * scale, y).astype(o_ref.dtype)


def _conv_prep(qkv, conv_w, batch, seq, ts=1024, tc=512):
    t, ch = qkv.shape
    ts = _pick(seq, ts)
    prev_rows = 16
    n_s = seq // ts
    kern = functools.partial(_conv_prep_kernel, ts=ts, tc=tc, prev_rows=prev_rows,
                             n_qk_tiles=2 * A_WIDTH // tc, n_q_tiles=A_WIDTH // tc)
    rpb = ts // prev_rows
    return pl.pallas_call(
        kern,
        grid=(batch, n_s, ch // tc),
        in_specs=[
            pl.BlockSpec((ts, tc), lambda b, i, c: (b * n_s + i, c)),
            pl.BlockSpec((prev_rows, tc), lambda b, i, c: (jnp.maximum((b * n_s + i) * rpb - 1, 0), c)),
            pl.BlockSpec((CONV_W, tc), lambda b, i, c: (0, c)),
        ],
        out_specs=pl.BlockSpec((ts, tc), lambda b, i, c: (b * n_s + i, c)),
        out_shape=jax.ShapeDtypeStruct((t, ch), BF16),
        scratch_shapes=[pltpu.VMEM((prev_rows + ts, tc), F32)],
        compiler_params=_cparams(("parallel", "parallel", "parallel")),
        name="conv_prep",
    )(qkv, qkv, conv_w)


def _split3(x):
    hi = x.astype(BF16)
    r = x - hi.astype(F32)
    mid = r.astype(BF16)
    return hi, mid, (r - mid.astype(F32)).astype(BF16)


def _gates_kernel(ba_ref, alog_ref, dtb_ref, g_ref, b_ref, dec_ref, *, ts):
    def hdot(a, b):
        if a.dtype == BF16:
            return sum(jnp.dot(a, p, preferred_element_type=F32) for p in _split3(b))
        return sum(jnp.dot(p, b, preferred_element_type=F32) for p in _split3(a))

    ba = ba_ref[...]
    lane = lax.broadcasted_iota(jnp.int32, (ts, LANES), 1)
    head_lane = lane < A_HEADS
    beta = jnp.where(head_lane, jax.nn.sigmoid(ba), 0.0)
    alpha = pltpu.roll(ba, LANES - A_HEADS, axis=1)
    g = -jnp.exp(alog_ref[...]) * jax.nn.softplus(alpha + dtb_ref[...])
    g = jnp.where(head_lane, g, 0.0)
    r = lax.broadcasted_iota(jnp.int32, (ts, ts), 0)
    c = lax.broadcasted_iota(jnp.int32, (ts, ts), 1)
    same_chunk = (r >> CHUNK_SHIFT) == (c >> CHUNK_SHIFT)
    tri = jnp.where((c <= r) & same_chunk, 1.0, 0.0).astype(BF16)
    gcum = hdot(tri, g)
    er = lax.broadcasted_iota(jnp.int32, (LANES, A_WIDTH), 0)
    ec = lax.broadcasted_iota(jnp.int32, (LANES, A_WIDTH), 1)
    spread = jnp.where(er == (ec >> (A_DV.bit_length() - 1)), 1.0, 0.0).astype(BF16)
    g_ref[...] = hdot(gcum, spread)
    b_ref[...] = hdot(beta, spread)
    wd = A_HEADS * CHUNK
    er = lax.broadcasted_iota(jnp.int32, (LANES, wd), 0)
    ec = lax.broadcasted_iota(jnp.int32, (LANES, wd), 1)
    gi = hdot(gcum, jnp.where(er == (ec >> CHUNK_SHIFT), 1.0, 0.0).astype(BF16))
    ipos = lax.broadcasted_iota(jnp.int32, (ts, wd), 0) & (CHUNK - 1)
    jpos = lax.broadcasted_iota(jnp.int32, (ts, wd), 1) & (CHUNK - 1)
    blk = jnp.where(same_chunk, 1.0, 0.0).astype(BF16)
    gj = hdot(blk, jnp.where(ipos == jpos, gi, 0.0))
    dec_ref[...] = jnp.exp(jnp.where(ipos >= jpos, gi - gj, NEG_BIG))


def _gates(ba, a_log, dt_bias, ts=256):
    t = ba.shape[0]
    ts = _pick(t, ts)
    pad = lambda v: jnp.pad(v.astype(F32), (0, LANES - A_HEADS)).reshape(1, LANES)
    out = jax.ShapeDtypeStruct((t, A_WIDTH), F32)
    wd = A_HEADS * CHUNK
    return pl.pallas_call(
        functools.partial(_gates_kernel, ts=ts),
        grid=(t // ts,),
        in_specs=[pl.BlockSpec((ts, LANES), lambda i: (i, 0)),
                  pl.BlockSpec((1, LANES), lambda i: (0, 0)), pl.BlockSpec((1, LANES), lambda i: (0, 0))],
        out_specs=[pl.BlockSpec((ts, A_WIDTH), lambda i: (i, 0)), pl.BlockSpec((ts, A_WIDTH), lambda i: (i, 0)),
                   pl.BlockSpec((ts, wd), lambda i: (i, 0))],
        out_shape=[out, out, jax.ShapeDtypeStruct((t, wd), F32)],
        compiler_params=_cparams(("parallel",)),
        name="gates",
    )(ba, pad(a_log), pad(dt_bias))


def _delta_solve_kernel(q_ref, k_ref, v_ref, g_ref, b_ref, dec_ref, u_ref, wq_ref, kd_ref, at_ref, egl_ref,
                        *, ts, hp, group=16):
    ri = lax.broadcasted_iota(jnp.int32, (CHUNK, CHUNK), 0)
    ci = lax.broadcasted_iota(jnp.int32, (CHUNK, CHUNK), 1)
    strict = ri > ci
    ident = jnp.where(ri == ci, 1.0, 0.0).astype(F32)

    def load(n, h):
        rows = slice(n * CHUNK, (n + 1) * CHUNK)
        cols = slice(h * A_DK, (h + 1) * A_DK)
        c = dict(n=n, rows=rows, cols=cols, dcols=slice(h * CHUNK, (h + 1) * CHUNK))
        c["q"] = q_ref[rows, cols].astype(F32)
        c["k"] = k_ref[rows, cols].astype(F32)
        c["gb"] = g_ref[rows, cols]
        c["bt"] = b_ref[rows, cols]
        c["kb"] = c["k"] * c["bt"]
        c["eg"] = jnp.exp(c["gb"])
        return c

    chains = [(n, h) for n in range(ts // CHUNK) for h in range(hp)]
    for g0 in range(0, len(chains), group):
        cs = [load(n, h) for n, h in chains[g0:g0 + group]]
        for c in cs:
            qk_kk = _nt_dot(jnp.concatenate([c["q"], c["kb"]], axis=0).astype(BF16), c["k"].astype(BF16))
            dec = dec_ref[c["rows"], c["dcols"]]
            c["attn"] = qk_kk[:CHUNK] * dec
            c["a"] = jnp.where(strict, qk_kk[CHUNK:] * dec, 0.0)
        for c in cs:
            c["inv"] = ident - c["a"]
            c["p"] = _bdot(c["a"], c["a"])
        for _ in range(CHUNK_SHIFT - 2):
            for c in cs:
                y = _bdot(jnp.concatenate([c["inv"], c["p"]], axis=0), c["p"])
                c["inv"] = c["inv"] + y[:CHUNK]
                c["p"] = y[CHUNK:]
        for c in cs:
            c["inv"] = c["inv"] + _bdot(c["inv"], c["p"])
        for c in cs:
            v = v_ref[c["rows"], c["cols"]].astype(F32)
            c["uw"] = _bdot(c["inv"], jnp.concatenate([v * c["bt"], c["kb"] * c["eg"]], axis=1))
        for c in cs:
            n, rows, cols = c["n"], c["rows"], c["cols"]
            g_last = c["gb"][CHUNK - 1:CHUNK, :]
            u_ref[rows, cols] = c["uw"][:, :A_DV].astype(u_ref.dtype)
            wq_ref[2 * n * CHUNK:(2 * n + 1) * CHUNK, cols] = c["uw"][:, A_DV:].astype(wq_ref.dtype)
            wq_ref[(2 * n + 1) * CHUNK:(2 * n + 2) * CHUNK, cols] = (c["q"] * c["eg"]).astype(wq_ref.dtype)
            kd_ref[rows, cols] = (c["k"] * jnp.exp(g_last - c["gb"])).astype(kd_ref.dtype)
            at_ref[rows, c["dcols"]] = c["attn"].astype(at_ref.dtype)
            egl_ref[n * SUBLANES:(n + 1) * SUBLANES, cols] = jnp.broadcast_to(jnp.exp(g_last), (SUBLANES, A_DV))


def _delta_solve(qkv, gcum, beta, dec, ts=512, hp=4):
    t = qkv.shape[0]
    ts = _pick(t, ts)
    hw = hp * A_DK
    n_hb = A_WIDTH // hw
    spec = lambda off: pl.BlockSpec((ts, hw), lambda i, hb: (i, off * n_hb + hb))
    wide = jax.ShapeDtypeStruct((t, A_WIDTH), BF16)
    return pl.pallas_call(
        functools.partial(_delta_solve_kernel, ts=ts, hp=hp),
        grid=(t // ts, n_hb),
        in_specs=[spec(0), spec(1), spec(2), spec(0), spec(0),
                  pl.BlockSpec((ts, hp * CHUNK), lambda i, hb: (i, hb))],
        out_specs=[spec(0),
                   pl.BlockSpec((2 * ts, hw), lambda i, hb: (i, hb)),
                   spec(0),
                   pl.BlockSpec((ts, hp * CHUNK), lambda i, hb: (i, hb)),
                   pl.BlockSpec((ts // CHUNK * SUBLANES, hw), lambda i, hb: (i, hb))],
        out_shape=[wide, jax.ShapeDtypeStruct((2 * t, A_WIDTH), BF16), wide,
                   jax.ShapeDtypeStruct((t, A_HEADS * CHUNK), BF16),
                   jax.ShapeDtypeStruct((t // CHUNK * SUBLANES, A_WIDTH), F32)],
        compiler_params=_cparams(("parallel", "parallel")),
        name="delta_solve",
    )(qkv, qkv, qkv, gcum, beta, dec)


def _delta_rec_kernel(u_ref, wq_ref, kd_ref, at_ref, egl_ref, z_ref, gain_ref, o_ref, s_ref, *, ts):
    @pl.when(pl.program_id(1) == 0)
    def _():
        s_ref[...] = jnp.zeros_like(s_ref)

    gain = gain_ref[...]

    heads = range(A_HEADS)
    col = lambda h: slice(h * A_DK, (h + 1) * A_DK)
    for n in range(ts // CHUNK):
        rows = slice(n * CHUNK, (n + 1) * CHUNK)
        rows2 = slice(2 * n * CHUNK, (2 * n + 2) * CHUNK)
        s = [s_ref[h] for h in heads]
        ws = [jnp.dot(wq_ref[rows2, col(h)], s[h].astype(BF16), preferred_element_type=F32)
              for h in heads]
        v_new = [(u_ref[rows, col(h)].astype(F32) - ws[h][:CHUNK]).astype(BF16) for h in heads]
        o = [ws[h][CHUNK:] + jnp.dot(at_ref[rows, h * CHUNK:(h + 1) * CHUNK], v_new[h], preferred_element_type=F32)
             for h in heads]
        for h in heads:
            eg_last = egl_ref[n * SUBLANES:n * SUBLANES + 1, col(h)]
            s_ref[h] = s[h] * eg_last + _tn_dot(kd_ref[rows, col(h)], v_new[h])
        for h in heads:
            on = o[h] * lax.rsqrt(jnp.mean(o[h] * o[h], axis=-1, keepdims=True) + EPS) * gain
            z = z_ref[rows, col(h)].astype(F32)
            o_ref[rows, col(h)] = (on * (z * _sigmoid(z))).astype(o_ref.dtype)


def _delta_rec(u, wq, kd, at, egl, z, out_gain, batch, seq, ts=512):
    t = u.shape[0]
    ts = _pick(seq, ts)
    n_s = seq // ts
    row = lambda b, i: (b * n_s + i, 0)
    return pl.pallas_call(
        functools.partial(_delta_rec_kernel, ts=ts),
        grid=(batch, n_s),
        in_specs=[pl.BlockSpec((ts, A_WIDTH), row), pl.BlockSpec((2 * ts, A_WIDTH), row),
                  pl.BlockSpec((ts, A_WIDTH), row), pl.BlockSpec((ts, A_HEADS * CHUNK), row),
                  pl.BlockSpec((ts // CHUNK * SUBLANES, A_WIDTH), row), pl.BlockSpec((ts, A_WIDTH), row),
                  pl.BlockSpec((1, A_DV), lambda b, i: (0, 0))],
        out_specs=pl.BlockSpec((ts, A_WIDTH), row),
        out_shape=jax.ShapeDtypeStruct((t, A_WIDTH), BF16),
        scratch_shapes=[pltpu.VMEM((A_HEADS, A_DK, A_DV), F32)],
        compiler_params=_cparams(("parallel", "arbitrary")),
        name="delta_rec",
    )(u, wq, kd, at, egl, z, out_gain.reshape(1, A_DV).astype(F32))


def _gmlp_kernel(u_ref, v_ref, gain_ref, w_ref, bias_ref, o_ref, *, nb):
    ri = lax.broadcasted_iota(jnp.int32, (GMLP_CHUNK, GMLP_CHUNK), 0)
    ci = lax.broadcasted_iota(jnp.int32, (GMLP_CHUNK, GMLP_CHUNK), 1)
    tril = ri >= ci
    gain = gain_ref[...]
    for blk in range(nb):
        rows = slice(blk * GMLP_CHUNK, (blk + 1) * GMLP_CHUNK)
        u = jax.nn.gelu(u_ref[rows, :].astype(F32))
        v = jax.nn.gelu(v_ref[rows, :].astype(F32))
        vn = (v * lax.rsqrt(jnp.mean(v * v, axis=-1, keepdims=True) + EPS) * gain).astype(BF16)
        for g in range(GMLP_GROUPS):
            cols = slice(g * GMLP_GDIM, (g + 1) * GMLP_GDIM)
            w = jnp.where(tril, w_ref[g], 0.0).astype(BF16)
            mixed = jnp.dot(w, vn[:, cols], preferred_element_type=F32) + bias_ref[:, cols]
            o_ref[rows, cols] = (u[:, cols] * mixed).astype(o_ref.dtype)


def _gmlp(uv, norm_gain, w_spatial, b_spatial, nb=8):
    t = uv.shape[0]
    tm = nb * GMLP_CHUNK
    bias = jnp.repeat(b_spatial.T.astype(F32), GMLP_GDIM, axis=1)
    return pl.pallas_call(
        functools.partial(_gmlp_kernel, nb=nb),
        grid=(t // tm,),
        in_specs=[pl.BlockSpec((tm, GMLP_WIDTH), lambda i: (i, 0)),
                  pl.BlockSpec((tm, GMLP_WIDTH), lambda i: (i, 1)),
                  pl.BlockSpec((1, GMLP_WIDTH), lambda i: (0, 0)),
                  pl.BlockSpec((GMLP_GROUPS, GMLP_CHUNK, GMLP_CHUNK), lambda i: (0, 0, 0)),
                  pl.BlockSpec((GMLP_CHUNK, GMLP_WIDTH), lambda i: (0, 0))],
        out_specs=pl.BlockSpec((tm, GMLP_WIDTH), lambda i: (i, 0)),
        out_shape=jax.ShapeDtypeStruct((t, GMLP_WIDTH), BF16),
        compiler_params=_cparams(("parallel",)),
        name="gmlp",
    )(uv, uv, norm_gain.reshape(1, GMLP_WIDTH).astype(F32), w_spatial.astype(F32), bias)


def _band_kernel(q_ref, kp_ref, kc_ref, vp_ref, vc_ref, qg_ref, kg_ref, bias_ref, o_ref, *, tq, hp):
    i = pl.program_id(2)

    def norm(x, gain):
        x = x.astype(F32)
        return x * lax.rsqrt(jnp.mean(x * x, axis=-1, keepdims=True) + EPS) * gain

    col = lambda h: slice(h * C_DH, (h + 1) * C_DH)
    heads = range(hp)
    qn = [norm(q_ref[:, col(h)], qg_ref[...]).astype(BF16) for h in heads]
    kcat = [jnp.concatenate([norm(kp_ref[:, col(h)], kg_ref[...]), norm(kc_ref[:, col(h)], kg_ref[...])],
                            axis=0).astype(BF16) for h in heads]
    vcat = [jnp.concatenate([vp_ref[:, col(h)], vc_ref[:, col(h)]], axis=0).astype(BF16) for h in heads]
    pad = LEFT_CHUNKS * CHUNK
    kpos = lax.broadcasted_iota(jnp.int32, (CHUNK, BAND), 1)
    lo = [c * CHUNK + (tq - pad) for c in range(tq // CHUNK)]
    pairs = [(h, c) for h in heads for c in range(tq // CHUNK)]
    s = [_nt_dot(qn[h][c * CHUNK:(c + 1) * CHUNK], kcat[h][lo[c]:lo[c] + BAND]) for h, c in pairs]
    p = []
    for (h, c), sc in zip(pairs, s):
        sc = sc * (C_DH ** -0.5) + bias_ref[h]
        valid = (i > 0) | (kpos + c * CHUNK >= pad)
        sc = jnp.where(valid, sc, NEG_BIG)
        e = jnp.exp(sc - jnp.max(sc, axis=-1, keepdims=True))
        p.append((e / jnp.sum(e, axis=-1, keepdims=True)).astype(BF16))
    o = [jnp.dot(pp, vcat[h][lo[c]:lo[c] + BAND], preferred_element_type=F32) for (h, c), pp in zip(pairs, p)]
    for (h, c), oo in zip(pairs, o):
        o_ref[c * CHUNK:(c + 1) * CHUNK, col(h)] = oo.astype(o_ref.dtype)


def _band_bias(rel_bias):
    diag = np.arange(-(CHUNK - 1), BAND)
    idx = np.clip(LEFT_CHUNKS * CHUNK - diag, -MAX_REL, MAX_REL) + MAX_REL
    vec = rel_bias.astype(F32)[:, idx]
    return jnp.stack([vec[:, CHUNK - 1 - i:CHUNK - 1 - i + BAND] for i in range(CHUNK)], axis=1)


def _band_attention(qkv, q_gain, k_gain, rel_bias, batch, seq, hp=2):
    t = qkv.shape[0]
    tq = LEFT_CHUNKS * CHUNK
    n_s = seq // tq
    n_hb = C_HEADS // hp
    hw = hp * C_DH
    bias = _band_bias(rel_bias)
    cur = lambda off: pl.BlockSpec((tq, hw), lambda b, h, i: (b * n_s + i, off * n_hb + h))
    prv = lambda off: pl.BlockSpec((tq, hw), lambda b, h, i: (b * n_s + jnp.maximum(i - 1, 0), off * n_hb + h))
    vec = pl.BlockSpec((1, C_DH), lambda b, h, i: (0, 0))
    return pl.pallas_call(
        functools.partial(_band_kernel, tq=tq, hp=hp),
        grid=(batch, n_hb, n_s),
        in_specs=[cur(0), prv(1), cur(1), prv(2), cur(2), vec, vec,
                  pl.BlockSpec((hp, CHUNK, BAND), lambda b, h, i: (h, 0, 0))],
        out_specs=pl.BlockSpec((tq, hw), lambda b, h, i: (b * n_s + i, h)),
        out_shape=jax.ShapeDtypeStruct((t, C_WIDTH), BF16),
        compiler_params=_cparams(("parallel", "parallel", "parallel")),
        name="band_attention",
    )(qkv, qkv, qkv, qkv, qkv, q_gain.reshape(1, C_DH).astype(F32), k_gain.reshape(1, C_DH).astype(F32), bias)


def _merge_kernel(ya_ref, yb_ref, yc_ref, pa_ref, pb_ref, pc_ref, ga_ref, gb_ref, gc_ref, o_ref,
                  wa_ref, wb_ref, wc_ref):
    @pl.when(pl.program_id(1) == 0)
    def _():
        for p_ref, w_ref in ((pa_ref, wa_ref), (pb_ref, wb_ref), (pc_ref, wc_ref)):
            w_ref[...] = p_ref[...].astype(w_ref.dtype)

    def branch(y_ref, w_ref, g_ref):
        gate = _sigmoid(g_ref[...].astype(F32))
        return gate * jnp.dot(y_ref[...], w_ref[...], preferred_element_type=F32)

    merged = branch(ya_ref, wa_ref, ga_ref) + branch(yb_ref, wb_ref, gb_ref) + branch(yc_ref, wc_ref, gc_ref)
    o_ref[...] = merged.astype(o_ref.dtype)


def _merge(ya, yb, yc, pa, pb, pc, layer, gate, tm=1024, tn=512):
    t, k = ya.shape
    d = pa.shape[2]
    tm, tn = _pick(t, tm), _pick(d, tn)
    nd = d // tn
    ysp = pl.BlockSpec((tm, k), lambda j, i: (i, 0))
    psp = pl.BlockSpec((None, k, tn), lambda j, i: (layer, 0, j))
    gsp = lambda br: pl.BlockSpec((tm, tn), lambda j, i: (i, br * nd + j))
    return pl.pallas_call(
        _merge_kernel,
        grid=(nd, t // tm),
        in_specs=[ysp, ysp, ysp, psp, psp, psp, gsp(0), gsp(1), gsp(2)],
        out_specs=pl.BlockSpec((tm, tn), lambda j, i: (i, j)),
        out_shape=jax.ShapeDtypeStruct((t, d), BF16),
        scratch_shapes=[pltpu.VMEM((k, tn), BF16)] * 3,
        compiler_params=_cparams(("parallel", "arbitrary")),
        name="merge",
    )(ya, yb, yc, pa, pb, pc, gate, gate, gate)


def _top_values(x, k, out_ref, want_rank=False):
    cur = x
    rank = jnp.full(x.shape, float(k), F32) if want_rank else None
    for r in range(k):
        m = jnp.max(cur, axis=0, keepdims=True)
        out_ref[r:r + 1, :] = m
        hit = cur == m
        if want_rank:
            rank = jnp.where(hit, float(r), rank)
        if r + 1 < k:
            cur = jnp.where(hit, NEG_BIG, cur)
    return rank


def _peer_select_kernel(q_ref, keys_ref, cnt_ref, f1_ref, r2_ref, e2_ref, top_ref, *, tt, hg):
    k = PEER_TOPK

    def head(h, sub):
        a_ref, b_ref, c_ref = top_ref.at[sub, 0], top_ref.at[sub, 1], top_ref.at[sub, 2]
        qh = q_ref[:, pl.ds(pl.multiple_of(h * PEER_QDIM, PEER_QDIM), PEER_QDIM)]
        s1 = _nt_dot(keys_ref[h, 0], qh[:, :PEER_QHALF], precision=HIGHEST)
        s2 = _nt_dot(keys_ref[h, 1], qh[:, PEER_QHALF:], precision=HIGHEST)
        _top_values(s1, k, a_ref)
        rank2 = _top_values(s2, k, b_ref, want_rank=True)
        av = a_ref[...]
        bv = b_ref[...]
        half = k // 2
        cand = jnp.concatenate([av[0:1] + bv] + [av[r:r + 1] + bv[:half] for r in range(1, half)]
                               + [av[half:] + bv[0:1]], axis=0)
        _top_values(cand, k, c_ref)
        cv = c_ref[...]
        z = jnp.sum(jnp.exp(cv - cv[0:1, :]), axis=0, keepdims=True)
        tau = cv[k - 1:k, :]
        cnt = jnp.zeros(s1.shape, F32)
        for c in range(half):
            ok = av + bv[c:c + 1] >= tau
            thr = jnp.min(jnp.where(ok, av, -NEG_BIG), axis=0, keepdims=True)
            cnt = cnt + jnp.where(s1 >= thr, 1.0, 0.0)
        n_hi = jnp.sum(jnp.where(av[0:1] + bv[half:] >= tau, 1.0, 0.0), axis=0, keepdims=True)
        cnt = cnt + jnp.where(s1 >= av[0:1], n_hi, 0.0)
        cnt_ref[:, pl.ds(h, 1), :] = cnt[:, None, :]
        f1_ref[:, pl.ds(h, 1), :] = (jnp.exp(s1 - av[0:1]) / z)[:, None, :]
        rank2 = rank2.astype(r2_ref.dtype)
        e2 = jnp.exp(s2 - bv[0:1]).astype(e2_ref.dtype)
        rb = 2 * SUBLANES
        for g in range(PEER_NKEYS // rb):
            for tl in range(tt // LANES):
                r2_ref[h * (PEER_NKEYS // rb) + g, tl] = rank2[g * rb:(g + 1) * rb, tl * LANES:(tl + 1) * LANES]
                e2_ref[h * (PEER_NKEYS // rb) + g, tl] = e2[g * rb:(g + 1) * rb, tl * LANES:(tl + 1) * LANES]

    def group(p, carry):
        for sub in range(hg):
            head(hg * p + sub, sub)
        return carry

    lax.fori_loop(0, PEER_HEADS // hg, group, 0)


def _peer_select(q, keys, tt=256, hg=4):
    t = q.shape[0]
    tt = _pick(t, tt)
    shape = (PEER_NKEYS, PEER_HEADS, t)
    bspec = pl.BlockSpec((PEER_NKEYS, PEER_HEADS, tt), lambda i: (0, 0, i))
    n_slab = PEER_HEADS * PEER_NKEYS // (2 * SUBLANES)
    flat = (n_slab, t // LANES, 2 * SUBLANES, LANES)
    flat_spec = pl.BlockSpec((n_slab, tt // LANES, 2 * SUBLANES, LANES), lambda i: (0, i, 0, 0))
    return pl.pallas_call(
        functools.partial(_peer_select_kernel, tt=tt, hg=hg),
        grid=(t // tt,),
        in_specs=[pl.BlockSpec((tt, PEER_HEADS * PEER_QDIM), lambda i: (i, 0)),
                  pl.BlockSpec((PEER_HEADS, 2, PEER_NKEYS, PEER_QHALF), lambda i: (0, 0, 0, 0))],
        out_specs=[bspec, bspec, flat_spec, flat_spec],
        out_shape=[jax.ShapeDtypeStruct(shape, F32), jax.ShapeDtypeStruct(shape, F32),
                   jax.ShapeDtypeStruct(flat, BF16), jax.ShapeDtypeStruct(flat, BF16)],
        scratch_shapes=[pltpu.VMEM((hg, 3, PEER_TOPK, tt), F32)],
        compiler_params=_cparams(("parallel",)),
        name="peer_select",
    )(q, keys.astype(F32))


def _peer_dense_kernel(hn_ref, u_ref, vt_ref, cnt_ref, f1_ref, r2_ref, e2_ref, x_ref, o_ref,
                       acc_ref, ht0_ref, ht1_ref, g0_ref, g1_ref, *, tt, ec, nc, d):
    s = pl.program_id(0)
    c_out = lax.rem(jnp.maximum(s - 2, 0), nc)

    @pl.when(s == 0)
    def _():
        for ref in (ht0_ref, ht1_ref, g0_ref, g1_ref):
            ref[...] = jnp.zeros_like(ref)

    @pl.when(c_out == 0)
    def _():
        acc_ref[...] = jnp.zeros_like(acc_ref)

    nk = PEER_NKEYS

    def stages(ht_w, ht_r, g_w, g_r):
        halves = [slice(0, tt // 2), slice(tt // 2, tt)]

        def stage_a(hs):
            ht_w[:, hs] = _nt_dot(u_ref[...], hn_ref[hs, :]).astype(ht_w.dtype)

        def stage_c(hs):
            acc_ref[:, hs] += jnp.dot(vt_ref[...], g_r[:, hs], preferred_element_type=F32)

        mxu_pieces = [functools.partial(stage_a, halves[0]), functools.partial(stage_a, halves[1]),
                      functools.partial(stage_c, halves[0]), functools.partial(stage_c, halves[1])]
        rb = 2 * SUBLANES
        tiles = [(ii, tg) for ii in range(ec // nk) for tg in range(tt // LANES)]
        per_piece = len(tiles) // len(mxu_pieces)
        for t_idx, (ii, tg) in enumerate(tiles):
            if t_idx % per_piece == 0:
                mxu_pieces[t_idx // per_piece]()
            lanes = slice(tg * LANES, (tg + 1) * LANES)
            bcast = lambda ref, h: jnp.broadcast_to(ref[ii, h:h + 1, lanes], (rb, LANES)).astype(BF16)
            cnt = [bcast(cnt_ref, h) for h in range(PEER_HEADS)]
            f1 = [bcast(f1_ref, h) for h in range(PEER_HEADS)]
            for j0 in range(0, nk, rb):
                wsel = jnp.zeros((rb, LANES), BF16)
                for h in range(PEER_HEADS):
                    slab = (h * nk + j0) // rb
                    picked = jnp.clip(cnt[h] - r2_ref[slab, tg], 0, 1)
                    wsel = wsel + picked * (f1[h] * e2_ref[slab, tg])
                rows = slice(ii * nk + j0, ii * nk + j0 + rb)
                g_w[rows, lanes] = wsel * jax.nn.gelu(ht_r[rows, lanes])

    parity = lax.rem(s, 2)
    pl.when(parity == 0)(functools.partial(stages, ht0_ref, ht1_ref, g1_ref, g0_ref))
    pl.when(parity == 1)(functools.partial(stages, ht1_ref, ht0_ref, g0_ref, g1_ref))

    @pl.when((c_out == nc - 1) & (s >= 2))
    def _():
        step = 512
        for d0 in range(0, d, step):
            o_ref[:, d0:d0 + step] = x_ref[:, d0:d0 + step] + acc_ref[d0:d0 + step, :].T


def _peer_dense(hn, u_tab, vt_tab, layer, cntr, f1r, r2, e2, x, tt=512, ec=1024):
    t, d = hn.shape
    e = u_tab.shape[1]
    tt, ec = _pick(t, tt), _pick(e, ec)
    ni = ec // PEER_NKEYS
    nc = e // ec
    n_steps = (t // tt) * nc
    n_slab = r2.shape[0]
    tile = lambda s, lag: jnp.clip(s - lag, 0, n_steps - 1) // nc
    chunk = lambda s, lag: jnp.clip(s - lag, 0, n_steps - 1) % nc
    once = dict(pipeline_mode=pl.Buffered(1))
    return pl.pallas_call(
        functools.partial(_peer_dense_kernel, tt=tt, ec=ec, nc=nc, d=d),
        grid=(n_steps + 2,),
        in_specs=[pl.BlockSpec((tt, d), lambda s: (tile(s, 0), 0), **once),
                  pl.BlockSpec((None, ec, d), lambda s: (layer, chunk(s, 0), 0)),
                  pl.BlockSpec((None, d, ec), lambda s: (layer, 0, chunk(s, 2))),
                  pl.BlockSpec((ni, PEER_HEADS, tt), lambda s: (chunk(s, 1), 0, tile(s, 1))),
                  pl.BlockSpec((ni, PEER_HEADS, tt), lambda s: (chunk(s, 1), 0, tile(s, 1))),
                  pl.BlockSpec((n_slab, tt // LANES, 2 * SUBLANES, LANES), lambda s: (0, tile(s, 1), 0, 0)),
                  pl.BlockSpec((n_slab, tt // LANES, 2 * SUBLANES, LANES), lambda s: (0, tile(s, 1), 0, 0)),
                  pl.BlockSpec((tt, d), lambda s: (tile(s, 2), 0), **once)],
        out_specs=pl.BlockSpec((tt, d), lambda s: (tile(s, 2), 0)),
        out_shape=jax.ShapeDtypeStruct((t, d), F32),
        scratch_shapes=[pltpu.VMEM((d, tt), F32)] + [pltpu.VMEM((ec, tt), BF16)] * 4,
        compiler_params=_cparams(("arbitrary",)),
        name="peer_dense",
    )(hn, u_tab, vt_tab, cntr, f1r, r2, e2, x)


def _layer(x, batch, seq, layer, w_in, conv_w, a_log, dt_bias, a_out_gain, gmlp_norm, w_spatial, b_spatial,
           c_q_gain, c_k_gain, rel_bias, p_a, p_b, p_c, w_out, norm_mix, norm_ffn,
           peer_wq, peer_keys, peer_u, peer_vt):
    o_qkv_a = 0
    o_z = o_qkv_a + 3 * A_WIDTH
    o_beta = o_z + A_WIDTH
    o_uv = o_beta + 2 * A_HEADS
    o_qkv_c = o_uv + 2 * GMLP_WIDTH
    o_gate = o_qkv_c + 3 * C_WIDTH
    d_model = x.shape[1]

    h = _rmsnorm(x, norm_mix)
    proj = lambda col0, n, dtype, name: _matmul(h, w_in, layer, col0, n, dtype, w_is_nk=True, name=name)
    qkv_a = proj(o_qkv_a, 3 * A_WIDTH, BF16, "proj_qkv_a")
    z_a = proj(o_z, A_WIDTH, BF16, "proj_z_a")
    ba = proj(o_beta, LANES, F32, "proj_beta_alpha")
    uv_b = proj(o_uv, 2 * GMLP_WIDTH, BF16, "proj_uv_b")
    qkv_c = proj(o_qkv_c, 3 * C_WIDTH, BF16, "proj_qkv_c")
    gate = proj(o_gate, N_BRANCH * d_model, BF16, "proj_gate")

    qkv_prep = _conv_prep(qkv_a, conv_w.astype(F32), batch, seq)
    gcum, beta, dec = _gates(ba, a_log, dt_bias)
    u, wq, kd, at, egl = _delta_solve(qkv_prep, gcum, beta, dec)
    y_a = _delta_rec(u, wq, kd, at, egl, z_a, a_out_gain, batch, seq)
    y_b = _gmlp(uv_b, gmlp_norm, w_spatial, b_spatial)
    y_c = _band_attention(qkv_c, c_q_gain, c_k_gain, rel_bias, batch, seq)

    merged = _merge(y_a, y_b, y_c, p_a, p_b, p_c, layer, gate)
    x = _matmul(merged, w_out, layer, 0, d_model, F32, residual=x, name="out_proj")

    hn = _rmsnorm(x, norm_ffn)
    q = _matmul(hn, peer_wq, layer, 0, peer_wq.shape[2], F32, name="peer_query")
    cnt, f1, r2, e2 = _peer_select(q, peer_keys)
    return _peer_dense(hn, peer_u, peer_vt, layer, cnt, f1, r2, e2, x)


def kernel(x, w_in, conv_w, a_log, dt_bias, a_out_gain, gmlp_norm, w_spatial, b_spatial, c_q_gain, c_k_gain,
           rel_bias, p_a, p_b, p_c, w_out, norm_mix, norm_ffn, peer_wq, peer_keys, peer_u, peer_v):
    batch, seq, d_model = x.shape
    xt = x.reshape(batch * seq, d_model)
    w_in_t = jnp.swapaxes(w_in, 1, 2)
    peer_u16 = peer_u.astype(BF16)
    peer_vt16 = jnp.swapaxes(peer_v.astype(BF16), 1, 2)
    for l in range(w_in.shape[0]):
        xt = _layer(xt, batch, seq, l, w_in_t, conv_w[l], a_log[l], dt_bias[l], a_out_gain[l], gmlp_norm[l],
                    w_spatial[l], b_spatial[l], c_q_gain[l], c_k_gain[l], rel_bias[l], p_a, p_b, p_c,
                    w_out, norm_mix[l], norm_ffn[l], peer_wq, peer_keys[l], peer_u16, peer_vt16)
    return xt.reshape(batch, seq, d_model)
```

```python
import functools

import jax
import jax.numpy as jnp
import numpy as np
from jax import lax
from jax.experimental import pallas as pl
from jax.experimental.pallas import tpu as pltpu

F32 = jnp.float32
BF16 = jnp.bfloat16
HIGHEST = lax.Precision.HIGHEST

CHUNK = 64
EPS = 1e-6
A_HEADS = 8
A_DK = 128
A_DV = 128
A_WIDTH = A_HEADS * A_DV
CONV_W = 4
GMLP_CHUNK = 128
GMLP_GROUPS = 8
GMLP_GDIM = 128
GMLP_WIDTH = GMLP_GROUPS * GMLP_GDIM
C_HEADS = 8
C_DH = 128
C_WIDTH = C_HEADS * C_DH
LEFT_CHUNKS = 8
BAND = (LEFT_CHUNKS + 1) * CHUNK
MAX_REL = 128
N_BRANCH = 3
PEER_HEADS = 8
PEER_NKEYS = 128
PEER_TOPK = 16
PEER_QDIM = 256
PEER_QHALF = PEER_QDIM // 2

LANES = 128
SUBLANES = 8
NEG_BIG = -1e30
VMEM_LIMIT = 56 * 1024 * 1024
CHUNK_SHIFT = CHUNK.bit_length() - 1


def _cparams(sem, vmem_limit=VMEM_LIMIT, flags=None):
    return pltpu.CompilerParams(dimension_semantics=sem, vmem_limit_bytes=vmem_limit, flags=flags)


def _nt_dot(a, b, precision=None):
    return lax.dot_general(a, b, (((1,), (1,)), ((), ())), precision=precision,
                           preferred_element_type=F32)


def _tn_dot(a, b, precision=None):
    return lax.dot_general(a, b, (((0,), (0,)), ((), ())), precision=precision,
                           preferred_element_type=F32)


def _bdot(a, b):
    return jnp.dot(a.astype(BF16), b.astype(BF16), preferred_element_type=F32)


def _sigmoid(x):
    return 0.5 * jnp.tanh(0.5 * x) + 0.5


def _pick(n, pref):
    t = min(pref, n)
    while n % t:
        t -= LANES if t > LANES else 8
    return t


def _rmsnorm_kernel(x_ref, g_ref, o_ref):
    x = x_ref[...]
    ms = jnp.mean(x * x, axis=-1, keepdims=True)
    o_ref[...] = (x * lax.rsqrt(ms + EPS) * g_ref[...]).astype(o_ref.dtype)


def _rmsnorm(x, gain, out_dtype=BF16, tm=512):
    m, d = x.shape
    tm = _pick(m, tm)
    return pl.pallas_call(
        _rmsnorm_kernel,
        grid=(m // tm,),
        in_specs=[pl.BlockSpec((tm, d), lambda i: (i, 0)), pl.BlockSpec((1, d), lambda i: (0, 0))],
        out_specs=pl.BlockSpec((tm, d), lambda i: (i, 0)),
        out_shape=jax.ShapeDtypeStruct((m, d), out_dtype),
        compiler_params=_cparams(("parallel",)),
        name="rmsnorm",
    )(x, gain.reshape(1, d))


def _mm_w32_kernel(*refs, shift, has_res, w_is_nk):
    refs = list(refs)
    a_ref, w_ref = refs[:2]
    tail_ref = refs[2] if shift else None
    rest = refs[3:] if shift else refs[2:]
    r_ref = rest[0] if has_res else None
    o_ref, wb_ref = rest[-2:]

    @pl.when(pl.program_id(1) == 0)
    def _():
        w = w_ref[...]
        if shift:
            w = jnp.concatenate([w[shift:], tail_ref[:shift]], axis=0)
        wb_ref[...] = w.astype(wb_ref.dtype)

    if w_is_nk:
        acc = _nt_dot(a_ref[...], wb_ref[...])
    else:
        acc = jnp.dot(a_ref[...], wb_ref[...], preferred_element_type=F32)
    if has_res:
        acc = r_ref[...] + acc
    o_ref[...] = acc.astype(o_ref.dtype)


def _matmul(a, w, layer, col0, n, out_dtype, residual=None, w_is_nk=False, tm=1024, tn=1024, name="matmul"):
    m, k = a.shape
    tm, tn = _pick(m, tm), _pick(n, tn)
    shift = col0 % LANES
    base = col0 - shift
    assert base % tn == 0 and shift % SUBLANES == 0 and (w_is_nk or not shift), (col0, tn)
    in_specs = [pl.BlockSpec((tm, k), lambda j, i: (i, 0))]
    if w_is_nk:
        in_specs.append(pl.BlockSpec((None, tn, k), lambda j, i: (layer, base // tn + j, 0)))
        wb_shape = (tn, k)
    else:
        in_specs.append(pl.BlockSpec((None, k, tn), lambda j, i: (layer, 0, base // tn + j)))
        wb_shape = (k, tn)
    args = [a, w]
    if shift:
        in_specs.append(pl.BlockSpec((None, LANES, k), lambda j, i: (layer, (base + (j + 1) * tn) // LANES, 0)))
        args.append(w)
    if residual is not None:
        in_specs.append(pl.BlockSpec((tm, tn), lambda j, i: (i, j)))
        args.append(residual)
    return pl.pallas_call(
        functools.partial(_mm_w32_kernel, shift=shift, has_res=residual is not None, w_is_nk=w_is_nk),
        grid=(n // tn, m // tm),
        in_specs=in_specs,
        out_specs=pl.BlockSpec((tm, tn), lambda j, i: (i, j)),
        out_shape=jax.ShapeDtypeStruct((m, n), out_dtype),
        scratch_shapes=[pltpu.VMEM(wb_shape, BF16)],
        compiler_params=_cparams(("parallel", "arbitrary")),
        name=name,
    )(*args)


def _conv_prep_kernel(cur_ref, prev_ref, w_ref, o_ref, ext_ref, *, ts, tc, prev_rows, n_qk_tiles, n_q_tiles):
    i = pl.program_id(1)
    c = pl.program_id(2)
    prev = prev_ref[...].astype(F32)
    prev = jnp.where(i == 0, 0.0, prev)
    cur = cur_ref[...].astype(F32)
    ext_ref[0:prev_rows, :] = prev
    ext_ref[prev_rows:prev_rows + ts, :] = cur
    w = w_ref[...]
    acc = cur * w[CONV_W - 1:CONV_W, :]
    for d in range(1, CONV_W):
        acc = acc + ext_ref[prev_rows - d:prev_rows - d + ts, :] * w[CONV_W - 1 - d:CONV_W - d, :]
    y = acc * _sigmoid(acc)
    parts = []
    for h in range(tc // A_DK):
        yh = y[:, h * A_DK:(h + 1) * A_DK]
        parts.append(yh * lax.rsqrt(jnp.sum(yh * yh, axis=-1, keepdims=True) + EPS))
    yn = jnp.concatenate(parts, axis=-1)
    scale = jnp.where(c < n_q_tiles, A_DK ** -0.5, 1.0).astype(F32)
    o_ref[...] = jnp.where(c < n_qk_tiles, yn * scale, y).astype(o_ref.dtype)


def _conv_prep(qkv, conv_w, batch, seq, ts=1024, tc=512):
    t, ch = qkv.shape
    ts = _pick(seq, ts)
    prev_rows = 16
    n_s = seq // ts
    kern = functools.partial(_conv_prep_kernel, ts=ts, tc=tc, prev_rows=prev_rows,
                             n_qk_tiles=2 * A_WIDTH // tc, n_q_tiles=A_WIDTH // tc)
    rpb = ts // prev_rows
    return pl.pallas_call(
        kern,
        grid=(batch, n_s, ch // tc),
        in_specs=[
            pl.BlockSpec((ts, tc), lambda b, i, c: (b * n_s + i, c)),
            pl.BlockSpec((prev_rows, tc), lambda b, i, c: (jnp.maximum((b * n_s + i) * rpb - 1, 0), c)),
            pl.BlockSpec((CONV_W, tc), lambda b, i, c: (0, c)),
        ],
        out_specs=pl.BlockSpec((ts, tc), lambda b, i, c: (b * n_s + i, c)),
        out_shape=jax.ShapeDtypeStruct((t, ch), BF16),
        scratch_shapes=[pltpu.VMEM((prev_rows + ts, tc), F32)],
        compiler_params=_cparams(("parallel", "parallel", "parallel")),
        name="conv_prep",
    )(qkv, qkv, conv_w)


def _split3(x):
    hi = x.astype(BF16)
    r = x - hi.astype(F32)
    mid = r.astype(BF16)
    return hi, mid, (r - mid.astype(F32)).astype(BF16)


def _gates_kernel(ba_ref, alog_ref, dtb_ref, g_ref, b_ref, dec_ref, *, ts):
    def hdot(a, b):
        if a.dtype == BF16:
            return sum(jnp.dot(a, p, preferred_element_type=F32) for p in _split3(b))
        return sum(jnp.dot(p, b, preferred_element_type=F32) for p in _split3(a))

    ba = ba_ref[...]
    lane = lax.broadcasted_iota(jnp.int32, (ts, LANES), 1)
    head_lane = lane < A_HEADS
    beta = jnp.where(head_lane, jax.nn.sigmoid(ba), 0.0)
    alpha = pltpu.roll(ba, LANES - A_HEADS, axis=1)
    g = -jnp.exp(alog_ref[...]) * jax.nn.softplus(alpha + dtb_ref[...])
    g = jnp.where(head_lane, g, 0.0)
    r = lax.broadcasted_iota(jnp.int32, (ts, ts), 0)
    c = lax.broadcasted_iota(jnp.int32, (ts, ts), 1)
    same_chunk = (r >> CHUNK_SHIFT) == (c >> CHUNK_SHIFT)
    tri = jnp.where((c <= r) & same_chunk, 1.0, 0.0).astype(BF16)
    gcum = hdot(tri, g)
    er = lax.broadcasted_iota(jnp.int32, (LANES, A_WIDTH), 0)
    ec = lax.broadcasted_iota(jnp.int32, (LANES, A_WIDTH), 1)
    spread = jnp.where(er == (ec >> (A_DV.bit_length() - 1)), 1.0, 0.0).astype(BF16)
    g_ref[...] = hdot(gcum, spread)
    b_ref[...] = hdot(beta, spread)
    wd = A_HEADS * CHUNK
    er = lax.broadcasted_iota(jnp.int32, (LANES, wd), 0)
    ec = lax.broadcasted_iota(jnp.int32, (LANES, wd), 1)
    gi = hdot(gcum, jnp.where(er == (ec >> CHUNK_SHIFT), 1.0, 0.0).astype(BF16))
    ipos = lax.broadcasted_iota(jnp.int32, (ts, wd), 0) & (CHUNK - 1)
    jpos = lax.broadcasted_iota(jnp.int32, (ts, wd), 1) & (CHUNK - 1)
    blk = jnp.where(same_chunk, 1.0, 0.0).astype(BF16)
    gj = hdot(blk, jnp.where(ipos == jpos, gi, 0.0))
    dec_ref[...] = jnp.exp(jnp.where(ipos >= jpos, gi - gj, NEG_BIG))


def _gates(ba, a_log, dt_bias, ts=256):
    t = ba.shape[0]
    ts = _pick(t, ts)
    pad = lambda v: jnp.pad(v.astype(F32), (0, LANES - A_HEADS)).reshape(1, LANES)
    out = jax.ShapeDtypeStruct((t, A_WIDTH), F32)
    wd = A_HEADS * CHUNK
    return pl.pallas_call(
        functools.partial(_gates_kernel, ts=ts),
        grid=(t // ts,),
        in_specs=[pl.BlockSpec((ts, LANES), lambda i: (i, 0)),
                  pl.BlockSpec((1, LANES), lambda i: (0, 0)), pl.BlockSpec((1, LANES), lambda i: (0, 0))],
        out_specs=[pl.BlockSpec((ts, A_WIDTH), lambda i: (i, 0)), pl.BlockSpec((ts, A_WIDTH), lambda i: (i, 0)),
                   pl.BlockSpec((ts, wd), lambda i: (i, 0))],
        out_shape=[out, out, jax.ShapeDtypeStruct((t, wd), F32)],
        compiler_params=_cparams(("parallel",)),
        name="gates",
    )(ba, pad(a_log), pad(dt_bias))


def _delta_solve_kernel(q_ref, k_ref, v_ref, g_ref, b_ref, dec_ref, u_ref, wq_ref, kd_ref, at_ref, egl_ref,
                        *, ts, hp, group=16):
    ri = lax.broadcasted_iota(jnp.int32, (CHUNK, CHUNK), 0)
    ci = lax.broadcasted_iota(jnp.int32, (CHUNK, CHUNK), 1)
    strict = ri > ci
    ident = jnp.where(ri == ci, 1.0, 0.0).astype(F32)

    def load(n, h):
        rows = slice(n * CHUNK, (n + 1) * CHUNK)
        cols = slice(h * A_DK, (h + 1) * A_DK)
        c = dict(n=n, rows=rows, cols=cols, dcols=slice(h * CHUNK, (h + 1) * CHUNK))
        c["q"] = q_ref[rows, cols].astype(F32)
        c["k"] = k_ref[rows, cols].astype(F32)
        c["gb"] = g_ref[rows, cols]
        c["bt"] = b_ref[rows, cols]
        c["kb"] = c["k"] * c["bt"]
        c["eg"] = jnp.exp(c["gb"])
        return c

    chains = [(n, h) for n in range(ts // CHUNK) for h in range(hp)]
    for g0 in range(0, len(chains), group):
        cs = [load(n, h) for n, h in chains[g0:g0 + group]]
        for c in cs:
            qk_kk = _nt_dot(jnp.concatenate([c["q"], c["kb"]], axis=0).astype(BF16), c["k"].astype(BF16))
            dec = dec_ref[c["rows"], c["dcols"]]
            c["attn"] = qk_kk[:CHUNK] * dec
            c["a"] = jnp.where(strict, qk_kk[CHUNK:] * dec, 0.0)
        for c in cs:
            c["inv"] = ident - c["a"]
            c["p"] = _bdot(c["a"], c["a"])
        for _ in range(CHUNK_SHIFT - 2):
            for c in cs:
                y = _bdot(jnp.concatenate([c["inv"], c["p"]], axis=0), c["p"])
                c["inv"] = c["inv"] + y[:CHUNK]
                c["p"] = y[CHUNK:]
        for c in cs:
            c["inv"] = c["inv"] + _bdot(c["inv"], c["p"])
        for c in cs:
            v = v_ref[c["rows"], c["cols"]].astype(F32)
            c["uw"] = _bdot(c["inv"], jnp.concatenate([v * c["bt"], c["kb"] * c["eg"]], axis=1))
        for c in cs:
            n, rows, cols = c["n"], c["rows"], c["cols"]
            g_last = c["gb"][CHUNK - 1:CHUNK, :]
            u_ref[rows, cols] = c["uw"][:, :A_DV].astype(u_ref.dtype)
            wq_ref[2 * n * CHUNK:(2 * n + 1) * CHUNK, cols] = c["uw"][:, A_DV:].astype(wq_ref.dtype)
            wq_ref[(2 * n + 1) * CHUNK:(2 * n + 2) * CHUNK, cols] = (c["q"] * c["eg"]).astype(wq_ref.dtype)
            kd_ref[rows, cols] = (c["k"] * jnp.exp(g_last - c["gb"])).astype(kd_ref.dtype)
            at_ref[rows, c["dcols"]] = c["attn"].astype(at_ref.dtype)
            egl_ref[n * SUBLANES:(n + 1) * SUBLANES, cols] = jnp.broadcast_to(jnp.exp(g_last), (SUBLANES, A_DV))


def _delta_solve(qkv, gcum, beta, dec, ts=512, hp=4):
    t = qkv.shape[0]
    ts = _pick(t, ts)
    hw = hp * A_DK
    n_hb = A_WIDTH // hw
    spec = lambda off: pl.BlockSpec((ts, hw), lambda i, hb: (i, off * n_hb + hb))
    wide = jax.ShapeDtypeStruct((t, A_WIDTH), BF16)
    return pl.pallas_call(
        functools.partial(_delta_solve_kernel, ts=ts, hp=hp),
        grid=(t // ts, n_hb),
        in_specs=[spec(0), spec(1), spec(2), spec(0), spec(0),
                  pl.BlockSpec((ts, hp * CHUNK), lambda i, hb: (i, hb))],
        out_specs=[spec(0),
                   pl.BlockSpec((2 * ts, hw), lambda i, hb: (i, hb)),
                   spec(0),
                   pl.BlockSpec((ts, hp * CHUNK), lambda i, hb: (i, hb)),
                   pl.BlockSpec((ts // CHUNK * SUBLANES, hw), lambda i, hb: (i, hb))],
        out_shape=[wide, jax.ShapeDtypeStruct((2 * t, A_WIDTH), BF16), wide,
                   jax.ShapeDtypeStruct((t, A_HEADS * CHUNK), BF16),
                   jax.ShapeDtypeStruct((t // CHUNK * SUBLANES, A_WIDTH), F32)],
        compiler_params=_cparams(("parallel", "parallel")),
        name="delta_solve",
    )(qkv, qkv, qkv, gcum, beta, dec)


def _delta_rec_kernel(u_ref, wq_ref, kd_ref, at_ref, egl_ref, z_ref, gain_ref, o_ref, s_ref, *, ts):
    @pl.when(pl.program_id(1) == 0)
    def _():
        s_ref[...] = jnp.zeros_like(s_ref)

    gain = gain_ref[...]

    heads = range(A_HEADS)
    col = lambda h: slice(h * A_DK, (h + 1) * A_DK)
    for n in range(ts // CHUNK):
        rows = slice(n * CHUNK, (n + 1) * CHUNK)
        rows2 = slice(2 * n * CHUNK, (2 * n + 2) * CHUNK)
        s = [s_ref[h] for h in heads]
        ws = [jnp.dot(wq_ref[rows2, col(h)], s[h].astype(BF16), preferred_element_type=F32)
              for h in heads]
        v_new = [(u_ref[rows, col(h)].astype(F32) - ws[h][:CHUNK]).astype(BF16) for h in heads]
        o = [ws[h][CHUNK:] + jnp.dot(at_ref[rows, h * CHUNK:(h + 1) * CHUNK], v_new[h], preferred_element_type=F32)
             for h in heads]
        for h in heads:
            eg_last = egl_ref[n * SUBLANES:n * SUBLANES + 1, col(h)]
            s_ref[h] = s[h] * eg_last + _tn_dot(kd_ref[rows, col(h)], v_new[h])
        for h in heads:
            on = o[h] * lax.rsqrt(jnp.mean(o[h] * o[h], axis=-1, keepdims=True) + EPS) * gain
            z = z_ref[rows, col(h)].astype(F32)
            o_ref[rows, col(h)] = (on * (z * _sigmoid(z))).astype(o_ref.dtype)


def _delta_rec(u, wq, kd, at, egl, z, out_gain, batch, seq, ts=512):
    t = u.shape[0]
    ts = _pick(seq, ts)
    n_s = seq // ts
    row = lambda b, i: (b * n_s + i, 0)
    return pl.pallas_call(
        functools.partial(_delta_rec_kernel, ts=ts),
        grid=(batch, n_s),
        in_specs=[pl.BlockSpec((ts, A_WIDTH), row), pl.BlockSpec((2 * ts, A_WIDTH), row),
                  pl.BlockSpec((ts, A_WIDTH), row), pl.BlockSpec((ts, A_HEADS * CHUNK), row),
                  pl.BlockSpec((ts // CHUNK * SUBLANES, A_WIDTH), row), pl.BlockSpec((ts, A_WIDTH), row),
                  pl.BlockSpec((1, A_DV), lambda b, i: (0, 0))],
        out_specs=pl.BlockSpec((ts, A_WIDTH), row),
        out_shape=jax.ShapeDtypeStruct((t, A_WIDTH), BF16),
        scratch_shapes=[pltpu.VMEM((A_HEADS, A_DK, A_DV), F32)],
        compiler_params=_cparams(("parallel", "arbitrary")),
        name="delta_rec",
    )(u, wq, kd, at, egl, z, out_gain.reshape(1, A_DV).astype(F32))


def _gmlp_kernel(u_ref, v_ref, gain_ref, w_ref, bias_ref, o_ref, *, nb):
    ri = lax.broadcasted_iota(jnp.int32, (GMLP_CHUNK, GMLP_CHUNK), 0)
    ci = lax.broadcasted_iota(jnp.int32, (GMLP_CHUNK, GMLP_CHUNK), 1)
    tril = ri >= ci
    gain = gain_ref[...]
    for blk in range(nb):
        rows = slice(blk * GMLP_CHUNK, (blk + 1) * GMLP_CHUNK)
        u = jax.nn.gelu(u_ref[rows, :].astype(F32))
        v = jax.nn.gelu(v_ref[rows, :].astype(F32))
        vn = (v * lax.rsqrt(jnp.mean(v * v, axis=-1, keepdims=True) + EPS) * gain).astype(BF16)
        for g in range(GMLP_GROUPS):
            cols = slice(g * GMLP_GDIM, (g + 1) * GMLP_GDIM)
            w = jnp.where(tril, w_ref[g], 0.0).astype(BF16)
            mixed = jnp.dot(w, vn[:, cols], preferred_element_type=F32) + bias_ref[:, cols]
            o_ref[rows, cols] = (u[:, cols] * mixed).astype(o_ref.dtype)


def _gmlp(uv, norm_gain, w_spatial, b_spatial, nb=8):
    t = uv.shape[0]
    tm = nb * GMLP_CHUNK
    bias = jnp.repeat(b_spatial.T.astype(F32), GMLP_GDIM, axis=1)
    return pl.pallas_call(
        functools.partial(_gmlp_kernel, nb=nb),
        grid=(t // tm,),
        in_specs=[pl.BlockSpec((tm, GMLP_WIDTH), lambda i: (i, 0)),
                  pl.BlockSpec((tm, GMLP_WIDTH), lambda i: (i, 1)),
                  pl.BlockSpec((1, GMLP_WIDTH), lambda i: (0, 0)),
                  pl.BlockSpec((GMLP_GROUPS, GMLP_CHUNK, GMLP_CHUNK), lambda i: (0, 0, 0)),
                  pl.BlockSpec((GMLP_CHUNK, GMLP_WIDTH), lambda i: (0, 0))],
        out_specs=pl.BlockSpec((tm, GMLP_WIDTH), lambda i: (i, 0)),
        out_shape=jax.ShapeDtypeStruct((t, GMLP_WIDTH), BF16),
        compiler_params=_cparams(("parallel",)),
        name="gmlp",
    )(uv, uv, norm_gain.reshape(1, GMLP_WIDTH).astype(F32), w_spatial.astype(F32), bias)


def _band_kernel(q_ref, kp_ref, kc_ref, vp_ref, vc_ref, qg_ref, kg_ref, bias_ref, o_ref, *, tq, hp):
    i = pl.program_id(2)

    def norm(x, gain):
        x = x.astype(F32)
        return x * lax.rsqrt(jnp.mean(x * x, axis=-1, keepdims=True) + EPS) * gain

    col = lambda h: slice(h * C_DH, (h + 1) * C_DH)
    heads = range(hp)
    qn = [norm(q_ref[:, col(h)], qg_ref[...]).astype(BF16) for h in heads]
    kcat = [jnp.concatenate([norm(kp_ref[:, col(h)], kg_ref[...]), norm(kc_ref[:, col(h)], kg_ref[...])],
                            axis=0).astype(BF16) for h in heads]
    vcat = [jnp.concatenate([vp_ref[:, col(h)], vc_ref[:, col(h)]], axis=0).astype(BF16) for h in heads]
    pad = LEFT_CHUNKS * CHUNK
    kpos = lax.broadcasted_iota(jnp.int32, (CHUNK, BAND), 1)
    lo = [c * CHUNK + (tq - pad) for c in range(tq // CHUNK)]
    pairs = [(h, c) for h in heads for c in range(tq // CHUNK)]
    s = [_nt_dot(qn[h][c * CHUNK:(c + 1) * CHUNK], kcat[h][lo[c]:lo[c] + BAND]) for h, c in pairs]
    p = []
    for (h, c), sc in zip(pairs, s):
        sc = sc * (C_DH ** -0.5) + bias_ref[h]
        valid = (i > 0) | (kpos + c * CHUNK >= pad)
        sc = jnp.where(valid, sc, NEG_BIG)
        e = jnp.exp(sc - jnp.max(sc, axis=-1, keepdims=True))
        p.append((e / jnp.sum(e, axis=-1, keepdims=True)).astype(BF16))
    o = [jnp.dot(pp, vcat[h][lo[c]:lo[c] + BAND], preferred_element_type=F32) for (h, c), pp in zip(pairs, p)]
    for (h, c), oo in zip(pairs, o):
        o_ref[c * CHUNK:(c + 1) * CHUNK, col(h)] = oo.astype(o_ref.dtype)


def _band_bias(rel_bias):
    diag = np.arange(-(CHUNK - 1), BAND)
    idx = np.clip(LEFT_CHUNKS * CHUNK - diag, -MAX_REL, MAX_REL) + MAX_REL
    vec = rel_bias.astype(F32)[:, idx]
    return jnp.stack([vec[:, CHUNK - 1 - i:CHUNK - 1 - i + BAND] for i in range(CHUNK)], axis=1)


def _band_attention(qkv, q_gain, k_gain, rel_bias, batch, seq, hp=2):
    t = qkv.shape[0]
    tq = LEFT_CHUNKS * CHUNK
    n_s = seq // tq
    n_hb = C_HEADS // hp
    hw = hp * C_DH
    bias = _band_bias(rel_bias)
    cur = lambda off: pl.BlockSpec((tq, hw), lambda b, h, i: (b * n_s + i, off * n_hb + h))
    prv = lambda off: pl.BlockSpec((tq, hw), lambda b, h, i: (b * n_s + jnp.maximum(i - 1, 0), off * n_hb + h))
    vec = pl.BlockSpec((1, C_DH), lambda b, h, i: (0, 0))
    return pl.pallas_call(
        functools.partial(_band_kernel, tq=tq, hp=hp),
        grid=(batch, n_hb, n_s),
        in_specs=[cur(0), prv(1), cur(1), prv(2), cur(2), vec, vec,
                  pl.BlockSpec((hp, CHUNK, BAND), lambda b, h, i: (h, 0, 0))],
        out_specs=pl.BlockSpec((tq, hw), lambda b, h, i: (b * n_s + i, h)),
        out_shape=jax.ShapeDtypeStruct((t, C_WIDTH), BF16),
        compiler_params=_cparams(("parallel", "parallel", "parallel")),
        name="band_attention",
    )(qkv, qkv, qkv, qkv, qkv, q_gain.reshape(1, C_DH).astype(F32), k_gain.reshape(1, C_DH).astype(F32), bias)


def _merge_kernel(ya_ref, yb_ref, yc_ref, pa_ref, pb_ref, pc_ref, ga_ref, gb_ref, gc_ref, o_ref,
                  wa_ref, wb_ref, wc_ref):
    @pl.when(pl.program_id(1) == 0)
    def _():
        for p_ref, w_ref in ((pa_ref, wa_ref), (pb_ref, wb_ref), (pc_ref, wc_ref)):
            w_ref[...] = p_ref[...].astype(w_ref.dtype)

    def branch(y_ref, w_ref, g_ref):
        gate = _sigmoid(g_ref[...].astype(F32))
        return gate * jnp.dot(y_ref[...], w_ref[...], preferred_element_type=F32)

    merged = branch(ya_ref, wa_ref, ga_ref) + branch(yb_ref, wb_ref, gb_ref) + branch(yc_ref, wc_ref, gc_ref)
    o_ref[...] = merged.astype(o_ref.dtype)


def _merge(ya, yb, yc, pa, pb, pc, layer, gate, tm=1024, tn=512):
    t, k = ya.shape
    d = pa.shape[2]
    tm, tn = _pick(t, tm), _pick(d, tn)
    nd = d // tn
    ysp = pl.BlockSpec((tm, k), lambda j, i: (i, 0))
    psp = pl.BlockSpec((None, k, tn), lambda j, i: (layer, 0, j))
    gsp = lambda br: pl.BlockSpec((tm, tn), lambda j, i: (i, br * nd + j))
    return pl.pallas_call(
        _merge_kernel,
        grid=(nd, t // tm),
        in_specs=[ysp, ysp, ysp, psp, psp, psp, gsp(0), gsp(1), gsp(2)],
        out_specs=pl.BlockSpec((tm, tn), lambda j, i: (i, j)),
        out_shape=jax.ShapeDtypeStruct((t, d), BF16),
        scratch_shapes=[pltpu.VMEM((k, tn), BF16)] * 3,
        compiler_params=_cparams(("parallel", "arbitrary")),
        name="merge",
    )(ya, yb, yc, pa, pb, pc, gate, gate, gate)


def _top_values(x, k, out_ref, want_rank=False):
    cur = x
    rank = jnp.full(x.shape, float(k), F32) if want_rank else None
    for r in range(k):
        m = jnp.max(cur, axis=0, keepdims=True)
        out_ref[r:r + 1, :] = m
        hit = cur == m
        if want_rank:
            rank = jnp.where(hit, float(r), rank)
        if r + 1 < k:
            cur = jnp.where(hit, NEG_BIG, cur)
    return rank


def _peer_select_kernel(q_ref, keys_ref, cnt_ref, f1_ref, r2_ref, e2_ref, top_ref, *, tt, hg):
    k = PEER_TOPK

    def head(h, sub):
        a_ref, b_ref, c_ref = top_ref.at[sub, 0], top_ref.at[sub, 1], top_ref.at[sub, 2]
        qh = q_ref[:, pl.ds(pl.multiple_of(h * PEER_QDIM, PEER_QDIM), PEER_QDIM)]
        s1 = _nt_dot(keys_ref[h, 0], qh[:, :PEER_QHALF], precision=HIGHEST)
        s2 = _nt_dot(keys_ref[h, 1], qh[:, PEER_QHALF:], precision=HIGHEST)
        _top_values(s1, k, a_ref)
        rank2 = _top_values(s2, k, b_ref, want_rank=True)
        av = a_ref[...]
        bv = b_ref[...]
        half = k // 2
        cand = jnp.concatenate([av[0:1] + bv] + [av[r:r + 1] + bv[:half] for r in range(1, half)]
                               + [av[half:] + bv[0:1]], axis=0)
        _top_values(cand, k, c_ref)
        cv = c_ref[...]
        z = jnp.sum(jnp.exp(cv - cv[0:1, :]), axis=0, keepdims=True)
        tau = cv[k - 1:k, :]
        cnt = jnp.zeros(s1.shape, F32)
        for c in range(half):
            ok = av + bv[c:c + 1] >= tau
            thr = jnp.min(jnp.where(ok, av, -NEG_BIG), axis=0, keepdims=True)
            cnt = cnt + jnp.where(s1 >= thr, 1.0, 0.0)
        n_hi = jnp.sum(jnp.where(av[0:1] + bv[half:] >= tau, 1.0, 0.0), axis=0, keepdims=True)
        cnt = cnt + jnp.where(s1 >= av[0:1], n_hi, 0.0)
        cnt_ref[:, pl.ds(h, 1), :] = cnt[:, None, :]
        f1_ref[:, pl.ds(h, 1), :] = (jnp.exp(s1 - av[0:1]) / z)[:, None, :]
        rank2 = rank2.astype(r2_ref.dtype)
        e2 = jnp.exp(s2 - bv[0:1]).astype(e2_ref.dtype)
        rb = 2 * SUBLANES
        for g in range(PEER_NKEYS // rb):
            for tl in range(tt // LANES):
                r2_ref[h * (PEER_NKEYS // rb) + g, tl] = rank2[g * rb:(g + 1) * rb, tl * LANES:(tl + 1) * LANES]
                e2_ref[h * (PEER_NKEYS // rb) + g, tl] = e2[g * rb:(g + 1) * rb, tl * LANES:(tl + 1) * LANES]

    def group(p, carry):
        for sub in range(hg):
            head(hg * p + sub, sub)
        return carry

    lax.fori_loop(0, PEER_HEADS // hg, group, 0)


def _peer_select(q, keys, tt=256, hg=4):
    t = q.shape[0]
    tt = _pick(t, tt)
    shape = (PEER_NKEYS, PEER_HEADS, t)
    bspec = pl.BlockSpec((PEER_NKEYS, PEER_HEADS, tt), lambda i: (0, 0, i))
    n_slab = PEER_HEADS * PEER_NKEYS // (2 * SUBLANES)
    flat = (n_slab, t // LANES, 2 * SUBLANES, LANES)
    flat_spec = pl.BlockSpec((n_slab, tt // LANES, 2 * SUBLANES, LANES), lambda i: (0, i, 0, 0))
    return pl.pallas_call(
        functools.partial(_peer_select_kernel, tt=tt, hg=hg),
        grid=(t // tt,),
        in_specs=[pl.BlockSpec((tt, PEER_HEADS * PEER_QDIM), lambda i: (i, 0)),
                  pl.BlockSpec((PEER_HEADS, 2, PEER_NKEYS, PEER_QHALF), lambda i: (0, 0, 0, 0))],
        out_specs=[bspec, bspec, flat_spec, flat_spec],
        out_shape=[jax.ShapeDtypeStruct(shape, F32), jax.ShapeDtypeStruct(shape, F32),
                   jax.ShapeDtypeStruct(flat, BF16), jax.ShapeDtypeStruct(flat, BF16)],
        scratch_shapes=[pltpu.VMEM((hg, 3, PEER_TOPK, tt), F32)],
        compiler_params=_cparams(("parallel",)),
        name="peer_select",
    )(q, keys.astype(F32))


def _peer_dense_kernel(hn_ref, u_ref, vt_ref, cnt_ref, f1_ref, r2_ref, e2_ref, x_ref, o_ref,
                       acc_ref, ht0_ref, ht1_ref, g0_ref, g1_ref, *, tt, ec, nc, d):
    s = pl.program_id(0)
    c_out = lax.rem(jnp.maximum(s - 2, 0), nc)

    @pl.when(s == 0)
    def _():
        for ref in (ht0_ref, ht1_ref, g0_ref, g1_ref):
            ref[...] = jnp.zeros_like(ref)

    @pl.when(c_out == 0)
    def _():
        acc_ref[...] = jnp.zeros_like(acc_ref)

    nk = PEER_NKEYS

    def stages(ht_w, ht_r, g_w, g_r):
        halves = [slice(0, tt // 2), slice(tt // 2, tt)]

        def stage_a(hs):
            ht_w[:, hs] = _nt_dot(u_ref[...], hn_ref[hs, :]).astype(ht_w.dtype)

        def stage_c(hs):
            acc_ref[:, hs] += jnp.dot(vt_ref[...], g_r[:, hs], preferred_element_type=F32)

        mxu_pieces = [functools.partial(stage_a, halves[0]), functools.partial(stage_a, halves[1]),
                      functools.partial(stage_c, halves[0]), functools.partial(stage_c, halves[1])]
        rb = 2 * SUBLANES
        tiles = [(ii, tg) for ii in range(ec // nk) for tg in range(tt // LANES)]
        per_piece = len(tiles) // len(mxu_pieces)
        for t_idx, (ii, tg) in enumerate(tiles):
            if t_idx % per_piece == 0:
                mxu_pieces[t_idx // per_piece]()
            lanes = slice(tg * LANES, (tg + 1) * LANES)
            bcast = lambda ref, h: jnp.broadcast_to(ref[ii, h:h + 1, lanes], (rb, LANES)).astype(BF16)
            cnt = [bcast(cnt_ref, h) for h in range(PEER_HEADS)]
            f1 = [bcast(f1_ref, h) for h in range(PEER_HEADS)]
            for j0 in range(0, nk, rb):
                wsel = jnp.zeros((rb, LANES), BF16)
                for h in range(PEER_HEADS):
                    slab = (h * nk + j0) // rb
                    gate1 = jnp.minimum(jnp.maximum(cnt[h] - r2_ref[slab, tg], 0), f1[h])
                    wsel = wsel + gate1 * e2_ref[slab, tg]
                rows = slice(ii * nk + j0, ii * nk + j0 + rb)
                g_w[rows, lanes] = wsel * jax.nn.gelu(ht_r[rows, lanes])

    parity = lax.rem(s, 2)
    pl.when(parity == 0)(functools.partial(stages, ht0_ref, ht1_ref, g1_ref, g0_ref))
    pl.when(parity == 1)(functools.partial(stages, ht1_ref, ht0_ref, g0_ref, g1_ref))

    @pl.when((c_out == nc - 1) & (s >= 2))
    def _():
        step = 512
        for d0 in range(0, d, step):
            o_ref[:, d0:d0 + step] = x_ref[:, d0:d0 + step] + acc_ref[d0:d0 + step, :].T


def _peer_dense(hn, u_tab, vt_tab, layer, cntr, f1r, r2, e2, x, tt=512, ec=1024):
    t, d = hn.shape
    e = u_tab.shape[1]
    tt, ec = _pick(t, tt), _pick(e, ec)
    ni = ec // PEER_NKEYS
    nc = e // ec
    n_steps = (t // tt) * nc
    n_slab = r2.shape[0]
    tile = lambda s, lag: jnp.clip(s - lag, 0, n_steps - 1) // nc
    chunk = lambda s, lag: jnp.clip(s - lag, 0, n_steps - 1) % nc
    once = dict(pipeline_mode=pl.Buffered(1))
    return pl.pallas_call(
        functools.partial(_peer_dense_kernel, tt=tt, ec=ec, nc=nc, d=d),
        grid=(n_steps + 2,),
        in_specs=[pl.BlockSpec((tt, d), lambda s: (tile(s, 0), 0), **once),
                  pl.BlockSpec((None, ec, d), lambda s: (layer, chunk(s, 0), 0)),
                  pl.BlockSpec((None, d, ec), lambda s: (layer, 0, chunk(s, 2))),
                  pl.BlockSpec((ni, PEER_HEADS, tt), lambda s: (chunk(s, 1), 0, tile(s, 1))),
                  pl.BlockSpec((ni, PEER_HEADS, tt), lambda s: (chunk(s, 1), 0, tile(s, 1))),
                  pl.BlockSpec((n_slab, tt // LANES, 2 * SUBLANES, LANES), lambda s: (0, tile(s, 1), 0, 0)),
                  pl.BlockSpec((n_slab, tt // LANES, 2 * SUBLANES, LANES), lambda s: (0, tile(s, 1), 0, 0)),
                  pl.BlockSpec((tt, d), lambda s: (tile(s, 2), 0), **once)],
        out_specs=pl.BlockSpec((tt, d), lambda s: (tile(s, 2), 0)),
        out_shape=jax.ShapeDtypeStruct((t, d), F32),
        scratch_shapes=[pltpu.VMEM((d, tt), F32)] + [pltpu.VMEM((ec, tt), BF16)] * 4,
        compiler_params=_cparams(("arbitrary",)),
        name="peer_dense",
    )(hn, u_tab, vt_tab, cntr, f1r, r2, e2, x)


def _layer(x, batch, seq, layer, w_in, conv_w, a_log, dt_bias, a_out_gain, gmlp_norm, w_spatial, b_spatial,
           c_q_gain, c_k_gain, rel_bias, p_a, p_b, p_c, w_out, norm_mix, norm_ffn,
           peer_wq, peer_keys, peer_u, peer_vt):
    o_qkv_a = 0
    o_z = o_qkv_a + 3 * A_WIDTH
    o_beta = o_z + A_WIDTH
    o_uv = o_beta + 2 * A_HEADS
    o_qkv_c = o_uv + 2 * GMLP_WIDTH
    o_gate = o_qkv_c + 3 * C_WIDTH
    d_model = x.shape[1]

    h = _rmsnorm(x, norm_mix)
    proj = lambda col0, n, dtype, name: _matmul(h, w_in, layer, col0, n, dtype, w_is_nk=True, name=name)
    qkv_a = proj(o_qkv_a, 3 * A_WIDTH, BF16, "proj_qkv_a")
    z_a = proj(o_z, A_WIDTH, BF16, "proj_z_a")
    ba = proj(o_beta, LANES, F32, "proj_beta_alpha")
    uv_b = proj(o_uv, 2 * GMLP_WIDTH, BF16, "proj_uv_b")
    qkv_c = proj(o_qkv_c, 3 * C_WIDTH, BF16, "proj_qkv_c")
    gate = proj(o_gate, N_BRANCH * d_model, BF16, "proj_gate")

    qkv_prep = _conv_prep(qkv_a, conv_w.astype(F32), batch, seq)
    gcum, beta, dec = _gates(ba, a_log, dt_bias)
    u, wq, kd, at, egl = _delta_solve(qkv_prep, gcum, beta, dec)
    y_a = _delta_rec(u, wq, kd, at, egl, z_a, a_out_gain, batch, seq)
    y_b = _gmlp(uv_b, gmlp_norm, w_spatial, b_spatial)
    y_c = _band_attention(qkv_c, c_q_gain, c_k_gain, rel_bias, batch, seq)

    merged = _merge(y_a, y_b, y_c, p_a, p_b, p_c, layer, gate)
    x = _matmul(merged, w_out, layer, 0, d_model, F32, residual=x, name="out_proj")

    hn = _rmsnorm(x, norm_ffn)
    q = _matmul(hn, peer_wq, layer, 0, peer_wq.shape[2], F32, name="peer_query")
    cnt, f1, r2, e2 = _peer_select(q, peer_keys)
    return _peer_dense(hn, peer_u, peer_vt, layer, cnt, f1, r2, e2, x)


def kernel(x, w_in, conv_w, a_log, dt_bias, a_out_gain, gmlp_norm, w_spatial, b_spatial, c_q_gain, c_k_gain,
           rel_bias, p_a, p_b, p_c, w_out, norm_mix, norm_ffn, peer_wq, peer_keys, peer_u, peer_v):
    batch, seq, d_model = x.shape
    xt = x.reshape(batch * seq, d_model)
    w_in_t = jnp.swapaxes(w_in, 1, 2)
    peer_u16 = peer_u.astype(BF16)
    peer_vt16 = jnp.swapaxes(peer_v.astype(BF16), 1, 2)
    for l in range(w_in.shape[0]):
        xt = _layer(xt, batch, seq, l, w_in_t, conv_w[l], a_log[l], dt_bias[l], a_out_gain[l], gmlp_norm[l],
                    w_spatial[l], b_spatial[l], c_q_gain[l], c_k_gain[l], rel_bias[l], p_a, p_b, p_c,
                    w_out, norm_mix[l], norm_ffn[l], peer_wq, peer_keys[l], peer_u16, peer_vt16)
    return xt.reshape(batch, seq, d_model)
```

```python
import functools

import jax
import jax.numpy as jnp
import numpy as np
from jax import lax
from jax.experimental import pallas as pl
from jax.experimental.pallas import tpu as pltpu

F32 = jnp.float32
BF16 = jnp.bfloat16
HIGHEST = lax.Precision.HIGHEST

CHUNK = 64
EPS = 1e-6
A_HEADS = 8
A_DK = 128
A_DV = 128
A_WIDTH = A_HEADS * A_DV
CONV_W = 4
GMLP_CHUNK = 128
GMLP_GROUPS = 8
GMLP_GDIM = 128
GMLP_WIDTH = GMLP_GROUPS * GMLP_GDIM
C_HEADS = 8
C_DH = 128
C_WIDTH = C_HEADS * C_DH
LEFT_CHUNKS = 8
BAND = (LEFT_CHUNKS + 1) * CHUNK
MAX_REL = 128
N_BRANCH = 3
PEER_HEADS = 8
PEER_NKEYS = 128
PEER_TOPK = 16
PEER_QDIM = 256
PEER_QHALF = PEER_QDIM // 2

LANES = 128
SUBLANES = 8
NEG_BIG = -1e30
GELU_C = float(np.sqrt(2.0 / np.pi))
GELU_CUBIC = 0.044715
VMEM_LIMIT = 56 * 1024 * 1024
CHUNK_SHIFT = CHUNK.bit_length() - 1


def _cparams(sem, vmem_limit=VMEM_LIMIT, flags=None):
    return pltpu.CompilerParams(dimension_semantics=sem, vmem_limit_bytes=vmem_limit, flags=flags)


def _nt_dot(a, b, precision=None):
    return lax.dot_general(a, b, (((1,), (1,)), ((), ())), precision=precision,
                           preferred_element_type=F32)


def _tn_dot(a, b, precision=None):
    return lax.dot_general(a, b, (((0,), (0,)), ((), ())), precision=precision,
                           preferred_element_type=F32)


def _bdot(a, b):
    return jnp.dot(a.astype(BF16), b.astype(BF16), preferred_element_type=F32)


def _sigmoid(x):
    return 0.5 * jnp.tanh(0.5 * x) + 0.5


def _pick(n, pref):
    t = min(pref, n)
    while n % t:
        t -= LANES if t > LANES else 8
    return t


def _rmsnorm_kernel(x_ref, g_ref, o_ref):
    x = x_ref[...]
    ms = jnp.mean(x * x, axis=-1, keepdims=True)
    o_ref[...] = (x * lax.rsqrt(ms + EPS) * g_ref[...]).astype(o_ref.dtype)


def _rmsnorm(x, gain, out_dtype=BF16, tm=512):
    m, d = x.shape
    tm = _pick(m, tm)
    return pl.pallas_call(
        _rmsnorm_kernel,
        grid=(m // tm,),
        in_specs=[pl.BlockSpec((tm, d), lambda i: (i, 0)), pl.BlockSpec((1, d), lambda i: (0, 0))],
        out_specs=pl.BlockSpec((tm, d), lambda i: (i, 0)),
        out_shape=jax.ShapeDtypeStruct((m, d), out_dtype),
        compiler_params=_cparams(("parallel",)),
        name="rmsnorm",
    )(x, gain.reshape(1, d))


def _mm_w32_kernel(*refs, shift, has_res, w_is_nk):
    refs = list(refs)
    a_ref, w_ref = refs[:2]
    tail_ref = refs[2] if shift else None
    rest = refs[3:] if shift else refs[2:]
    r_ref = rest[0] if has_res else None
    o_ref, wb_ref = rest[-2:]

    @pl.when(pl.program_id(1) == 0)
    def _():
        w = w_ref[...]
        if shift:
            w = jnp.concatenate([w[shift:], tail_ref[:shift]], axis=0)
        wb_ref[...] = w.astype(wb_ref.dtype)

    if w_is_nk:
        acc = _nt_dot(a_ref[...], wb_ref[...])
    else:
        acc = jnp.dot(a_ref[...], wb_ref[...], preferred_element_type=F32)
    if has_res:
        acc = r_ref[...] + acc
    o_ref[...] = acc.astype(o_ref.dtype)


def _matmul(a, w, layer, col0, n, out_dtype, residual=None, w_is_nk=False, tm=1024, tn=1024, name="matmul"):
    m, k = a.shape
    tm, tn = _pick(m, tm), _pick(n, tn)
    shift = col0 % LANES
    base = col0 - shift
    assert base % tn == 0 and shift % SUBLANES == 0 and (w_is_nk or not shift), (col0, tn)
    in_specs = [pl.BlockSpec((tm, k), lambda j, i: (i, 0))]
    if w_is_nk:
        in_specs.append(pl.BlockSpec((None, tn, k), lambda j, i: (layer, base // tn + j, 0)))
        wb_shape = (tn, k)
    else:
        in_specs.append(pl.BlockSpec((None, k, tn), lambda j, i: (layer, 0, base // tn + j)))
        wb_shape = (k, tn)
    args = [a, w]
    if shift:
        in_specs.append(pl.BlockSpec((None, LANES, k), lambda j, i: (layer, (base + (j + 1) * tn) // LANES, 0)))
        args.append(w)
    if residual is not None:
        in_specs.append(pl.BlockSpec((tm, tn), lambda j, i: (i, j)))
        args.append(residual)
    return pl.pallas_call(
        functools.partial(_mm_w32_kernel, shift=shift, has_res=residual is not None, w_is_nk=w_is_nk),
        grid=(n // tn, m // tm),
        in_specs=in_specs,
        out_specs=pl.BlockSpec((tm, tn), lambda j, i: (i, j)),
        out_shape=jax.ShapeDtypeStruct((m, n), out_dtype),
        scratch_shapes=[pltpu.VMEM(wb_shape, BF16)],
        compiler_params=_cparams(("parallel", "arbitrary")),
        name=name,
    )(*args)


def _conv_prep_kernel(cur_ref, prev_ref, w_ref, o_ref, ext_ref, *, ts, tc, prev_rows, n_qk_tiles, n_q_tiles):
    i = pl.program_id(1)
    c = pl.program_id(2)
    prev = prev_ref[...].astype(F32)
    prev = jnp.where(i == 0, 0.0, prev)
    cur = cur_ref[...].astype(F32)
    ext_ref[0:prev_rows, :] = prev
    ext_ref[prev_rows:prev_rows + ts, :] = cur
    w = w_ref[...]
    acc = cur * w[CONV_W - 1:CONV_W, :]
    for d in range(1, CONV_W):
        acc = acc + ext_ref[prev_rows - d:prev_rows - d + ts, :] * w[CONV_W - 1 - d:CONV_W - d, :]
    y = acc * _sigmoid(acc)
    parts = []
    for h in range(tc // A_DK):
        yh = y[:, h * A_DK:(h + 1) * A_DK]
        parts.append(yh * lax.rsqrt(jnp.sum(yh * yh, axis=-1, keepdims=True) + EPS))
    yn = jnp.concatenate(parts, axis=-1)
    scale = jnp.where(c < n_q_tiles, A_DK ** -0.5, 1.0).astype(F32)
    o_ref[...] = jnp.where(c < n_qk_tiles, yn * scale, y).astype(o_ref.dtype)


def _conv_prep(qkv, conv_w, batch, seq, ts=1024, tc=512):
    t, ch = qkv.shape
    ts = _pick(seq, ts)
    prev_rows = 16
    n_s = seq // ts
    kern = functools.partial(_conv_prep_kernel, ts=ts, tc=tc, prev_rows=prev_rows,
                             n_qk_tiles=2 * A_WIDTH // tc, n_q_tiles=A_WIDTH // tc)
    rpb = ts // prev_rows
    return pl.pallas_call(
        kern,
        grid=(batch, n_s, ch // tc),
        in_specs=[
            pl.BlockSpec((ts, tc), lambda b, i, c: (b * n_s + i, c)),
            pl.BlockSpec((prev_rows, tc), lambda b, i, c: (jnp.maximum((b * n_s + i) * rpb - 1, 0), c)),
            pl.BlockSpec((CONV_W, tc), lambda b, i, c: (0, c)),
        ],
        out_specs=pl.BlockSpec((ts, tc), lambda b, i, c: (b * n_s + i, c)),
        out_shape=jax.ShapeDtypeStruct((t, ch), BF16),
        scratch_shapes=[pltpu.VMEM((prev_rows + ts, tc), F32)],
        compiler_params=_cparams(("parallel", "parallel", "parallel")),
        name="conv_prep",
    )(qkv, qkv, conv_w)


def _split3(x):
    hi = x.astype(BF16)
    r = x - hi.astype(F32)
    mid = r.astype(BF16)
    return hi, mid, (r - mid.astype(F32)).astype(BF16)


def _gates_kernel(ba_ref, alog_ref, dtb_ref, g_ref, b_ref, dec_ref, *, ts):
    def hdot(a, b):
        if a.dtype == BF16:
            return sum(jnp.dot(a, p, preferred_element_type=F32) for p in _split3(b))
        return sum(jnp.dot(p, b, preferred_element_type=F32) for p in _split3(a))

    ba = ba_ref[...]
    lane = lax.broadcasted_iota(jnp.int32, (ts, LANES), 1)
    head_lane = lane < A_HEADS
    beta = jnp.where(head_lane, jax.nn.sigmoid(ba), 0.0)
    alpha = pltpu.roll(ba, LANES - A_HEADS, axis=1)
    g = -jnp.exp(alog_ref[...]) * jax.nn.softplus(alpha + dtb_ref[...])
    g = jnp.where(head_lane, g, 0.0)
    r = lax.broadcasted_iota(jnp.int32, (ts, ts), 0)
    c = lax.broadcasted_iota(jnp.int32, (ts, ts), 1)
    same_chunk = (r >> CHUNK_SHIFT) == (c >> CHUNK_SHIFT)
    tri = jnp.where((c <= r) & same_chunk, 1.0, 0.0).astype(BF16)
    gcum = hdot(tri, g)
    er = lax.broadcasted_iota(jnp.int32, (LANES, A_WIDTH), 0)
    ec = lax.broadcasted_iota(jnp.int32, (LANES, A_WIDTH), 1)
    spread = jnp.where(er == (ec >> (A_DV.bit_length() - 1)), 1.0, 0.0).astype(BF16)
    g_ref[...] = hdot(gcum, spread)
    b_ref[...] = hdot(beta, spread)
    wd = A_HEADS * CHUNK
    er = lax.broadcasted_iota(jnp.int32, (LANES, wd), 0)
    ec = lax.broadcasted_iota(jnp.int32, (LANES, wd), 1)
    gi = hdot(gcum, jnp.where(er == (ec >> CHUNK_SHIFT), 1.0, 0.0).astype(BF16))
    ipos = lax.broadcasted_iota(jnp.int32, (ts, wd), 0) & (CHUNK - 1)
    jpos = lax.broadcasted_iota(jnp.int32, (ts, wd), 1) & (CHUNK - 1)
    blk = jnp.where(same_chunk, 1.0, 0.0).astype(BF16)
    gj = hdot(blk, jnp.where(ipos == jpos, gi, 0.0))
    dec_ref[...] = jnp.exp(jnp.where(ipos >= jpos, gi - gj, NEG_BIG))


def _gates(ba, a_log, dt_bias, ts=256):
    t = ba.shape[0]
    ts = _pick(t, ts)
    pad = lambda v: jnp.pad(v.astype(F32), (0, LANES - A_HEADS)).reshape(1, LANES)
    out = jax.ShapeDtypeStruct((t, A_WIDTH), F32)
    wd = A_HEADS * CHUNK
    return pl.pallas_call(
        functools.partial(_gates_kernel, ts=ts),
        grid=(t // ts,),
        in_specs=[pl.BlockSpec((ts, LANES), lambda i: (i, 0)),
                  pl.BlockSpec((1, LANES), lambda i: (0, 0)), pl.BlockSpec((1, LANES), lambda i: (0, 0))],
        out_specs=[pl.BlockSpec((ts, A_WIDTH), lambda i: (i, 0)), pl.BlockSpec((ts, A_WIDTH), lambda i: (i, 0)),
                   pl.BlockSpec((ts, wd), lambda i: (i, 0))],
        out_shape=[out, out, jax.ShapeDtypeStruct((t, wd), F32)],
        compiler_params=_cparams(("parallel",)),
        name="gates",
    )(ba, pad(a_log), pad(dt_bias))


def _delta_solve_kernel(q_ref, k_ref, v_ref, g_ref, b_ref, dec_ref, u_ref, wq_ref, kd_ref, at_ref, egl_ref,
                        *, ts, hp, group=16):
    ri = lax.broadcasted_iota(jnp.int32, (CHUNK, CHUNK), 0)
    ci = lax.broadcasted_iota(jnp.int32, (CHUNK, CHUNK), 1)
    strict = ri > ci
    ident = jnp.where(ri == ci, 1.0, 0.0).astype(F32)

    def load(n, h):
        rows = slice(n * CHUNK, (n + 1) * CHUNK)
        cols = slice(h * A_DK, (h + 1) * A_DK)
        c = dict(n=n, rows=rows, cols=cols, dcols=slice(h * CHUNK, (h + 1) * CHUNK))
        c["q"] = q_ref[rows, cols].astype(F32)
        c["k"] = k_ref[rows, cols].astype(F32)
        c["gb"] = g_ref[rows, cols]
        c["bt"] = b_ref[rows, cols]
        c["kb"] = c["k"] * c["bt"]
        c["eg"] = jnp.exp(c["gb"])
        return c

    chains = [(n, h) for n in range(ts // CHUNK) for h in range(hp)]
    for g0 in range(0, len(chains), group):
        cs = [load(n, h) for n, h in chains[g0:g0 + group]]
        for c in cs:
            qk_kk = _nt_dot(jnp.concatenate([c["q"], c["kb"]], axis=0).astype(BF16), c["k"].astype(BF16))
            dec = dec_ref[c["rows"], c["dcols"]]
            c["attn"] = qk_kk[:CHUNK] * dec
            c["a"] = jnp.where(strict, qk_kk[CHUNK:] * dec, 0.0)
        for c in cs:
            c["inv"] = ident - c["a"]
            c["p"] = _bdot(c["a"], c["a"])
        for _ in range(CHUNK_SHIFT - 2):
            for c in cs:
                y = _bdot(jnp.concatenate([c["inv"], c["p"]], axis=0), c["p"])
                c["inv"] = c["inv"] + y[:CHUNK]
                c["p"] = y[CHUNK:]
        for c in cs:
            c["inv"] = c["inv"] + _bdot(c["inv"], c["p"])
        for c in cs:
            v = v_ref[c["rows"], c["cols"]].astype(F32)
            c["uw"] = _bdot(c["inv"], jnp.concatenate([v * c["bt"], c["kb"] * c["eg"]], axis=1))
        for c in cs:
            n, rows, cols = c["n"], c["rows"], c["cols"]
            g_last = c["gb"][CHUNK - 1:CHUNK, :]
            u_ref[rows, cols] = c["uw"][:, :A_DV].astype(u_ref.dtype)
            wq_ref[2 * n * CHUNK:(2 * n + 1) * CHUNK, cols] = c["uw"][:, A_DV:].astype(wq_ref.dtype)
            wq_ref[(2 * n + 1) * CHUNK:(2 * n + 2) * CHUNK, cols] = (c["q"] * c["eg"]).astype(wq_ref.dtype)
            kd_ref[rows, cols] = (c["k"] * jnp.exp(g_last - c["gb"])).astype(kd_ref.dtype)
            at_ref[rows, c["dcols"]] = c["attn"].astype(at_ref.dtype)
            egl_ref[n * SUBLANES:(n + 1) * SUBLANES, cols] = jnp.broadcast_to(jnp.exp(g_last), (SUBLANES, A_DV))


def _delta_solve(qkv, gcum, beta, dec, ts=512, hp=4):
    t = qkv.shape[0]
    ts = _pick(t, ts)
    hw = hp * A_DK
    n_hb = A_WIDTH // hw
    spec = lambda off: pl.BlockSpec((ts, hw), lambda i, hb: (i, off * n_hb + hb))
    wide = jax.ShapeDtypeStruct((t, A_WIDTH), BF16)
    return pl.pallas_call(
        functools.partial(_delta_solve_kernel, ts=ts, hp=hp),
        grid=(t // ts, n_hb),
        in_specs=[spec(0), spec(1), spec(2), spec(0), spec(0),
                  pl.BlockSpec((ts, hp * CHUNK), lambda i, hb: (i, hb))],
        out_specs=[spec(0),
                   pl.BlockSpec((2 * ts, hw), lambda i, hb: (i, hb)),
                   spec(0),
                   pl.BlockSpec((ts, hp * CHUNK), lambda i, hb: (i, hb)),
                   pl.BlockSpec((ts // CHUNK * SUBLANES, hw), lambda i, hb: (i, hb))],
        out_shape=[wide, jax.ShapeDtypeStruct((2 * t, A_WIDTH), BF16), wide,
                   jax.ShapeDtypeStruct((t, A_HEADS * CHUNK), BF16),
                   jax.ShapeDtypeStruct((t // CHUNK * SUBLANES, A_WIDTH), F32)],
        compiler_params=_cparams(("parallel", "parallel")),
        name="delta_solve",
    )(qkv, qkv, qkv, gcum, beta, dec)


def _delta_rec_kernel(u_ref, wq_ref, kd_ref, at_ref, egl_ref, z_ref, gain_ref, o_ref, s_ref, *, ts):
    @pl.when(pl.program_id(1) == 0)
    def _():
        s_ref[...] = jnp.zeros_like(s_ref)

    gain = gain_ref[...]

    heads = range(A_HEADS)
    col = lambda h: slice(h * A_DK, (h + 1) * A_DK)
    for n in range(ts // CHUNK):
        rows = slice(n * CHUNK, (n + 1) * CHUNK)
        rows2 = slice(2 * n * CHUNK, (2 * n + 2) * CHUNK)
        s = [s_ref[h] for h in heads]
        ws = [jnp.dot(wq_ref[rows2, col(h)], s[h].astype(BF16), preferred_element_type=F32)
              for h in heads]
        v_new = [(u_ref[rows, col(h)].astype(F32) - ws[h][:CHUNK]).astype(BF16) for h in heads]
        o = [ws[h][CHUNK:] + jnp.dot(at_ref[rows, h * CHUNK:(h + 1) * CHUNK], v_new[h], preferred_element_type=F32)
             for h in heads]
        for h in heads:
            eg_last = egl_ref[n * SUBLANES:n * SUBLANES + 1, col(h)]
            s_ref[h] = s[h] * eg_last + _tn_dot(kd_ref[rows, col(h)], v_new[h])
        for h in heads:
            on = o[h] * lax.rsqrt(jnp.mean(o[h] * o[h], axis=-1, keepdims=True) + EPS) * gain
            z = z_ref[rows, col(h)].astype(F32)
            o_ref[rows, col(h)] = (on * (z * _sigmoid(z))).astype(o_ref.dtype)


def _delta_rec(u, wq, kd, at, egl, z, out_gain, batch, seq, ts=512):
    t = u.shape[0]
    ts = _pick(seq, ts)
    n_s = seq // ts
    row = lambda b, i: (b * n_s + i, 0)
    return pl.pallas_call(
        functools.partial(_delta_rec_kernel, ts=ts),
        grid=(batch, n_s),
        in_specs=[pl.BlockSpec((ts, A_WIDTH), row), pl.BlockSpec((2 * ts, A_WIDTH), row),
                  pl.BlockSpec((ts, A_WIDTH), row), pl.BlockSpec((ts, A_HEADS * CHUNK), row),
                  pl.BlockSpec((ts // CHUNK * SUBLANES, A_WIDTH), row), pl.BlockSpec((ts, A_WIDTH), row),
                  pl.BlockSpec((1, A_DV), lambda b, i: (0, 0))],
        out_specs=pl.BlockSpec((ts, A_WIDTH), row),
        out_shape=jax.ShapeDtypeStruct((t, A_WIDTH), BF16),
        scratch_shapes=[pltpu.VMEM((A_HEADS, A_DK, A_DV), F32)],
        compiler_params=_cparams(("parallel", "arbitrary")),
        name="delta_rec",
    )(u, wq, kd, at, egl, z, out_gain.reshape(1, A_DV).astype(F32))


def _gmlp_kernel(u_ref, v_ref, gain_ref, w_ref, bias_ref, o_ref, *, nb):
    ri = lax.broadcasted_iota(jnp.int32, (GMLP_CHUNK, GMLP_CHUNK), 0)
    ci = lax.broadcasted_iota(jnp.int32, (GMLP_CHUNK, GMLP_CHUNK), 1)
    tril = ri >= ci
    gain = gain_ref[...]
    for blk in range(nb):
        rows = slice(blk * GMLP_CHUNK, (blk + 1) * GMLP_CHUNK)
        u = jax.nn.gelu(u_ref[rows, :].astype(F32))
        v = jax.nn.gelu(v_ref[rows, :].astype(F32))
        vn = (v * lax.rsqrt(jnp.mean(v * v, axis=-1, keepdims=True) + EPS) * gain).astype(BF16)
        for g in range(GMLP_GROUPS):
            cols = slice(g * GMLP_GDIM, (g + 1) * GMLP_GDIM)
            w = jnp.where(tril, w_ref[g], 0.0).astype(BF16)
            mixed = jnp.dot(w, vn[:, cols], preferred_element_type=F32) + bias_ref[:, cols]
            o_ref[rows, cols] = (u[:, cols] * mixed).astype(o_ref.dtype)


def _gmlp(uv, norm_gain, w_spatial, b_spatial, nb=8):
    t = uv.shape[0]
    tm = nb * GMLP_CHUNK
    bias = jnp.repeat(b_spatial.T.astype(F32), GMLP_GDIM, axis=1)
    return pl.pallas_call(
        functools.partial(_gmlp_kernel, nb=nb),
        grid=(t // tm,),
        in_specs=[pl.BlockSpec((tm, GMLP_WIDTH), lambda i: (i, 0)),
                  pl.BlockSpec((tm, GMLP_WIDTH), lambda i: (i, 1)),
                  pl.BlockSpec((1, GMLP_WIDTH), lambda i: (0, 0)),
                  pl.BlockSpec((GMLP_GROUPS, GMLP_CHUNK, GMLP_CHUNK), lambda i: (0, 0, 0)),
                  pl.BlockSpec((GMLP_CHUNK, GMLP_WIDTH), lambda i: (0, 0))],
        out_specs=pl.BlockSpec((tm, GMLP_WIDTH), lambda i: (i, 0)),
        out_shape=jax.ShapeDtypeStruct((t, GMLP_WIDTH), BF16),
        compiler_params=_cparams(("parallel",)),
        name="gmlp",
    )(uv, uv, norm_gain.reshape(1, GMLP_WIDTH).astype(F32), w_spatial.astype(F32), bias)


def _band_kernel(q_ref, kp_ref, kc_ref, vp_ref, vc_ref, qg_ref, kg_ref, bias_ref, o_ref, *, tq, hp):
    i = pl.program_id(2)

    def norm(x, gain):
        x = x.astype(F32)
        return x * lax.rsqrt(jnp.mean(x * x, axis=-1, keepdims=True) + EPS) * gain

    col = lambda h: slice(h * C_DH, (h + 1) * C_DH)
    heads = range(hp)
    qn = [norm(q_ref[:, col(h)], qg_ref[...]).astype(BF16) for h in heads]
    kcat = [jnp.concatenate([norm(kp_ref[:, col(h)], kg_ref[...]), norm(kc_ref[:, col(h)], kg_ref[...])],
                            axis=0).astype(BF16) for h in heads]
    vcat = [jnp.concatenate([vp_ref[:, col(h)], vc_ref[:, col(h)]], axis=0).astype(BF16) for h in heads]
    pad = LEFT_CHUNKS * CHUNK
    kpos = lax.broadcasted_iota(jnp.int32, (CHUNK, BAND), 1)
    lo = [c * CHUNK + (tq - pad) for c in range(tq // CHUNK)]
    pairs = [(h, c) for h in heads for c in range(tq // CHUNK)]
    s = [_nt_dot(qn[h][c * CHUNK:(c + 1) * CHUNK], kcat[h][lo[c]:lo[c] + BAND]) for h, c in pairs]
    p = []
    for (h, c), sc in zip(pairs, s):
        sc = sc * (C_DH ** -0.5) + bias_ref[h]
        valid = (i > 0) | (kpos + c * CHUNK >= pad)
        sc = jnp.where(valid, sc, NEG_BIG)
        e = jnp.exp(sc - jnp.max(sc, axis=-1, keepdims=True))
        p.append((e / jnp.sum(e, axis=-1, keepdims=True)).astype(BF16))
    o = [jnp.dot(pp, vcat[h][lo[c]:lo[c] + BAND], preferred_element_type=F32) for (h, c), pp in zip(pairs, p)]
    for (h, c), oo in zip(pairs, o):
        o_ref[c * CHUNK:(c + 1) * CHUNK, col(h)] = oo.astype(o_ref.dtype)


def _band_bias(rel_bias):
    diag = np.arange(-(CHUNK - 1), BAND)
    idx = np.clip(LEFT_CHUNKS * CHUNK - diag, -MAX_REL, MAX_REL) + MAX_REL
    vec = rel_bias.astype(F32)[:, idx]
    return jnp.stack([vec[:, CHUNK - 1 - i:CHUNK - 1 - i + BAND] for i in range(CHUNK)], axis=1)


def _band_attention(qkv, q_gain, k_gain, rel_bias, batch, seq, hp=2):
    t = qkv.shape[0]
    tq = LEFT_CHUNKS * CHUNK
    n_s = seq // tq
    n_hb = C_HEADS // hp
    hw = hp * C_DH
    bias = _band_bias(rel_bias)
    cur = lambda off: pl.BlockSpec((tq, hw), lambda b, h, i: (b * n_s + i, off * n_hb + h))
    prv = lambda off: pl.BlockSpec((tq, hw), lambda b, h, i: (b * n_s + jnp.maximum(i - 1, 0), off * n_hb + h))
    vec = pl.BlockSpec((1, C_DH), lambda b, h, i: (0, 0))
    return pl.pallas_call(
        functools.partial(_band_kernel, tq=tq, hp=hp),
        grid=(batch, n_hb, n_s),
        in_specs=[cur(0), prv(1), cur(1), prv(2), cur(2), vec, vec,
                  pl.BlockSpec((hp, CHUNK, BAND), lambda b, h, i: (h, 0, 0))],
        out_specs=pl.BlockSpec((tq, hw), lambda b, h, i: (b * n_s + i, h)),
        out_shape=jax.ShapeDtypeStruct((t, C_WIDTH), BF16),
        compiler_params=_cparams(("parallel", "parallel", "parallel")),
        name="band_attention",
    )(qkv, qkv, qkv, qkv, qkv, q_gain.reshape(1, C_DH).astype(F32), k_gain.reshape(1, C_DH).astype(F32), bias)


def _merge_kernel(ya_ref, yb_ref, yc_ref, pa_ref, pb_ref, pc_ref, ga_ref, gb_ref, gc_ref, o_ref,
                  wa_ref, wb_ref, wc_ref):
    @pl.when(pl.program_id(1) == 0)
    def _():
        for p_ref, w_ref in ((pa_ref, wa_ref), (pb_ref, wb_ref), (pc_ref, wc_ref)):
            w_ref[...] = p_ref[...].astype(w_ref.dtype)

    def branch(y_ref, w_ref, g_ref):
        gate = _sigmoid(g_ref[...].astype(F32))
        return gate * jnp.dot(y_ref[...], w_ref[...], preferred_element_type=F32)

    merged = branch(ya_ref, wa_ref, ga_ref) + branch(yb_ref, wb_ref, gb_ref) + branch(yc_ref, wc_ref, gc_ref)
    o_ref[...] = merged.astype(o_ref.dtype)


def _merge(ya, yb, yc, pa, pb, pc, layer, gate, tm=1024, tn=512):
    t, k = ya.shape
    d = pa.shape[2]
    tm, tn = _pick(t, tm), _pick(d, tn)
    nd = d // tn
    ysp = pl.BlockSpec((tm, k), lambda j, i: (i, 0))
    psp = pl.BlockSpec((None, k, tn), lambda j, i: (layer, 0, j))
    gsp = lambda br: pl.BlockSpec((tm, tn), lambda j, i: (i, br * nd + j))
    return pl.pallas_call(
        _merge_kernel,
        grid=(nd, t // tm),
        in_specs=[ysp, ysp, ysp, psp, psp, psp, gsp(0), gsp(1), gsp(2)],
        out_specs=pl.BlockSpec((tm, tn), lambda j, i: (i, j)),
        out_shape=jax.ShapeDtypeStruct((t, d), BF16),
        scratch_shapes=[pltpu.VMEM((k, tn), BF16)] * 3,
        compiler_params=_cparams(("parallel", "arbitrary")),
        name="merge",
    )(ya, yb, yc, pa, pb, pc, gate, gate, gate)


def _top_values(x, k, out_ref, want_rank=False):
    cur = x
    rank = jnp.full(x.shape, float(k), F32) if want_rank else None
    for r in range(k):
        m = jnp.max(cur, axis=0, keepdims=True)
        out_ref[r:r + 1, :] = m
        hit = cur == m
        if want_rank:
            rank = jnp.where(hit, float(r), rank)
        if r + 1 < k:
            cur = jnp.where(hit, NEG_BIG, cur)
    return rank


def _peer_select_kernel(q_ref, keys_ref, cnt_ref, f1_ref, r2_ref, e2_ref, top_ref, *, tt, hg):
    k = PEER_TOPK

    def head(h, sub):
        a_ref, b_ref, c_ref = top_ref.at[sub, 0], top_ref.at[sub, 1], top_ref.at[sub, 2]
        qh = q_ref[:, pl.ds(pl.multiple_of(h * PEER_QDIM, PEER_QDIM), PEER_QDIM)]
        s1 = _nt_dot(keys_ref[h, 0], qh[:, :PEER_QHALF], precision=HIGHEST)
        s2 = _nt_dot(keys_ref[h, 1], qh[:, PEER_QHALF:], precision=HIGHEST)
        _top_values(s1, k, a_ref)
        rank2 = _top_values(s2, k, b_ref, want_rank=True)
        av = a_ref[...]
        bv = b_ref[...]
        half = k // 2
        cand = jnp.concatenate([av[0:1] + bv] + [av[r:r + 1] + bv[:half] for r in range(1, half)]
                               + [av[half:] + bv[0:1]], axis=0)
        _top_values(cand, k, c_ref)
        cv = c_ref[...]
        z = jnp.sum(jnp.exp(cv - cv[0:1, :]), axis=0, keepdims=True)
        tau = cv[k - 1:k, :]
        cnt = jnp.zeros(s1.shape, F32)
        for c in range(half):
            ok = av + bv[c:c + 1] >= tau
            thr = jnp.min(jnp.where(ok, av, -NEG_BIG), axis=0, keepdims=True)
            cnt = cnt + jnp.where(s1 >= thr, 1.0, 0.0)
        n_hi = jnp.sum(jnp.where(av[0:1] + bv[half:] >= tau, 1.0, 0.0), axis=0, keepdims=True)
        cnt = cnt + jnp.where(s1 >= av[0:1], n_hi, 0.0)
        cnt_ref[:, pl.ds(h, 1), :] = cnt[:, None, :]
        f1_ref[:, pl.ds(h, 1), :] = (0.5 * jnp.exp(s1 - av[0:1]) / z)[:, None, :]
        rank2 = rank2.astype(r2_ref.dtype)
        e2 = jnp.exp(s2 - bv[0:1]).astype(e2_ref.dtype)
        rb = 2 * SUBLANES
        for g in range(PEER_NKEYS // rb):
            for tl in range(tt // LANES):
                r2_ref[h * (PEER_NKEYS // rb) + g, tl] = rank2[g * rb:(g + 1) * rb, tl * LANES:(tl + 1) * LANES]
                e2_ref[h * (PEER_NKEYS // rb) + g, tl] = e2[g * rb:(g + 1) * rb, tl * LANES:(tl + 1) * LANES]

    def group(p, carry):
        for sub in range(hg):
            head(hg * p + sub, sub)
        return carry

    lax.fori_loop(0, PEER_HEADS // hg, group, 0)


def _peer_select(q, keys, tt=256, hg=4):
    t = q.shape[0]
    tt = _pick(t, tt)
    shape = (PEER_NKEYS, PEER_HEADS, t)
    bspec = pl.BlockSpec((PEER_NKEYS, PEER_HEADS, tt), lambda i: (0, 0, i))
    n_slab = PEER_HEADS * PEER_NKEYS // (2 * SUBLANES)
    flat = (n_slab, t // LANES, 2 * SUBLANES, LANES)
    flat_spec = pl.BlockSpec((n_slab, tt // LANES, 2 * SUBLANES, LANES), lambda i: (0, i, 0, 0))
    return pl.pallas_call(
        functools.partial(_peer_select_kernel, tt=tt, hg=hg),
        grid=(t // tt,),
        in_specs=[pl.BlockSpec((tt, PEER_HEADS * PEER_QDIM), lambda i: (i, 0)),
                  pl.BlockSpec((PEER_HEADS, 2, PEER_NKEYS, PEER_QHALF), lambda i: (0, 0, 0, 0))],
        out_specs=[bspec, bspec, flat_spec, flat_spec],
        out_shape=[jax.ShapeDtypeStruct(shape, F32), jax.ShapeDtypeStruct(shape, F32),
                   jax.ShapeDtypeStruct(flat, BF16), jax.ShapeDtypeStruct(flat, BF16)],
        scratch_shapes=[pltpu.VMEM((hg, 3, PEER_TOPK, tt), F32)],
        compiler_params=_cparams(("parallel",)),
        name="peer_select",
    )(q, keys.astype(F32))


def _peer_dense_kernel(hn_ref, u_ref, vt_ref, cnt_ref, f1_ref, r2_ref, e2_ref, x_ref, o_ref,
                       acc_ref, ht0_ref, ht1_ref, g0_ref, g1_ref, *, tt, ec, nc, d):
    s = pl.program_id(0)
    c_out = lax.rem(jnp.maximum(s - 2, 0), nc)

    @pl.when(s == 0)
    def _():
        for ref in (ht0_ref, ht1_ref, g0_ref, g1_ref):
            ref[...] = jnp.zeros_like(ref)

    @pl.when(c_out == 0)
    def _():
        acc_ref[...] = jnp.zeros_like(acc_ref)

    nk = PEER_NKEYS

    def stages(ht_w, ht_r, g_w, g_r):
        halves = [slice(0, tt // 2), slice(tt // 2, tt)]

        def stage_a(hs):
            ht_w[:, hs] = _nt_dot(u_ref[...], hn_ref[hs, :]).astype(ht_w.dtype)

        def stage_c(hs):
            acc_ref[:, hs] += jnp.dot(vt_ref[...], g_r[:, hs], preferred_element_type=F32)

        mxu_pieces = [functools.partial(stage_a, halves[0]), functools.partial(stage_a, halves[1]),
                      functools.partial(stage_c, halves[0]), functools.partial(stage_c, halves[1])]
        rb = 2 * SUBLANES
        tiles = [(ii, tg) for ii in range(ec // nk) for tg in range(tt // LANES)]
        per_piece = len(tiles) // len(mxu_pieces)
        for t_idx, (ii, tg) in enumerate(tiles):
            if t_idx % per_piece == 0:
                mxu_pieces[t_idx // per_piece]()
            lanes = slice(tg * LANES, (tg + 1) * LANES)
            bcast = lambda ref, h: jnp.broadcast_to(ref[ii, h:h + 1, lanes], (rb, LANES)).astype(BF16)
            cnt = [bcast(cnt_ref, h) for h in range(PEER_HEADS)]
            f1 = [bcast(f1_ref, h) for h in range(PEER_HEADS)]
            for j0 in range(0, nk, rb):
                wsel = None
                for h in range(PEER_HEADS):
                    slab = (h * nk + j0) // rb
                    gate1 = jnp.minimum(jnp.maximum(cnt[h] - r2_ref[slab, tg], 0), f1[h])
                    term = gate1 * e2_ref[slab, tg]
                    wsel = term if wsel is None else wsel + term
                rows = slice(ii * nk + j0, ii * nk + j0 + rb)
                hid = ht_r[rows, lanes]
                inner = hid * (hid * hid * (GELU_C * GELU_CUBIC) + GELU_C)
                g_w[rows, lanes] = wsel * (hid * jnp.tanh(inner) + hid)

    parity = lax.rem(s, 2)
    pl.when(parity == 0)(functools.partial(stages, ht0_ref, ht1_ref, g1_ref, g0_ref))
    pl.when(parity == 1)(functools.partial(stages, ht1_ref, ht0_ref, g0_ref, g1_ref))

    @pl.when((c_out == nc - 1) & (s >= 2))
    def _():
        step = 512
        for d0 in range(0, d, step):
            o_ref[:, d0:d0 + step] = x_ref[:, d0:d0 + step] + acc_ref[d0:d0 + step, :].T


def _peer_dense(hn, u_tab, vt_tab, layer, cntr, f1r, r2, e2, x, tt=512, ec=1024):
    t, d = hn.shape
    e = u_tab.shape[1]
    tt, ec = _pick(t, tt), _pick(e, ec)
    ni = ec // PEER_NKEYS
    nc = e // ec
    n_steps = (t // tt) * nc
    n_slab = r2.shape[0]
    tile = lambda s, lag: jnp.clip(s - lag, 0, n_steps - 1) // nc
    chunk = lambda s, lag: jnp.clip(s - lag, 0, n_steps - 1) % nc
    once = dict(pipeline_mode=pl.Buffered(1))
    return pl.pallas_call(
        functools.partial(_peer_dense_kernel, tt=tt, ec=ec, nc=nc, d=d),
        grid=(n_steps + 2,),
        in_specs=[pl.BlockSpec((tt, d), lambda s: (tile(s, 0), 0), **once),
                  pl.BlockSpec((None, ec, d), lambda s: (layer, chunk(s, 0), 0)),
                  pl.BlockSpec((None, d, ec), lambda s: (layer, 0, chunk(s, 2))),
                  pl.BlockSpec((ni, PEER_HEADS, tt), lambda s: (chunk(s, 1), 0, tile(s, 1))),
                  pl.BlockSpec((ni, PEER_HEADS, tt), lambda s: (chunk(s, 1), 0, tile(s, 1))),
                  pl.BlockSpec((n_slab, tt // LANES, 2 * SUBLANES, LANES), lambda s: (0, tile(s, 1), 0, 0)),
                  pl.BlockSpec((n_slab, tt // LANES, 2 * SUBLANES, LANES), lambda s: (0, tile(s, 1), 0, 0)),
                  pl.BlockSpec((tt, d), lambda s: (tile(s, 2), 0), **once)],
        out_specs=pl.BlockSpec((tt, d), lambda s: (tile(s, 2), 0)),
        out_shape=jax.ShapeDtypeStruct((t, d), F32),
        scratch_shapes=[pltpu.VMEM((d, tt), F32)] + [pltpu.VMEM((ec, tt), BF16)] * 4,
        compiler_params=_cparams(("arbitrary",)),
        name="peer_dense",
    )(hn, u_tab, vt_tab, cntr, f1r, r2, e2, x)


def _layer(x, batch, seq, layer, w_in, conv_w, a_log, dt_bias, a_out_gain, gmlp_norm, w_spatial, b_spatial,
           c_q_gain, c_k_gain, rel_bias, p_a, p_b, p_c, w_out, norm_mix, norm_ffn,
           peer_wq, peer_keys, peer_u, peer_vt):
    o_qkv_a = 0
    o_z = o_qkv_a + 3 * A_WIDTH
    o_beta = o_z + A_WIDTH
    o_uv = o_beta + 2 * A_HEADS
    o_qkv_c = o_uv + 2 * GMLP_WIDTH
    o_gate = o_qkv_c + 3 * C_WIDTH
    d_model = x.shape[1]

    h = _rmsnorm(x, norm_mix)
    proj = lambda col0, n, dtype, name: _matmul(h, w_in, layer, col0, n, dtype, w_is_nk=True, name=name)
    qkv_a = proj(o_qkv_a, 3 * A_WIDTH, BF16, "proj_qkv_a")
    z_a = proj(o_z, A_WIDTH, BF16, "proj_z_a")
    ba = proj(o_beta, LANES, F32, "proj_beta_alpha")
    uv_b = proj(o_uv, 2 * GMLP_WIDTH, BF16, "proj_uv_b")
    qkv_c = proj(o_qkv_c, 3 * C_WIDTH, BF16, "proj_qkv_c")
    gate = proj(o_gate, N_BRANCH * d_model, BF16, "proj_gate")

    qkv_prep = _conv_prep(qkv_a, conv_w.astype(F32), batch, seq)
    gcum, beta, dec = _gates(ba, a_log, dt_bias)
    u, wq, kd, at, egl = _delta_solve(qkv_prep, gcum, beta, dec)
    y_a = _delta_rec(u, wq, kd, at, egl, z_a, a_out_gain, batch, seq)
    y_b = _gmlp(uv_b, gmlp_norm, w_spatial, b_spatial)
    y_c = _band_attention(qkv_c, c_q_gain, c_k_gain, rel_bias, batch, seq)

    merged = _merge(y_a, y_b, y_c, p_a, p_b, p_c, layer, gate)
    x = _matmul(merged, w_out, layer, 0, d_model, F32, residual=x, name="out_proj")

    hn = _rmsnorm(x, norm_ffn)
    q = _matmul(hn, peer_wq, layer, 0, peer_wq.shape[2], F32, name="peer_query")
    cnt, f1, r2, e2 = _peer_select(q, peer_keys)
    return _peer_dense(hn, peer_u, peer_vt, layer, cnt, f1, r2, e2, x)


def kernel(x, w_in, conv_w, a_log, dt_bias, a_out_gain, gmlp_norm, w_spatial, b_spatial, c_q_gain, c_k_gain,
           rel_bias, p_a, p_b, p_c, w_out, norm_mix, norm_ffn, peer_wq, peer_keys, peer_u, peer_v):
    batch, seq, d_model = x.shape
    xt = x.reshape(batch * seq, d_model)
    w_in_t = jnp.swapaxes(w_in, 1, 2)
    peer_u16 = peer_u.astype(BF16)
    peer_vt16 = jnp.swapaxes(peer_v.astype(BF16), 1, 2)
    for l in range(w_in.shape[0]):
        xt = _layer(xt, batch, seq, l, w_in_t, conv_w[l], a_log[l], dt_bias[l], a_out_gain[l], gmlp_norm[l],
                    w_spatial[l], b_spatial[l], c_q_gain[l], c_k_gain[l], rel_bias[l], p_a, p_b, p_c,
                    w_out, norm_mix[l], norm_ffn[l], peer_wq, peer_keys[l], peer_u16, peer_vt16)
    return xt.reshape(batch, seq, d_model)
```
